```python
import jax, jax.numpy as jnp
from jax import lax
import numpy as np

D_MODEL = 4096
BATCH = 1
SEQ = 8192
DEPTH = 4

N_Q_HEADS = 16
N_KV_GROUPS = 4
HEAD_DIM = 128
Q_PER_KV = N_Q_HEADS // N_KV_GROUPS
ATTN_WIDTH = N_Q_HEADS * HEAD_DIM
KV_WIDTH = N_KV_GROUPS * HEAD_DIM
N_NSA_BRANCHES = 3
ROT_DIM = HEAD_DIM // 4
ROPE_THETA = 500000.0
CMP_LEN = 32
CMP_STRIDE = 16
SEL_BLOCK = 64
SEL_TOP = 16
N_LOCAL_SEL = 2
WINDOW = 512
Q_BLOCK = 128
FORCE_SCORE = 2.0 * Q_PER_KV + 1.0
RNN_WIDTH = 2048
RNN_BLOCKS = 16
RNN_BLOCK_DIM = RNN_WIDTH // RNN_BLOCKS
RNN_CONV = 4
RG_C = 8.0
D_FF = 2 * D_MODEL
FFN_CONV = 3
COND_RANK = 512
N_MOD = 6
EPS = 1e-6
NEG_INF = -1e30

SPLIT_SIZES = (ATTN_WIDTH, KV_WIDTH, KV_WIDTH, KV_WIDTH, KV_WIDTH, KV_WIDTH, KV_WIDTH,
               N_Q_HEADS * N_NSA_BRANCHES, RNN_WIDTH, RNN_WIDTH, D_MODEL, D_MODEL)
N_IN = ATTN_WIDTH + 6 * KV_WIDTH + N_Q_HEADS * N_NSA_BRANCHES + 2 * RNN_WIDTH + 2 * D_MODEL

kernel_name = 'hybrid_nsa_rglru_convffn_adaln'


def rms_norm(x, gain):
    x32 = x.astype(jnp.float32)
    y = x32 * lax.rsqrt(jnp.mean(x32 * x32, axis=-1, keepdims=True) + EPS)
    return (y * gain.astype(jnp.float32)).astype(x.dtype)


def rope_tables(positions):
    inv_freq = ROPE_THETA ** (-jnp.arange(0, ROT_DIM, 2, dtype=jnp.float32) / ROT_DIM)
    ang = positions.astype(jnp.float32)[..., None] * inv_freq
    return jnp.cos(ang)[:, :, None, :], jnp.sin(ang)[:, :, None, :]


def apply_partial_rope(x, cos, sin):
    xr = x[..., :ROT_DIM].astype(jnp.float32)
    x1, x2 = xr[..., :ROT_DIM // 2], xr[..., ROT_DIM // 2:]
    rot = jnp.concatenate([x1 * cos - x2 * sin, x2 * cos + x1 * sin], axis=-1)
    return jnp.concatenate([rot.astype(x.dtype), x[..., ROT_DIM:]], axis=-1)


def causal_depthwise_conv(x, w, b):
    k_w = w.shape[0]
    s = x.shape[1]
    xp = jnp.pad(x, ((0, 0), (k_w - 1, 0), (0, 0)))
    y = b
    for j in range(k_w):
        y = y + xp[:, j:j + s] * w[j]
    return y


def masked_softmax(s, mask):
    s = jnp.where(mask, s, NEG_INF)
    m = jnp.max(s, axis=-1, keepdims=True)
    p = jnp.exp(s - m) * mask
    return p / jnp.maximum(jnp.sum(p, axis=-1, keepdims=True), 1e-30)


def linear_scan(a, b):
    def combine(l, r):
        a_l, b_l = l
        a_r, b_r = r
        return a_l * a_r, a_r * b_l + b_r
    _, h = lax.associative_scan(combine, (a, b), axis=1)
    return h


def compress(k, pe, w):
    s = k.shape[1]
    n_cmp = (s - CMP_LEN) // CMP_STRIDE + 1
    idx = jnp.arange(n_cmp)[:, None] * CMP_STRIDE + jnp.arange(CMP_LEN)[None, :]
    kb = k[:, idx] + pe[None, None, :, None, :]
    return jnp.einsum('bnlgd,lde->bgne', kb, w)


def nsa_attention(q, k_cmp, v_cmp, k_sel, v_sel, k_win, v_win, gates):
    b, s = q.shape[0], q.shape[1]
    n_cmp = k_cmp.shape[2]
    n_sel = s // SEL_BLOCK
    n_top = min(SEL_TOP, n_sel)
    n_qblk = s // Q_BLOCK
    scale = HEAD_DIM ** -0.5
    q_g = q.reshape(b, s, N_KV_GROUPS, Q_PER_KV, HEAD_DIM).transpose(0, 2, 3, 1, 4)
    g_g = gates.reshape(b, s, N_KV_GROUPS, Q_PER_KV, N_NSA_BRANCHES).transpose(0, 2, 3, 1, 4)
    ks_b = k_sel.transpose(0, 2, 1, 3).reshape(b, N_KV_GROUPS, n_sel, SEL_BLOCK, HEAD_DIM)
    vs_b = v_sel.transpose(0, 2, 1, 3).reshape(b, N_KV_GROUPS, n_sel, SEL_BLOCK, HEAD_DIM)
    pad = ((0, 0), (0, 0), (WINDOW, 0), (0, 0))
    kw_p = jnp.pad(k_win.transpose(0, 2, 1, 3), pad)
    vw_p = jnp.pad(v_win.transpose(0, 2, 1, 3), pad)
    cmp_start = np.arange(n_cmp) * CMP_STRIDE
    sel_start = np.arange(n_sel) * SEL_BLOCK
    ov = np.clip(np.minimum(cmp_start[:, None] + CMP_LEN, sel_start[None, :] + SEL_BLOCK)
                 - np.maximum(cmp_start[:, None], sel_start[None, :]), 0, None) / CMP_LEN
    overlap = jnp.asarray(ov.astype(np.float32))
    cmp_end = jnp.asarray((cmp_start + CMP_LEN - 1).astype(np.int32))
    sel_ids = jnp.arange(n_sel)
    b_ids = jnp.arange(b)[:, None, None, None]
    g_ids = jnp.arange(N_KV_GROUPS)[None, :, None, None]
    win_off = jnp.arange(Q_BLOCK + WINDOW)
    sel_off = jnp.arange(SEL_BLOCK)

    def block(qb):
        start = qb * Q_BLOCK
        t = start + jnp.arange(Q_BLOCK)
        qblk = lax.dynamic_slice_in_dim(q_g, start, Q_BLOCK, axis=3)
        gblk = lax.dynamic_slice_in_dim(g_g, start, Q_BLOCK, axis=3)
        s_c = jnp.einsum('bgzqd,bgnd->bgzqn', qblk, k_cmp).astype(jnp.float32) * scale
        p_c = masked_softmax(s_c, cmp_end[None, :] <= t[:, None])
        o_c = jnp.einsum('bgzqn,bgnd->bgzqd', p_c.astype(v_cmp.dtype), v_cmp)
        imp = jnp.einsum('bgzqn,nj->bgqj', p_c, overlap)
        cur = (t // SEL_BLOCK)[:, None]
        valid = sel_ids[None, :] <= cur
        forced = (sel_ids[None, :] == 0) | (valid & (sel_ids[None, :] > cur - N_LOCAL_SEL))
        score = jnp.where(forced, FORCE_SCORE, jnp.where(valid, imp, -1.0))
        _, idx = lax.top_k(score, n_top)
        kg = ks_b[b_ids, g_ids, idx].reshape(b, N_KV_GROUPS, Q_BLOCK, n_top * SEL_BLOCK, HEAD_DIM)
        vg = vs_b[b_ids, g_ids, idx].reshape(b, N_KV_GROUPS, Q_BLOCK, n_top * SEL_BLOCK, HEAD_DIM)
        pos_s = (idx[..., None] * SEL_BLOCK + sel_off).reshape(b, N_KV_GROUPS, Q_BLOCK, n_top * SEL_BLOCK)
        mask_s = (pos_s <= t[None, None, :, None])[:, :, None]
        s_s = jnp.einsum('bgzqd,bgqkd->bgzqk', qblk, kg).astype(jnp.float32) * scale
        p_s = masked_softmax(s_s, mask_s)
        o_s = jnp.einsum('bgzqk,bgqkd->bgzqd', p_s.astype(vg.dtype), vg)
        kw = lax.dynamic_slice_in_dim(kw_p, start, Q_BLOCK + WINDOW, axis=2)
        vw = lax.dynamic_slice_in_dim(vw_p, start, Q_BLOCK + WINDOW, axis=2)
        pos_w = (start - WINDOW + win_off)[None, :]
        mask_w = (pos_w <= t[:, None]) & (pos_w > t[:, None] - WINDOW) & (pos_w >= 0)
        s_w = jnp.einsum('bgzqd,bgkd->bgzqk', qblk, kw).astype(jnp.float32) * scale
        p_w = masked_softmax(s_w, mask_w)
        o_w = jnp.einsum('bgzqk,bgkd->bgzqd', p_w.astype(vw.dtype), vw)
        o = gblk[..., 0:1] * o_c + gblk[..., 1:2] * o_s + gblk[..., 2:3] * o_w
        return o.transpose(0, 3, 1, 2, 4).reshape(b, Q_BLOCK, ATTN_WIDTH)

    out = lax.map(block, jnp.arange(n_qblk))
    return out.transpose(1, 0, 2, 3).reshape(b, s, ATTN_WIDTH)


def rg_lru(x, w_a, b_a, w_x, b_x, lam):
    b, s, _ = x.shape
    xb = x.reshape(b, s, RNN_BLOCKS, RNN_BLOCK_DIM)
    r = jax.nn.sigmoid(jnp.einsum('bshi,hij->bshj', xb, w_a).reshape(b, s, RNN_WIDTH) + b_a)
    i = jax.nn.sigmoid(jnp.einsum('bshi,hij->bshj', xb, w_x).reshape(b, s, RNN_WIDTH) + b_x)
    log_a = -RG_C * r.astype(jnp.float32) * jax.nn.softplus(-lam.astype(jnp.float32))
    a = jnp.exp(log_a)
    bt = jnp.sqrt(-jnp.expm1(2.0 * log_a)) * (i * x).astype(jnp.float32)
    return linear_scan(a, bt).astype(x.dtype)


def hybrid_mixer(u, cos, sin, w_in, q_norm, k_norm, cmp_pe_k, cmp_w_k, cmp_pe_v, cmp_w_v,
                 rnn_conv_w, rnn_conv_b, rg_w_a, rg_b_a, rg_w_x, rg_b_x, rg_lambda,
                 w_attn_up, w_rnn_up, w_out):
    b, s, _ = u.shape
    points = [int(p) for p in np.cumsum(SPLIT_SIZES)[:-1]]
    (q, kc, vc, ks_, vs_, kw, vw, g_nsa, rx, ry, g_attn, g_rnn) = jnp.split(u @ w_in, points, axis=-1)

    def heads(t, n):
        return t.reshape(b, s, n, HEAD_DIM)

    q = apply_partial_rope(rms_norm(heads(q, N_Q_HEADS), q_norm), cos, sin)
    kc = rms_norm(compress(apply_partial_rope(heads(kc, N_KV_GROUPS), cos, sin), cmp_pe_k, cmp_w_k), k_norm[0])
    vc = compress(heads(vc, N_KV_GROUPS), cmp_pe_v, cmp_w_v)
    ks_ = apply_partial_rope(rms_norm(heads(ks_, N_KV_GROUPS), k_norm[1]), cos, sin)
    kw = apply_partial_rope(rms_norm(heads(kw, N_KV_GROUPS), k_norm[2]), cos, sin)
    gates = jax.nn.sigmoid(g_nsa).reshape(b, s, N_Q_HEADS, N_NSA_BRANCHES)
    attn = nsa_attention(q, kc, vc, ks_, heads(vs_, N_KV_GROUPS), kw, heads(vw, N_KV_GROUPS), gates)
    xr = causal_depthwise_conv(rx, rnn_conv_w, rnn_conv_b)
    rnn = rg_lru(xr, rg_w_a, rg_b_a, rg_w_x, rg_b_x, rg_lambda) * jax.nn.gelu(ry, approximate=True)
    merged = jax.nn.sigmoid(g_attn) * (attn @ w_attn_up) + jax.nn.sigmoid(g_rnn) * (rnn @ w_rnn_up)
    return merged @ w_out


def conv_ffn(u, w_ffn_in, conv_w, conv_b, w_down):
    gate, up = jnp.split(u @ w_ffn_in, 2, axis=-1)
    gate = causal_depthwise_conv(gate, conv_w, conv_b)
    return (jax.nn.silu(gate) * up) @ w_down


def setup_inputs(seed: int = 0) -> dict:
    key = jax.random.key(seed)
    ks = jax.random.split(key, 32)
    f32 = jnp.float32
    L = DEPTH

    def nrm(k, shape, scale):
        return jax.random.normal(k, shape, f32) * scale

    lam_u = jax.random.uniform(ks[20], (L, RNN_WIDTH), f32, 0.9, 0.999)
    a_base = lam_u ** (1.0 / RG_C)
    rg_lambda = jnp.log(a_base) - jnp.log1p(-a_base)
    return {
        'x': nrm(ks[0], (BATCH, SEQ, D_MODEL), 1.0),
        'c': nrm(ks[1], (BATCH, D_MODEL), 1.0),
        'positions': jnp.broadcast_to(jnp.arange(SEQ, dtype=jnp.int32), (BATCH, SEQ)),
        'w_cond': nrm(ks[2], (D_MODEL, COND_RANK), D_MODEL ** -0.5),
        'b_cond': nrm(ks[3], (COND_RANK,), 0.01),
        'w_mod': nrm(ks[4], (L, COND_RANK, N_MOD * D_MODEL), 0.5 * COND_RANK ** -0.5),
        'b_mod': nrm(ks[5], (L, N_MOD * D_MODEL), 0.01),
        'norm_mix': 1.0 + nrm(ks[6], (L, D_MODEL), 0.02),
        'norm_ffn': 1.0 + nrm(ks[7], (L, D_MODEL), 0.02),
        'w_in': nrm(ks[8], (L, D_MODEL, N_IN), D_MODEL ** -0.5),
        'q_norm': 1.0 + nrm(ks[9], (L, HEAD_DIM), 0.02),
        'k_norm': 1.0 + nrm(ks[10], (L, N_NSA_BRANCHES, HEAD_DIM), 0.02),
        'cmp_pe_k': nrm(ks[11], (L, CMP_LEN, HEAD_DIM), 0.1),
        'cmp_w_k': nrm(ks[12], (L, CMP_LEN, HEAD_DIM, HEAD_DIM), (CMP_LEN * HEAD_DIM) ** -0.5),
        'cmp_pe_v': nrm(ks[13], (L, CMP_LEN, HEAD_DIM), 0.1),
        'cmp_w_v': nrm(ks[14], (L, CMP_LEN, HEAD_DIM, HEAD_DIM), (CMP_LEN * HEAD_DIM) ** -0.5),
        'rnn_conv_w': nrm(ks[15], (L, RNN_CONV, RNN_WIDTH), RNN_CONV ** -0.5),
        'rnn_conv_b': nrm(ks[16], (L, RNN_WIDTH), 0.01),
        'rg_w_a': nrm(ks[17], (L, RNN_BLOCKS, RNN_BLOCK_DIM, RNN_BLOCK_DIM), RNN_BLOCK_DIM ** -0.5),
        'rg_b_a': nrm(ks[18], (L, RNN_WIDTH), 0.01),
        'rg_w_x': nrm(ks[19], (L, RNN_BLOCKS, RNN_BLOCK_DIM, RNN_BLOCK_DIM), RNN_BLOCK_DIM ** -0.5),
        'rg_b_x': nrm(ks[21], (L, RNN_WIDTH), 0.01),
        'rg_lambda': rg_lambda,
        'w_attn_up': nrm(ks[22], (L, ATTN_WIDTH, D_MODEL), ATTN_WIDTH ** -0.5),
        'w_rnn_up': nrm(ks[23], (L, RNN_WIDTH, D_MODEL), RNN_WIDTH ** -0.5),
        'w_out': nrm(ks[24], (L, D_MODEL, D_MODEL), D_MODEL ** -0.5),
        'w_ffn_in': nrm(ks[25], (L, D_MODEL, 2 * D_FF), D_MODEL ** -0.5),
        'ffn_conv_w': nrm(ks[26], (L, FFN_CONV, D_FF), FFN_CONV ** -0.5),
        'ffn_conv_b': nrm(ks[27], (L, D_FF), 0.01),
        'w_ffn_down': nrm(ks[28], (L, D_FF, D_MODEL), D_FF ** -0.5),
    }


def reference(x, c, positions, w_cond, b_cond, w_mod, b_mod, norm_mix, norm_ffn, w_in, q_norm, k_norm,
              cmp_pe_k, cmp_w_k, cmp_pe_v, cmp_w_v, rnn_conv_w, rnn_conv_b, rg_w_a, rg_b_a, rg_w_x,
              rg_b_x, rg_lambda, w_attn_up, w_rnn_up, w_out, w_ffn_in, ffn_conv_w, ffn_conv_b, w_ffn_down):
    cos, sin = rope_tables(positions)
    c_emb = jax.nn.silu(c @ w_cond + b_cond)
    h = x
    for l in range(DEPTH):
        mod = c_emb @ w_mod[l] + b_mod[l]
        sh1, sc1, g1, sh2, sc2, g2 = jnp.split(mod[:, None, :], N_MOD, axis=-1)
        u = rms_norm(h, norm_mix[l]) * (1.0 + sc1) + sh1
        h = h + g1 * hybrid_mixer(u, cos, sin, w_in[l], q_norm[l], k_norm[l], cmp_pe_k[l], cmp_w_k[l],
                                  cmp_pe_v[l], cmp_w_v[l], rnn_conv_w[l], rnn_conv_b[l], rg_w_a[l], rg_b_a[l],
                                  rg_w_x[l], rg_b_x[l], rg_lambda[l], w_attn_up[l], w_rnn_up[l], w_out[l])
        u = rms_norm(h, norm_ffn[l]) * (1.0 + sc2) + sh2
        h = h + g2 * conv_ffn(u, w_ffn_in[l], ffn_conv_w[l], ffn_conv_b[l], w_ffn_down[l])
    return h
```

```python
import functools

import numpy as np
import jax
import jax.numpy as jnp
from jax import lax
from jax.experimental import pallas as pl
from jax.experimental.pallas import tpu as pltpu

F32 = jnp.float32
BF16 = jnp.bfloat16

D_MODEL = 4096
N_Q_HEADS = 16
N_KV_GROUPS = 4
HEAD_DIM = 128
Q_PER_KV = N_Q_HEADS // N_KV_GROUPS
ATTN_WIDTH = N_Q_HEADS * HEAD_DIM
KV_WIDTH = N_KV_GROUPS * HEAD_DIM
N_NSA_BRANCHES = 3
ROT_DIM = HEAD_DIM // 4
ROPE_THETA = 500000.0
CMP_LEN = 32
CMP_STRIDE = 16
SEL_BLOCK = 64
SEL_TOP = 16
N_LOCAL_SEL = 2
WINDOW = 512
FORCE_SCORE = 2.0 * Q_PER_KV + 1.0
RNN_WIDTH = 2048
RNN_BLOCKS = 16
RNN_BLOCK_DIM = RNN_WIDTH // RNN_BLOCKS
RNN_CONV = 4
RG_C = 8.0
D_FF = 2 * D_MODEL
FFN_CONV = 3
N_MOD = 6
EPS = 1e-6
NEG_INF = -1e30
ATTN_SCALE = HEAD_DIM ** -0.5

LANES = 128
SUBLANES = 8
VMEM_LIMIT_BYTES = 56 * 1024 * 1024

COL_Q = 0
COL_KC = ATTN_WIDTH
COL_VC = COL_KC + KV_WIDTH
COL_KS = COL_VC + KV_WIDTH
COL_VS = COL_KS + KV_WIDTH
COL_KW = COL_VS + KV_WIDTH
COL_VW = COL_KW + KV_WIDTH
COL_RX = COL_VW + KV_WIDTH
COL_RY = COL_RX + RNN_WIDTH
COL_GA = COL_RY + RNN_WIDTH
COL_GR = COL_GA + D_MODEL
N_PROJ = COL_GR + D_MODEL
GATE_SRC = ATTN_WIDTH + 6 * KV_WIDTH

Q_TILE = 128
K_TILE = 512
N_WIN_BLK = WINDOW // Q_TILE + 1
NORM_CHUNK = 64


def _params(*sem):
    return pltpu.CompilerParams(dimension_semantics=sem, vmem_limit_bytes=VMEM_LIMIT_BYTES)


def _vecmat_kernel(x_ref, w_ref, b_ref, o_ref, *, silu):
    y = jnp.sum(w_ref[...] * x_ref[...], axis=0, keepdims=True) + b_ref[...]
    if silu:
        y = y * jax.nn.sigmoid(y)
    o_ref[...] = y


def _vecmat(x_col, w, b, *, silu, tn):
    n_l, k, n = w.shape
    return pl.pallas_call(
        functools.partial(_vecmat_kernel, silu=silu),
        out_shape=jax.ShapeDtypeStruct((n_l, 1, n), F32),
        grid=(n_l, n // tn),
        in_specs=[pl.BlockSpec((None, k, 1), lambda l, j: (l, 0, 0)),
                  pl.BlockSpec((None, k, tn), lambda l, j: (l, 0, j)),
                  pl.BlockSpec((None, 1, tn), lambda l, j: (l, 0, j))],
        out_specs=pl.BlockSpec((None, 1, tn), lambda l, j: (l, 0, j)),
        compiler_params=_params("parallel", "parallel"),
        name="vecmat",
    )(x_col, w, b)


def _rope_table_kernel(pos_ref, freq_ref, c_ref, sa_ref, sb_ref):
    ang = pos_ref[...].astype(F32) * freq_ref[...]
    lane = lax.broadcasted_iota(jnp.int32, ang.shape, 1)
    cos = jnp.cos(ang)
    sin = jnp.sin(ang)
    c_ref[...] = jnp.where(lane < ROT_DIM, cos, 1.0)
    sa_ref[...] = jnp.where(lane < ROT_DIM // 2, -sin, 0.0)
    sb_ref[...] = jnp.where((lane >= ROT_DIM // 2) & (lane < ROT_DIM), sin, 0.0)


def _rope_tables(pos_col, freq_row, tr):
    s = pos_col.shape[0]
    spec = pl.BlockSpec((tr, LANES), lambda i: (i, 0))
    return pl.pallas_call(
        _rope_table_kernel,
        out_shape=[jax.ShapeDtypeStruct((s, LANES), F32)] * 3,
        grid=(s // tr,),
        in_specs=[pl.BlockSpec((tr, 1), lambda i: (i, 0)),
                  pl.BlockSpec((1, LANES), lambda i: (0, 0))],
        out_specs=[spec, spec, spec],
        compiler_params=_params("parallel"),
        name="rope_tables",
    )(pos_col, freq_row)


def _norm_mm_kernel(*refs, with_gates):
    if with_gates:
        h_ref, gain_ref, sc_ref, sh_ref, w_ref, wg_ref, o_ref, og_ref, u_ref = refs
    else:
        h_ref, gain_ref, sc_ref, sh_ref, w_ref, o_ref, u_ref = refs

    @pl.when(pl.program_id(1) == 0)
    def _():
        for r0 in range(0, h_ref.shape[0], NORM_CHUNK):
            x = h_ref[r0:r0 + NORM_CHUNK, :]
            y = x * lax.rsqrt(jnp.mean(x * x, axis=-1, keepdims=True) + EPS)
            u = (y * gain_ref[...]) * (1.0 + sc_ref[...]) + sh_ref[...]
            u_ref[r0:r0 + NORM_CHUNK, :] = u.astype(BF16)
        if with_gates:
            og_ref[...] = jnp.dot(u_ref[...], wg_ref[...], preferred_element_type=F32)

    o_ref[...] = jnp.dot(u_ref[...], w_ref[...], preferred_element_type=F32).astype(o_ref.dtype)


def _norm_matmul(h, gain, sc, sh, w, wg=None, *, tm, tn, out_dtype):
    s, d = h.shape
    n = w.shape[1]
    row = pl.BlockSpec((1, d), lambda i, j: (0, 0))
    in_specs = [pl.BlockSpec((tm, d), lambda i, j: (i, 0)), row, row, row,
                pl.BlockSpec((d, tn), lambda i, j: (0, j))]
    out_shape = [jax.ShapeDtypeStruct((s, n), out_dtype)]
    out_specs = [pl.BlockSpec((tm, tn), lambda i, j: (i, j))]
    args = [h, gain, sc, sh, w]
    if wg is not None:
        ng = wg.shape[1]
        in_specs.append(pl.BlockSpec((d, ng), lambda i, j: (0, 0)))
        out_shape.append(jax.ShapeDtypeStruct((s, ng), F32))
        out_specs.append(pl.BlockSpec((tm, ng), lambda i, j: (i, 0)))
        args.append(wg)
    return pl.pallas_call(
        functools.partial(_norm_mm_kernel, with_gates=wg is not None),
        out_shape=out_shape,
        grid=(s // tm, n // tn),
        in_specs=in_specs,
        out_specs=out_specs,
        scratch_shapes=[pltpu.VMEM((tm, d), BF16)],
        compiler_params=_params("parallel", "arbitrary"),
        name="norm_matmul",
    )(*args)


def _mm_res_kernel(a_ref, w_ref, h_ref, g_ref, o_ref):
    y = jnp.dot(a_ref[...], w_ref[...], preferred_element_type=F32)
    o_ref[...] = h_ref[...] + g_ref[...] * y


def _matmul_residual(a, w, h, g, *, tm, tn):
    s, k = a.shape
    n = w.shape[1]
    return pl.pallas_call(
        _mm_res_kernel,
        out_shape=jax.ShapeDtypeStruct((s, n), F32),
        grid=(s // tm, n // tn),
        in_specs=[pl.BlockSpec((tm, k), lambda i, j: (i, 0)),
                  pl.BlockSpec((k, tn), lambda i, j: (0, j)),
                  pl.BlockSpec((tm, tn), lambda i, j: (i, j)),
                  pl.BlockSpec((1, tn), lambda i, j: (0, j))],
        out_specs=pl.BlockSpec((tm, tn), lambda i, j: (i, j)),
        compiler_params=_params("parallel", "parallel"),
        name="matmul_residual",
    )(a, w, h, g)


def _merge_kernel(attn_ref, rnn_ref, wa_ref, wr_ref, ga_ref, gr_ref, o_ref):
    ya = jnp.dot(attn_ref[...], wa_ref[...], preferred_element_type=F32)
    yr = jnp.dot(rnn_ref[...], wr_ref[...], preferred_element_type=F32)
    o_ref[...] = (jax.nn.sigmoid(ga_ref[...]) * ya + jax.nn.sigmoid(gr_ref[...]) * yr).astype(o_ref.dtype)


def _merge(attn, rnn, wa, wr, proj, *, tm, tn):
    s, k = attn.shape
    n = wa.shape[1]
    ga_blk = COL_GA // tn
    gr_blk = COL_GR // tn
    return pl.pallas_call(
        _merge_kernel,
        out_shape=jax.ShapeDtypeStruct((s, n), BF16),
        grid=(s // tm, n // tn),
        in_specs=[pl.BlockSpec((tm, k), lambda i, j: (i, 0)),
                  pl.BlockSpec((tm, k), lambda i, j: (i, 0)),
                  pl.BlockSpec((k, tn), lambda i, j: (0, j)),
                  pl.BlockSpec((k, tn), lambda i, j: (0, j)),
                  pl.BlockSpec((tm, tn), lambda i, j: (i, ga_blk + j)),
                  pl.BlockSpec((tm, tn), lambda i, j: (i, gr_blk + j))],
        out_specs=pl.BlockSpec((tm, tn), lambda i, j: (i, j)),
        compiler_params=_params("parallel", "parallel"),
        name="merge",
    )(attn, rnn, wa, wr, proj, proj)


def _head_norm(x, gain):
    return x * lax.rsqrt(jnp.mean(x * x, axis=-1, keepdims=True) + EPS) * gain


def _rope(x, c, sa, sb):
    return (x * c + pltpu.roll(x, LANES - ROT_DIM // 2, axis=1) * sa
            + pltpu.roll(x, ROT_DIM // 2, axis=1) * sb)


def _prep_kernel(p_ref, graw_ref, c_ref, sa_ref, sb_ref, qn_ref, kn_ref,
                 q_ref, kc_ref, ks_ref, vs_ref, kw_ref, vw_ref, g_ref):
    c, sa, sb = c_ref[...], sa_ref[...], sb_ref[...]
    qn = qn_ref[...]
    for hd in range(N_Q_HEADS):
        cols = slice(hd * HEAD_DIM, (hd + 1) * HEAD_DIM)
        q_ref[:, cols] = _rope(_head_norm(p_ref[:, cols], qn), c, sa, sb).astype(BF16)
    for g in range(N_KV_GROUPS):
        cols = slice(g * HEAD_DIM, (g + 1) * HEAD_DIM)

        def src(base):
            return p_ref[:, base + g * HEAD_DIM:base + (g + 1) * HEAD_DIM]

        kc_ref[:, cols] = _rope(src(COL_KC), c, sa, sb)
        ks_ref[:, cols] = _rope(_head_norm(src(COL_KS), kn_ref[1:2, :]), c, sa, sb).astype(BF16)
        kw_ref[:, cols] = _rope(_head_norm(src(COL_KW), kn_ref[2:3, :]), c, sa, sb).astype(BF16)
        vs_ref[:, cols] = src(COL_VS).astype(BF16)
        vw_ref[:, cols] = src(COL_VW).astype(BF16)
    g_ref[...] = jax.nn.sigmoid(graw_ref[...])


def _prep(proj, graw, c, sa, sb, qn, kn, *, tr):
    s = proj.shape[0]
    ng = graw.shape[1]
    tab = pl.BlockSpec((tr, LANES), lambda i: (i, 0))
    kv_spec = pl.BlockSpec((tr, KV_WIDTH), lambda i: (i, 0))
    return pl.pallas_call(
        _prep_kernel,
        out_shape=[jax.ShapeDtypeStruct((s, ATTN_WIDTH), BF16),
                   jax.ShapeDtypeStruct((s, KV_WIDTH), F32),
                   jax.ShapeDtypeStruct((s, KV_WIDTH), BF16),
                   jax.ShapeDtypeStruct((s, KV_WIDTH), BF16),
                   jax.ShapeDtypeStruct((s, KV_WIDTH), BF16),
                   jax.ShapeDtypeStruct((s, KV_WIDTH), BF16),
                   jax.ShapeDtypeStruct((s, ng), F32)],
        grid=(s // tr,),
        in_specs=[pl.BlockSpec((tr, COL_RX), lambda i: (i, 0)),
                  pl.BlockSpec((tr, ng), lambda i: (i, 0)),
                  tab, tab, tab,
                  pl.BlockSpec((1, HEAD_DIM), lambda i: (0, 0)),
                  pl.BlockSpec((N_NSA_BRANCHES, HEAD_DIM), lambda i: (0, 0))],
        out_specs=[pl.BlockSpec((tr, ATTN_WIDTH), lambda i: (i, 0)),
                   kv_spec, kv_spec, kv_spec, kv_spec, kv_spec,
                   pl.BlockSpec((tr, ng), lambda i: (i, 0))],
        compiler_params=_params("parallel"),
        name="qk_prep",
    )(proj, graw, c, sa, sb, qn, kn)


def _compress_kernel(x_ref, pe_ref, w_ref, gain_ref, o_ref, shift_ref, *, do_norm, n_chunks):
    acc_a = jnp.zeros((n_chunks, HEAD_DIM), F32)
    acc_b = jnp.zeros((n_chunks, HEAD_DIM), F32)
    for l in range(CMP_STRIDE):
        xl = x_ref[pl.ds(l, n_chunks, stride=CMP_STRIDE), :]
        xa = (xl + pe_ref[l:l + 1, :]).astype(BF16)
        xb = (xl + pe_ref[CMP_STRIDE + l:CMP_STRIDE + l + 1, :]).astype(BF16)
        acc_a = acc_a + jnp.dot(xa, w_ref[l], preferred_element_type=F32)
        acc_b = acc_b + jnp.dot(xb, w_ref[CMP_STRIDE + l], preferred_element_type=F32)
    shift_ref[0:n_chunks, :] = acc_b
    shift_ref[n_chunks:n_chunks + SUBLANES, :] = jnp.zeros((SUBLANES, HEAD_DIM), F32)
    out = acc_a + shift_ref[1:n_chunks + 1, :]
    if do_norm:
        out = _head_norm(out, gain_ref[...])
    o_ref[...] = out.astype(o_ref.dtype)


def _compress(x, col_blk0, pe, w, gain, *, do_norm):
    s = x.shape[0]
    n_chunks = s // CMP_STRIDE
    return pl.pallas_call(
        functools.partial(_compress_kernel, do_norm=do_norm, n_chunks=n_chunks),
        out_shape=jax.ShapeDtypeStruct((N_KV_GROUPS, n_chunks, HEAD_DIM), BF16),
        grid=(N_KV_GROUPS,),
        in_specs=[pl.BlockSpec((s, HEAD_DIM), lambda g: (0, col_blk0 + g)),
                  pl.BlockSpec((CMP_LEN, HEAD_DIM), lambda g: (0, 0)),
                  pl.BlockSpec((CMP_LEN, HEAD_DIM, HEAD_DIM), lambda g: (0, 0, 0)),
                  pl.BlockSpec((1, HEAD_DIM), lambda g: (0, 0))],
        out_specs=pl.BlockSpec((None, n_chunks, HEAD_DIM), lambda g: (g, 0, 0)),
        scratch_shapes=[pltpu.VMEM((n_chunks + SUBLANES, HEAD_DIM), F32)],
        compiler_params=_params("parallel"),
        name="compress",
    )(x, pe, w, gain)


def _dot_nt(a, b):
    return lax.dot_general(a, b, (((1,), (1,)), ((), ())), preferred_element_type=F32)


def _masked_softmax(s, mask):
    s = jnp.where(mask, s, NEG_INF)
    m = jnp.max(s, axis=-1, keepdims=True)
    p = jnp.where(mask, jnp.exp(s - m), 0.0)
    return p / jnp.maximum(jnp.sum(p, axis=-1, keepdims=True), 1e-30)


def _stack_heads(q_ref):
    return jnp.concatenate([q_ref[:, z * HEAD_DIM:(z + 1) * HEAD_DIM] for z in range(Q_PER_KV)], axis=0)


def _cw_attn_kernel(*refs, n_chunks):
    q_ref, kc_ref, vc_ref, ovt_ref = refs[:4]
    kw_refs = refs[4:4 + N_WIN_BLK]
    vw_refs = refs[4 + N_WIN_BLK:4 + 2 * N_WIN_BLK]
    g_ref, ocw_ref, sel_ref = refs[4 + 2 * N_WIN_BLK:]
    qb = pl.program_id(1)
    t0 = qb * Q_TILE
    rows = Q_PER_KV * Q_TILE
    q4 = _stack_heads(q_ref)

    s_c = _dot_nt(q4, kc_ref[...]) * ATTN_SCALE
    tq = t0 + (lax.broadcasted_iota(jnp.int32, (rows, n_chunks), 0) & (Q_TILE - 1))
    n_id = lax.broadcasted_iota(jnp.int32, (rows, n_chunks), 1)
    mask_c = (n_id * CMP_STRIDE + (CMP_LEN - 1) <= tq) & (n_id < n_chunks - 1)
    p_c = _masked_softmax(s_c, mask_c)
    o_c = jnp.dot(p_c.astype(BF16), vc_ref[...], preferred_element_type=F32)

    p_sum = p_c[0:Q_TILE]
    for z in range(1, Q_PER_KV):
        p_sum = p_sum + p_c[z * Q_TILE:(z + 1) * Q_TILE]
    p_hi = p_sum.astype(BF16)
    p_lo = (p_sum - p_hi.astype(F32)).astype(BF16)
    ovt = ovt_ref[...]
    imp_t = _dot_nt(ovt, p_hi) + _dot_nt(ovt, p_lo)
    j_id = lax.broadcasted_iota(jnp.int32, (LANES, Q_TILE), 0)
    cur = (t0 + lax.broadcasted_iota(jnp.int32, (LANES, Q_TILE), 1)) // SEL_BLOCK
    valid = j_id <= cur
    forced = (j_id == 0) | (valid & (j_id > cur - N_LOCAL_SEL))
    score = jnp.where(forced, FORCE_SCORE, jnp.where(valid, imp_t, -1.0))
    n_grp = LANES // SUBLANES
    grp = [score[gi * SUBLANES:(gi + 1) * SUBLANES, :] for gi in range(n_grp)]
    cnt = [jnp.zeros((SUBLANES, Q_TILE), F32) for _ in range(n_grp)]
    sub = lax.broadcasted_iota(jnp.int32, (SUBLANES, Q_TILE), 0)
    for jp in range(LANES):
        gj = jp // SUBLANES
        row = jnp.broadcast_to(grp[gj][jp % SUBLANES:jp % SUBLANES + 1, :], (SUBLANES, Q_TILE))
        for gi in range(n_grp):
            if gi < gj:
                inc = jnp.where(row > grp[gi], 1.0, 0.0)
            elif gi > gj:
                inc = jnp.where(row >= grp[gi], 1.0, 0.0)
            else:
                inc = jnp.where(sub > jp % SUBLANES, jnp.where(row >= grp[gi], 1.0, 0.0),
                                jnp.where(row > grp[gi], 1.0, 0.0))
            cnt[gi] = cnt[gi] + inc
    sel_t = jnp.where(jnp.concatenate(cnt, axis=0) < float(SEL_TOP), 1.0, 0.0)
    sel_ref[...] = sel_t.T.astype(sel_ref.dtype)

    s_w = jnp.concatenate([_dot_nt(q4, kw_refs[i][...]) for i in range(N_WIN_BLK)], axis=1) * ATTN_SCALE
    n_keys = N_WIN_BLK * Q_TILE
    tq_w = t0 + (lax.broadcasted_iota(jnp.int32, (rows, n_keys), 0) & (Q_TILE - 1))
    pos = t0 - WINDOW + lax.broadcasted_iota(jnp.int32, (rows, n_keys), 1)
    mask_w = (pos <= tq_w) & (pos > tq_w - WINDOW) & (pos >= 0)
    p_w = _masked_softmax(s_w, mask_w).astype(BF16)
    o_w = jnp.dot(p_w[:, 0:Q_TILE], vw_refs[0][...], preferred_element_type=F32)
    for i in range(1, N_WIN_BLK):
        o_w = o_w + jnp.dot(p_w[:, i * Q_TILE:(i + 1) * Q_TILE], vw_refs[i][...],
                            preferred_element_type=F32)

    gates = g_ref[...]
    for z in range(Q_PER_KV):
        r = slice(z * Q_TILE, (z + 1) * Q_TILE)
        g_c = gates[:, z * N_NSA_BRANCHES:z * N_NSA_BRANCHES + 1]
        g_w = gates[:, z * N_NSA_BRANCHES + 2:z * N_NSA_BRANCHES + 3]
        ocw_ref[:, z * HEAD_DIM:(z + 1) * HEAD_DIM] = g_c * o_c[r] + g_w * o_w[r]


def _cw_attention(q, k_cmp, v_cmp, ovt, kw, vw, gates):
    s = q.shape[0]
    n_chunks = k_cmp.shape[1]
    n_qb = s // Q_TILE
    grp_w = Q_PER_KV * HEAD_DIM

    def win_spec(i):
        return pl.BlockSpec((Q_TILE, HEAD_DIM),
                            lambda g, qb: (jnp.maximum(qb - (N_WIN_BLK - 1) + i, 0), g))

    cmp_spec = pl.BlockSpec((None, n_chunks, HEAD_DIM), lambda g, qb: (g, 0, 0))
    in_specs = ([pl.BlockSpec((Q_TILE, grp_w), lambda g, qb: (qb, g)), cmp_spec, cmp_spec,
                 pl.BlockSpec((LANES, n_chunks), lambda g, qb: (0, 0))]
                + [win_spec(i) for i in range(N_WIN_BLK)] * 2
                + [pl.BlockSpec((Q_TILE, LANES), lambda g, qb: (qb, g))])
    return pl.pallas_call(
        functools.partial(_cw_attn_kernel, n_chunks=n_chunks),
        out_shape=[jax.ShapeDtypeStruct((s, ATTN_WIDTH), F32),
                   jax.ShapeDtypeStruct((N_KV_GROUPS, s, LANES), BF16)],
        grid=(N_KV_GROUPS, n_qb),
        in_specs=in_specs,
        out_specs=[pl.BlockSpec((Q_TILE, grp_w), lambda g, qb: (qb, g)),
                   pl.BlockSpec((None, Q_TILE, LANES), lambda g, qb: (g, qb, 0))],
        compiler_params=_params("parallel", "parallel"),
        name="cmp_win_attention",
    )(q, k_cmp, v_cmp, ovt, *([kw] * N_WIN_BLK), *([vw] * N_WIN_BLK), gates)


def _sel_attn_kernel(qb_ref, kt_ref, q_ref, k_ref, v_ref, sel_ref, e_ref, ocw_ref, g_ref, o_ref,
                     q4_ref, m_ref, l_ref, acc_ref):
    step = pl.program_id(1)
    qb = qb_ref[step]
    kt = kt_ref[step]

    @pl.when(kt == 0)
    def _():
        q4_ref[...] = _stack_heads(q_ref)
        m_ref[...] = jnp.full(m_ref.shape, NEG_INF, F32)
        l_ref[...] = jnp.zeros(l_ref.shape, F32)
        acc_ref[...] = jnp.zeros(acc_ref.shape, F32)

    picked = jnp.dot(sel_ref[...], e_ref[...], preferred_element_type=F32)
    tq = qb * Q_TILE + lax.broadcasted_iota(jnp.int32, (Q_TILE, K_TILE), 0)
    kpos = kt * K_TILE + lax.broadcasted_iota(jnp.int32, (Q_TILE, K_TILE), 1)
    mask = (picked > 0.5) & (kpos <= tq)
    k = k_ref[...]
    v = v_ref[...]
    for z in range(Q_PER_KV):
        r = slice(z * Q_TILE, (z + 1) * Q_TILE)
        s = jnp.where(mask, _dot_nt(q4_ref[r, :], k) * ATTN_SCALE, NEG_INF)
        m_old = m_ref[r, :]
        m_new = jnp.maximum(m_old, jnp.max(s, axis=-1, keepdims=True))
        alpha = jnp.exp(m_old - m_new)
        p = jnp.where(mask, jnp.exp(s - m_new), 0.0)
        l_ref[r, :] = alpha * l_ref[r, :] + jnp.sum(p, axis=-1, keepdims=True)
        acc_ref[r, :] = alpha * acc_ref[r, :] + jnp.dot(p.astype(BF16), v, preferred_element_type=F32)
        m_ref[r, :] = m_new

    @pl.when(kt == (qb * Q_TILE) // K_TILE)
    def _():
        gates = g_ref[...]
        for z in range(Q_PER_KV):
            r = slice(z * Q_TILE, (z + 1) * Q_TILE)
            cols = slice(z * HEAD_DIM, (z + 1) * HEAD_DIM)
            g_s = gates[:, z * N_NSA_BRANCHES + 1:z * N_NSA_BRANCHES + 2]
            o_s = acc_ref[r, :] / jnp.maximum(l_ref[r, :], 1e-30)
            o_ref[:, cols] = (ocw_ref[:, cols] + g_s * o_s).astype(o_ref.dtype)


def _sel_attention(q, ks, vs, sel, expand, ocw, gates):
    s = q.shape[0]
    n_qb = s // Q_TILE
    grp_w = Q_PER_KV * HEAD_DIM
    qb_of, kt_of = [], []
    for qb in range(n_qb):
        for kt in range((qb * Q_TILE) // K_TILE + 1):
            qb_of.append(qb)
            kt_of.append(kt)
    qb_arr = jnp.asarray(np.asarray(qb_of, np.int32))
    kt_arr = jnp.asarray(np.asarray(kt_of, np.int32))
    rows = Q_PER_KV * Q_TILE
    grid_spec = pltpu.PrefetchScalarGridSpec(
        num_scalar_prefetch=2,
        grid=(N_KV_GROUPS, len(qb_of)),
        in_specs=[pl.BlockSpec((Q_TILE, grp_w), lambda g, i, qbr, ktr: (qbr[i], g)),
                  pl.BlockSpec((K_TILE, HEAD_DIM), lambda g, i, qbr, ktr: (ktr[i], g)),
                  pl.BlockSpec((K_TILE, HEAD_DIM), lambda g, i, qbr, ktr: (ktr[i], g)),
                  pl.BlockSpec((None, Q_TILE, LANES), lambda g, i, qbr, ktr: (g, qbr[i], 0)),
                  pl.BlockSpec((None, LANES, K_TILE), lambda g, i, qbr, ktr: (ktr[i], 0, 0)),
                  pl.BlockSpec((Q_TILE, grp_w), lambda g, i, qbr, ktr: (qbr[i], g)),
                  pl.BlockSpec((Q_TILE, LANES), lambda g, i, qbr, ktr: (qbr[i], g))],
        out_specs=pl.BlockSpec((Q_TILE, grp_w), lambda g, i, qbr, ktr: (qbr[i], g)),
        scratch_shapes=[pltpu.VMEM((rows, HEAD_DIM), BF16),
                        pltpu.VMEM((rows, 1), F32),
                        pltpu.VMEM((rows, 1), F32),
                        pltpu.VMEM((rows, HEAD_DIM), F32)],
    )
    return pl.pallas_call(
        _sel_attn_kernel,
        out_shape=jax.ShapeDtypeStruct((s, ATTN_WIDTH), BF16),
        grid_spec=grid_spec,
        compiler_params=_params("parallel", "arbitrary"),
        name="sel_attention",
    )(qb_arr, kt_arr, q, ks, vs, sel, expand, ocw, gates)


def _rnn_kernel(rx_ref, ry_ref, cw_ref, cb_ref, wa_ref, ba_ref, wx_ref, bx_ref, lam_ref, o_ref,
                ext_ref, a_ref, b_ref, h_ref, *, tt, tc):
    ti = pl.program_id(1)
    n_blk = tc // RNN_BLOCK_DIM
    halo = SUBLANES

    @pl.when(ti == 0)
    def _():
        ext_ref[0:halo, :] = jnp.zeros((halo, tc), F32)
        h_ref[...] = jnp.zeros(h_ref.shape, F32)

    @pl.when(ti > 0)
    def _():
        ext_ref[0:halo, :] = ext_ref[tt:tt + halo, :]

    ext_ref[halo:halo + tt, :] = rx_ref[...]
    xr = cb_ref[...] + ext_ref[pl.ds(halo - (RNN_CONV - 1), tt), :] * cw_ref[0:1, :]
    for j in range(1, RNN_CONV):
        xr = xr + ext_ref[pl.ds(halo - (RNN_CONV - 1) + j, tt), :] * cw_ref[j:j + 1, :]

    sp = jnp.maximum(-lam_ref[...], 0.0) + jnp.log(1.0 + jnp.exp(-jnp.abs(lam_ref[...])))
    xb = xr.astype(BF16)
    for blk in range(n_blk):
        cols = slice(blk * RNN_BLOCK_DIM, (blk + 1) * RNN_BLOCK_DIM)
        xs = xb[:, cols]
        r = jax.nn.sigmoid(jnp.dot(xs, wa_ref[blk], preferred_element_type=F32) + ba_ref[:, cols])
        i = jax.nn.sigmoid(jnp.dot(xs, wx_ref[blk], preferred_element_type=F32) + bx_ref[:, cols])
        log_a = -RG_C * r * sp[:, cols]
        a_ref[:, cols] = jnp.exp(log_a)
        b_ref[:, cols] = jnp.sqrt(1.0 - jnp.exp(2.0 * log_a)) * (i * xr[:, cols])

    row = lax.broadcasted_iota(jnp.int32, (SUBLANES, tc), 0)

    def scan_rows(i, carry):
        r0 = pl.multiple_of(i * SUBLANES, SUBLANES)
        a8 = a_ref[pl.ds(r0, SUBLANES), :]
        b8 = b_ref[pl.ds(r0, SUBLANES), :]
        for d in (1, 2, 4):
            keep = row >= d
            a_sh = pltpu.roll(a8, d, axis=0)
            b_sh = pltpu.roll(b8, d, axis=0)
            b8 = jnp.where(keep, a8 * b_sh + b8, b8)
            a8 = jnp.where(keep, a8 * a_sh, a8)
        h8 = a8 * carry + b8
        b_ref[pl.ds(r0, SUBLANES), :] = h8
        return jnp.broadcast_to(h8[SUBLANES - 1:SUBLANES, :], (SUBLANES, tc))

    h_ref[...] = lax.fori_loop(0, tt // SUBLANES, scan_rows, h_ref[...])
    o_ref[...] = (b_ref[...] * jax.nn.gelu(ry_ref[...], approximate=True)).astype(o_ref.dtype)


def _rnn_branch(proj, cw, cb, wa, ba, wx, bx, lam, *, tt, tc):
    s = proj.shape[0]
    n_cb = RNN_WIDTH // tc
    blk_per = tc // RNN_BLOCK_DIM
    rx_blk = COL_RX // tc
    ry_blk = COL_RY // tc
    vec = pl.BlockSpec((1, tc), lambda c, t: (0, c))
    wspec = pl.BlockSpec((blk_per, RNN_BLOCK_DIM, RNN_BLOCK_DIM), lambda c, t: (c, 0, 0))
    return pl.pallas_call(
        functools.partial(_rnn_kernel, tt=tt, tc=tc),
        out_shape=jax.ShapeDtypeStruct((s, RNN_WIDTH), BF16),
        grid=(n_cb, s // tt),
        in_specs=[pl.BlockSpec((tt, tc), lambda c, t: (t, rx_blk + c)),
                  pl.BlockSpec((tt, tc), lambda c, t: (t, ry_blk + c)),
                  pl.BlockSpec((RNN_CONV, tc), lambda c, t: (0, c)),
                  vec, wspec, vec, wspec, vec, vec],
        out_specs=pl.BlockSpec((tt, tc), lambda c, t: (t, c)),
        scratch_shapes=[pltpu.VMEM((tt + SUBLANES, tc), F32),
                        pltpu.VMEM((tt, tc), F32),
                        pltpu.VMEM((tt, tc), F32),
                        pltpu.VMEM((SUBLANES, tc), F32)],
        compiler_params=_params("parallel", "arbitrary"),
        name="rg_lru",
    )(proj, proj, cw, cb, wa, ba, wx, bx, lam)


def _ffn_act_kernel(gate_ref, up_ref, prev_ref, cw_ref, cb_ref, o_ref, ext_ref, *, tm):
    halo = SUBLANES

    @pl.when(pl.program_id(0) == 0)
    def _():
        ext_ref[0:halo, :] = jnp.zeros((halo, ext_ref.shape[1]), F32)

    @pl.when(pl.program_id(0) > 0)
    def _():
        ext_ref[0:halo, :] = prev_ref[...]

    ext_ref[halo:halo + tm, :] = gate_ref[...]
    y = cb_ref[...] + ext_ref[pl.ds(halo - (FFN_CONV - 1), tm), :] * cw_ref[0:1, :]
    for j in range(1, FFN_CONV):
        y = y + ext_ref[pl.ds(halo - (FFN_CONV - 1) + j, tm), :] * cw_ref[j:j + 1, :]
    o_ref[...] = (y * jax.nn.sigmoid(y) * up_ref[...]).astype(o_ref.dtype)


def _ffn_act(gu, cw, cb, *, tm, tn):
    s = gu.shape[0]
    n_j = D_FF // tn
    rows_per = tm // SUBLANES
    return pl.pallas_call(
        functools.partial(_ffn_act_kernel, tm=tm),
        out_shape=jax.ShapeDtypeStruct((s, D_FF), BF16),
        grid=(s // tm, n_j),
        in_specs=[pl.BlockSpec((tm, tn), lambda i, j: (i, j)),
                  pl.BlockSpec((tm, tn), lambda i, j: (i, n_j + j)),
                  pl.BlockSpec((SUBLANES, tn), lambda i, j: (jnp.maximum(i * rows_per - 1, 0), j)),
                  pl.BlockSpec((FFN_CONV, tn), lambda i, j: (0, j)),
                  pl.BlockSpec((1, tn), lambda i, j: (0, j))],
        out_specs=pl.BlockSpec((tm, tn), lambda i, j: (i, j)),
        scratch_shapes=[pltpu.VMEM((tm + SUBLANES, tn), F32)],
        compiler_params=_params("parallel", "parallel"),
        name="ffn_act",
    )(gu, gu, gu, cw, cb)


def _overlap_t(s):
    n_chunks = s // CMP_STRIDE
    n_cmp = n_chunks - 1
    n_sel = s // SEL_BLOCK
    cmp_start = np.arange(n_cmp) * CMP_STRIDE
    sel_start = np.arange(n_sel) * SEL_BLOCK
    ov = np.clip(np.minimum(cmp_start[:, None] + CMP_LEN, sel_start[None, :] + SEL_BLOCK)
                 - np.maximum(cmp_start[:, None], sel_start[None, :]), 0, None) / CMP_LEN
    out = np.zeros((LANES, n_chunks), np.float32)
    out[:n_sel, :n_cmp] = ov.T
    return out


def _expand_table(s):
    n_kt = s // K_TILE
    key = np.arange(n_kt)[:, None, None] * K_TILE + np.arange(K_TILE)[None, None, :]
    return (key // SEL_BLOCK == np.arange(LANES)[None, :, None]).astype(np.float32)


def _gate_weight(w_in_l):
    cols = w_in_l[:, GATE_SRC:GATE_SRC + N_Q_HEADS * N_NSA_BRANCHES]
    per = Q_PER_KV * N_NSA_BRANCHES
    cols = cols.reshape(D_MODEL, N_KV_GROUPS, per)
    cols = jnp.pad(cols, ((0, 0), (0, 0), (0, LANES - per)))
    return cols.reshape(D_MODEL, N_KV_GROUPS * LANES).astype(BF16)


def kernel(x, c, positions, w_cond, b_cond, w_mod, b_mod, norm_mix, norm_ffn, w_in, q_norm, k_norm, cmp_pe_k, cmp_w_k, cmp_pe_v, cmp_w_v, rnn_conv_w, rnn_conv_b, rg_w_a, rg_b_a, rg_w_x, rg_b_x, rg_lambda, w_attn_up, w_rnn_up, w_out, w_ffn_in, ffn_conv_w, ffn_conv_b, w_ffn_down):
    b, s, d = x.shape
    depth = w_in.shape[0]
    assert b == 1 and d == D_MODEL and s % 2048 == 0 and s // SEL_BLOCK <= LANES

    c_emb = _vecmat(c.reshape(1, d, 1), w_cond[None], b_cond.reshape(1, 1, -1), silu=True, tn=w_cond.shape[1])
    c_col = jnp.broadcast_to(c_emb.reshape(1, -1, 1), (depth, c_emb.shape[-1], 1))
    mod = _vecmat(c_col, w_mod, b_mod[:, None, :], silu=False, tn=2048)

    inv_freq = ROPE_THETA ** (-jnp.arange(0, ROT_DIM, 2, dtype=jnp.float32) / ROT_DIM)
    freq_row = jnp.concatenate([inv_freq, inv_freq, jnp.zeros((LANES - ROT_DIM,), F32)])[None, :]
    rope_c, rope_sa, rope_sb = _rope_tables(positions.reshape(s, 1), freq_row, tr=512)

    ovt = jnp.asarray(_overlap_t(s), BF16)
    expand = jnp.asarray(_expand_table(s), BF16)

    h = x.reshape(s, d)
    for l in range(depth):
        sh1, sc1, g1, sh2, sc2, g2 = [mod[l, :, i * d:(i + 1) * d] for i in range(N_MOD)]
        w_main = jnp.concatenate([w_in[l][:, :GATE_SRC],
                                  w_in[l][:, GATE_SRC + N_Q_HEADS * N_NSA_BRANCHES:]], axis=1).astype(BF16)
        proj, graw = _norm_matmul(h, norm_mix[l][None], sc1, sh1, w_main, _gate_weight(w_in[l]),
                                  tm=512, tn=512, out_dtype=F32)
        q, kc, ks, vs, kw, vw, gates = _prep(proj, graw, rope_c, rope_sa, rope_sb,
                                             q_norm[l][None], k_norm[l], tr=256)
        k_cmp = _compress(kc, 0, cmp_pe_k[l], cmp_w_k[l].astype(BF16), k_norm[l][0:1], do_norm=True)
        v_cmp = _compress(proj, COL_VC // HEAD_DIM, cmp_pe_v[l], cmp_w_v[l].astype(BF16),
                          k_norm[l][0:1], do_norm=False)
        ocw, sel = _cw_attention(q, k_cmp, v_cmp, ovt, kw, vw, gates)
        attn = _sel_attention(q, ks, vs, sel, expand, ocw, gates)
        rnn = _rnn_branch(proj, rnn_conv_w[l], rnn_conv_b[l][None], rg_w_a[l].astype(BF16), rg_b_a[l][None],
                          rg_w_x[l].astype(BF16), rg_b_x[l][None], rg_lambda[l][None], tt=512, tc=1024)
        merged = _merge(attn, rnn, w_attn_up[l].astype(BF16), w_rnn_up[l].astype(BF16), proj, tm=512, tn=1024)
        h = _matmul_residual(merged, w_out[l].astype(BF16), h, g1, tm=512, tn=1024)
        gu = _norm_matmul(h, norm_ffn[l][None], sc2, sh2, w_ffn_in[l].astype(BF16),
                          tm=512, tn=1024, out_dtype=F32)[0]
        act = _ffn_act(gu, ffn_conv_w[l], ffn_conv_b[l][None], tm=512, tn=1024)
        h = _matmul_residual(act, w_ffn_down[l].astype(BF16), h, g2, tm=512, tn=512)
    return h.reshape(b, s, d)
```

```python
import functools

import numpy as np
import jax
import jax.numpy as jnp
from jax import lax
from jax.experimental import pallas as pl
from jax.experimental.pallas import tpu as pltpu

F32 = jnp.float32
BF16 = jnp.bfloat16

D_MODEL = 4096
N_Q_HEADS = 16
N_KV_GROUPS = 4
HEAD_DIM = 128
Q_PER_KV = N_Q_HEADS // N_KV_GROUPS
ATTN_WIDTH = N_Q_HEADS * HEAD_DIM
KV_WIDTH = N_KV_GROUPS * HEAD_DIM
N_NSA_BRANCHES = 3
ROT_DIM = HEAD_DIM // 4
ROPE_THETA = 500000.0
CMP_LEN = 32
CMP_STRIDE = 16
SEL_BLOCK = 64
SEL_TOP = 16
N_LOCAL_SEL = 2
WINDOW = 512
FORCE_SCORE = 2.0 * Q_PER_KV + 1.0
RNN_WIDTH = 2048
RNN_BLOCKS = 16
RNN_BLOCK_DIM = RNN_WIDTH // RNN_BLOCKS
RNN_CONV = 4
RG_C = 8.0
D_FF = 2 * D_MODEL
FFN_CONV = 3
N_MOD = 6
EPS = 1e-6
NEG_INF = -1e30
ATTN_SCALE = HEAD_DIM ** -0.5
LOG2_E = 1.4426950408889634
Q_SCALE = ATTN_SCALE * LOG2_E

LANES = 128
SUBLANES = 8
VMEM_LIMIT_BYTES = 56 * 1024 * 1024

COL_Q = 0
COL_KC = ATTN_WIDTH
COL_VC = COL_KC + KV_WIDTH
COL_KS = COL_VC + KV_WIDTH
COL_VS = COL_KS + KV_WIDTH
COL_KW = COL_VS + KV_WIDTH
COL_VW = COL_KW + KV_WIDTH
COL_RX = COL_VW + KV_WIDTH
COL_RY = COL_RX + RNN_WIDTH
COL_GA = COL_RY + RNN_WIDTH
COL_GR = COL_GA + D_MODEL
N_PROJ = COL_GR + D_MODEL
GATE_SRC = ATTN_WIDTH + 6 * KV_WIDTH

Q_TILE = 128
K_TILE = 1024
SEL_ROWS = Q_PER_KV * Q_TILE
N_WIN_BLK = WINDOW // Q_TILE + 1
NORM_CHUNK = 64
GATE_LANES = LANES // N_KV_GROUPS


def _params(*sem):
    return pltpu.CompilerParams(dimension_semantics=sem, vmem_limit_bytes=VMEM_LIMIT_BYTES)


def _vecmat_kernel(x_ref, w_ref, b_ref, o_ref, *, silu):
    y = jnp.sum(w_ref[...] * x_ref[...], axis=0, keepdims=True) + b_ref[...]
    if silu:
        y = y * jax.nn.sigmoid(y)
    o_ref[...] = y


def _vecmat(x_col, w, b, *, silu, tn):
    n_l, k, n = w.shape
    return pl.pallas_call(
        functools.partial(_vecmat_kernel, silu=silu),
        out_shape=jax.ShapeDtypeStruct((n_l, 1, n), F32),
        grid=(n_l, n // tn),
        in_specs=[pl.BlockSpec((None, k, 1), lambda l, j: (l, 0, 0)),
                  pl.BlockSpec((None, k, tn), lambda l, j: (l, 0, j)),
                  pl.BlockSpec((None, 1, tn), lambda l, j: (l, 0, j))],
        out_specs=pl.BlockSpec((None, 1, tn), lambda l, j: (l, 0, j)),
        compiler_params=_params("parallel", "parallel"),
        name="vecmat",
    )(x_col, w, b)


def _rope_table_kernel(pos_ref, freq_ref, c_ref, sa_ref, sb_ref):
    ang = pos_ref[...].astype(F32) * freq_ref[...]
    lane = lax.broadcasted_iota(jnp.int32, ang.shape, 1)
    cos = jnp.cos(ang)
    sin = jnp.sin(ang)
    c_ref[...] = jnp.where(lane < ROT_DIM, cos, 1.0)
    sa_ref[...] = jnp.where(lane < ROT_DIM // 2, -sin, 0.0)
    sb_ref[...] = jnp.where((lane >= ROT_DIM // 2) & (lane < ROT_DIM), sin, 0.0)


def _rope_tables(pos_col, freq_row, tr):
    s = pos_col.shape[0]
    spec = pl.BlockSpec((tr, LANES), lambda i: (i, 0))
    return pl.pallas_call(
        _rope_table_kernel,
        out_shape=[jax.ShapeDtypeStruct((s, LANES), F32)] * 3,
        grid=(s // tr,),
        in_specs=[pl.BlockSpec((tr, 1), lambda i: (i, 0)),
                  pl.BlockSpec((1, LANES), lambda i: (0, 0))],
        out_specs=[spec, spec, spec],
        compiler_params=_params("parallel"),
        name="rope_tables",
    )(pos_col, freq_row)


def _ada_norm(h_ref, gain_ref, sc_ref, sh_ref, u_ref):
    for r0 in range(0, h_ref.shape[0], NORM_CHUNK):
        x = h_ref[r0:r0 + NORM_CHUNK, :]
        y = x * lax.rsqrt(jnp.mean(x * x, axis=-1, keepdims=True) + EPS)
        u = (y * gain_ref[...]) * (1.0 + sc_ref[...]) + sh_ref[...]
        u_ref[r0:r0 + NORM_CHUNK, :] = u.astype(BF16)


def _norm_mm_kernel(h_ref, gain_ref, sc_ref, sh_ref, w_ref, wg_ref, o_ref, og_ref, u_ref):
    @pl.when(pl.program_id(1) == 0)
    def _():
        _ada_norm(h_ref, gain_ref, sc_ref, sh_ref, u_ref)
        og_ref[...] = jnp.dot(u_ref[...], wg_ref[...], preferred_element_type=F32)

    o_ref[...] = jnp.dot(u_ref[...], w_ref[...], preferred_element_type=F32).astype(o_ref.dtype)


def _norm_matmul(h, gain, sc, sh, w, wg, *, layer, tm, tn, out_dtype):
    s, d = h.shape
    n = w.shape[2]
    ng = wg.shape[2]
    row = pl.BlockSpec((1, d), lambda i, j: (0, 0))
    return pl.pallas_call(
        _norm_mm_kernel,
        out_shape=[jax.ShapeDtypeStruct((s, n), out_dtype), jax.ShapeDtypeStruct((s, ng), F32)],
        grid=(s // tm, n // tn),
        in_specs=[pl.BlockSpec((tm, d), lambda i, j: (i, 0)), row, row, row,
                  pl.BlockSpec((None, d, tn), lambda i, j: (layer, 0, j)),
                  pl.BlockSpec((None, d, ng), lambda i, j: (layer, 0, 0))],
        out_specs=[pl.BlockSpec((tm, tn), lambda i, j: (i, j)),
                   pl.BlockSpec((tm, ng), lambda i, j: (i, 0))],
        scratch_shapes=[pltpu.VMEM((tm, d), BF16)],
        compiler_params=_params("parallel", "arbitrary"),
        name="norm_matmul",
    )(h, gain, sc, sh, w, wg)


def _ffn_in_kernel(h_ref, gain_ref, sc_ref, sh_ref, wg_ref, wu_ref, cw_ref, cb_ref, o_ref,
                   u_ref, ext_ref, carry_ref, *, tm):
    i = pl.program_id(0)
    j = pl.program_id(1)
    halo = SUBLANES

    @pl.when(j == 0)
    def _():
        _ada_norm(h_ref, gain_ref, sc_ref, sh_ref, u_ref)

    @pl.when(i == 0)
    def _():
        ext_ref[0:halo, :] = jnp.zeros((halo, ext_ref.shape[1]), F32)

    @pl.when(i > 0)
    def _():
        ext_ref[0:halo, :] = carry_ref[j]

    ext_ref[halo:halo + tm, :] = jnp.dot(u_ref[...], wg_ref[...], preferred_element_type=F32)
    carry_ref[j] = ext_ref[tm:tm + halo, :]
    y = cb_ref[...] + ext_ref[pl.ds(halo - (FFN_CONV - 1), tm), :] * cw_ref[0:1, :]
    for t in range(1, FFN_CONV):
        y = y + ext_ref[pl.ds(halo - (FFN_CONV - 1) + t, tm), :] * cw_ref[t:t + 1, :]
    up = jnp.dot(u_ref[...], wu_ref[...], preferred_element_type=F32)
    o_ref[...] = (y * jax.nn.sigmoid(y) * up).astype(o_ref.dtype)


def _ffn_in(h, gain, sc, sh, w, cw, cb, *, layer, tm, tn):
    s, d = h.shape
    n_j = D_FF // tn
    row = pl.BlockSpec((1, d), lambda i, j: (0, 0))
    return pl.pallas_call(
        functools.partial(_ffn_in_kernel, tm=tm),
        out_shape=jax.ShapeDtypeStruct((s, D_FF), BF16),
        grid=(s // tm, n_j),
        in_specs=[pl.BlockSpec((tm, d), lambda i, j: (i, 0)), row, row, row,
                  pl.BlockSpec((None, d, tn), lambda i, j: (layer, 0, j)),
                  pl.BlockSpec((None, d, tn), lambda i, j: (layer, 0, n_j + j)),
                  pl.BlockSpec((FFN_CONV, tn), lambda i, j: (0, j)),
                  pl.BlockSpec((1, tn), lambda i, j: (0, j))],
        out_specs=pl.BlockSpec((tm, tn), lambda i, j: (i, j)),
        scratch_shapes=[pltpu.VMEM((tm, d), BF16),
                        pltpu.VMEM((tm + SUBLANES, tn), F32),
                        pltpu.VMEM((n_j, SUBLANES, tn), F32)],
        compiler_params=_params("arbitrary", "arbitrary"),
        name="ffn_in",
    )(h, gain, sc, sh, w, w, cw, cb)


def _mm_res_kernel(a_ref, w_ref, h_ref, g_ref, o_ref):
    y = jnp.dot(a_ref[...], w_ref[...], preferred_element_type=F32)
    o_ref[...] = h_ref[...] + g_ref[...] * y


def _matmul_residual(a, w, h, g, *, layer, tm, tn):
    s, k = a.shape
    n = w.shape[2]
    return pl.pallas_call(
        _mm_res_kernel,
        out_shape=jax.ShapeDtypeStruct((s, n), F32),
        grid=(s // tm, n // tn),
        in_specs=[pl.BlockSpec((tm, k), lambda i, j: (i, 0)),
                  pl.BlockSpec((None, k, tn), lambda i, j: (layer, 0, j)),
                  pl.BlockSpec((tm, tn), lambda i, j: (i, j)),
                  pl.BlockSpec((1, tn), lambda i, j: (0, j))],
        out_specs=pl.BlockSpec((tm, tn), lambda i, j: (i, j)),
        compiler_params=_params("parallel", "parallel"),
        name="matmul_residual",
    )(a, w, h, g)


def _merge_kernel(attn_ref, rnn_ref, wa_ref, wr_ref, ga_ref, gr_ref, o_ref):
    ya = jnp.dot(attn_ref[...], wa_ref[...], preferred_element_type=F32)
    yr = jnp.dot(rnn_ref[...], wr_ref[...], preferred_element_type=F32)
    o_ref[...] = (jax.nn.sigmoid(ga_ref[...]) * ya + jax.nn.sigmoid(gr_ref[...]) * yr).astype(o_ref.dtype)


def _merge(attn, rnn, wa, wr, proj, *, layer, tm, tn):
    s, k = attn.shape
    n = wa.shape[2]
    ga_blk = COL_GA // tn
    gr_blk = COL_GR // tn
    return pl.pallas_call(
        _merge_kernel,
        out_shape=jax.ShapeDtypeStruct((s, n), BF16),
        grid=(s // tm, n // tn),
        in_specs=[pl.BlockSpec((tm, k), lambda i, j: (i, 0)),
                  pl.BlockSpec((tm, k), lambda i, j: (i, 0)),
                  pl.BlockSpec((None, k, tn), lambda i, j: (layer, 0, j)),
                  pl.BlockSpec((None, k, tn), lambda i, j: (layer, 0, j)),
                  pl.BlockSpec((tm, tn), lambda i, j: (i, ga_blk + j)),
                  pl.BlockSpec((tm, tn), lambda i, j: (i, gr_blk + j))],
        out_specs=pl.BlockSpec((tm, tn), lambda i, j: (i, j)),
        compiler_params=_params("parallel", "parallel"),
        name="merge",
    )(attn, rnn, wa, wr, proj, proj)


def _head_norm(x, gain):
    return x * lax.rsqrt(jnp.mean(x * x, axis=-1, keepdims=True) + EPS) * gain


def _rope(x, c, sa, sb):
    return (x * c + pltpu.roll(x, LANES - ROT_DIM // 2, axis=1) * sa
            + pltpu.roll(x, ROT_DIM // 2, axis=1) * sb)


def _prep_kernel(p_ref, graw_ref, c_ref, sa_ref, sb_ref, qn_ref, kn_ref, blk_ref,
                 q_ref, kc_ref, ks_ref, vs_ref, kw_ref, vw_ref, g_ref):
    c, sa, sb = c_ref[...], sa_ref[...], sb_ref[...]
    qn = qn_ref[...]
    for hd in range(N_Q_HEADS):
        cols = slice(hd * HEAD_DIM, (hd + 1) * HEAD_DIM)
        q_ref[:, cols] = (_rope(_head_norm(p_ref[:, cols], qn), c, sa, sb) * Q_SCALE).astype(BF16)
    for g in range(N_KV_GROUPS):
        cols = slice(g * HEAD_DIM, (g + 1) * HEAD_DIM)

        def src(base):
            return p_ref[:, base + g * HEAD_DIM:base + (g + 1) * HEAD_DIM]

        kc_ref[:, cols] = _rope(src(COL_KC), c, sa, sb)
        ks_ref[:, 2 * g * HEAD_DIM:(2 * g + 1) * HEAD_DIM] = _rope(
            _head_norm(src(COL_KS), kn_ref[1:2, :]), c, sa, sb).astype(BF16)
        ks_ref[:, (2 * g + 1) * HEAD_DIM:(2 * g + 2) * HEAD_DIM] = blk_ref[...]
        kw_ref[:, cols] = _rope(_head_norm(src(COL_KW), kn_ref[2:3, :]), c, sa, sb).astype(BF16)
        vs_ref[:, cols] = src(COL_VS).astype(BF16)
        vw_ref[:, cols] = src(COL_VW).astype(BF16)
    sig = jax.nn.sigmoid(graw_ref[...])
    for g in range(N_KV_GROUPS):
        g_ref[:, g * LANES:(g + 1) * LANES] = sig if g == 0 else pltpu.roll(sig, LANES - g * GATE_LANES, axis=1)


def _prep(proj, graw, c, sa, sb, qn, kn, blk_onehot, *, tr):
    s = proj.shape[0]
    ng = N_KV_GROUPS * LANES
    tab = pl.BlockSpec((tr, LANES), lambda i: (i, 0))
    kv_spec = pl.BlockSpec((tr, KV_WIDTH), lambda i: (i, 0))
    return pl.pallas_call(
        _prep_kernel,
        out_shape=[jax.ShapeDtypeStruct((s, ATTN_WIDTH), BF16),
                   jax.ShapeDtypeStruct((s, KV_WIDTH), F32),
                   jax.ShapeDtypeStruct((s, 2 * KV_WIDTH), BF16),
                   jax.ShapeDtypeStruct((s, KV_WIDTH), BF16),
                   jax.ShapeDtypeStruct((s, KV_WIDTH), BF16),
                   jax.ShapeDtypeStruct((s, KV_WIDTH), BF16),
                   jax.ShapeDtypeStruct((s, ng), F32)],
        grid=(s // tr,),
        in_specs=[pl.BlockSpec((tr, COL_RX), lambda i: (i, 0)),
                  tab, tab, tab, tab,
                  pl.BlockSpec((1, HEAD_DIM), lambda i: (0, 0)),
                  pl.BlockSpec((N_NSA_BRANCHES, HEAD_DIM), lambda i: (0, 0)),
                  tab],
        out_specs=[pl.BlockSpec((tr, ATTN_WIDTH), lambda i: (i, 0)),
                   kv_spec, pl.BlockSpec((tr, 2 * KV_WIDTH), lambda i: (i, 0)),
                   kv_spec, kv_spec, kv_spec,
                   pl.BlockSpec((tr, ng), lambda i: (i, 0))],
        compiler_params=_params("parallel"),
        name="qk_prep",
    )(proj, graw, c, sa, sb, qn, kn, blk_onehot)


def _compress_kernel(x_ref, pe_ref, w_ref, gain_ref, o_ref, shift_ref, *, do_norm, n_chunks):
    acc_a = jnp.zeros((n_chunks, HEAD_DIM), F32)
    acc_b = jnp.zeros((n_chunks, HEAD_DIM), F32)
    for l in range(CMP_STRIDE):
        xl = x_ref[pl.ds(l, n_chunks, stride=CMP_STRIDE), :]
        xa = (xl + pe_ref[l:l + 1, :]).astype(BF16)
        xb = (xl + pe_ref[CMP_STRIDE + l:CMP_STRIDE + l + 1, :]).astype(BF16)
        acc_a = acc_a + jnp.dot(xa, w_ref[l], preferred_element_type=F32)
        acc_b = acc_b + jnp.dot(xb, w_ref[CMP_STRIDE + l], preferred_element_type=F32)
    shift_ref[0:n_chunks, :] = acc_b
    shift_ref[n_chunks:n_chunks + SUBLANES, :] = jnp.zeros((SUBLANES, HEAD_DIM), F32)
    out = acc_a + shift_ref[1:n_chunks + 1, :]
    if do_norm:
        out = _head_norm(out, gain_ref[...])
    o_ref[...] = out.astype(o_ref.dtype)


def _compress(x, col_blk0, pe, w, gain, *, do_norm):
    s = x.shape[0]
    n_chunks = s // CMP_STRIDE
    return pl.pallas_call(
        functools.partial(_compress_kernel, do_norm=do_norm, n_chunks=n_chunks),
        out_shape=jax.ShapeDtypeStruct((N_KV_GROUPS, n_chunks, HEAD_DIM), BF16),
        grid=(N_KV_GROUPS,),
        in_specs=[pl.BlockSpec((s, HEAD_DIM), lambda g: (0, col_blk0 + g)),
                  pl.BlockSpec((CMP_LEN, HEAD_DIM), lambda g: (0, 0)),
                  pl.BlockSpec((CMP_LEN, HEAD_DIM, HEAD_DIM), lambda g: (0, 0, 0)),
                  pl.BlockSpec((1, HEAD_DIM), lambda g: (0, 0))],
        out_specs=pl.BlockSpec((None, n_chunks, HEAD_DIM), lambda g: (g, 0, 0)),
        scratch_shapes=[pltpu.VMEM((n_chunks + SUBLANES, HEAD_DIM), F32)],
        compiler_params=_params("parallel"),
        name="compress",
    )(x, pe, w, gain)


def _dot_nt(a, b):
    return lax.dot_general(a, b, (((1,), (1,)), ((), ())), preferred_element_type=F32)


def _masked_softmax2(s, mask):
    s = jnp.where(mask, s, NEG_INF)
    m = jnp.max(s, axis=-1, keepdims=True)
    p = jnp.where(mask, jnp.exp2(s - m), 0.0)
    return p / jnp.maximum(jnp.sum(p, axis=-1, keepdims=True), 1e-30)


def _stack_heads(q_ref):
    return jnp.concatenate([q_ref[:, z * HEAD_DIM:(z + 1) * HEAD_DIM] for z in range(Q_PER_KV)], axis=0)


def _cw_attn_kernel(*refs, n_chunks):
    q_ref, kc_ref, vc_ref, ovt_ref = refs[:4]
    kw_refs = refs[4:4 + N_WIN_BLK]
    vw_refs = refs[4 + N_WIN_BLK:4 + 2 * N_WIN_BLK]
    g_ref, ocw_ref, sel_ref = refs[4 + 2 * N_WIN_BLK:]
    qb = pl.program_id(1)
    t0 = qb * Q_TILE
    rows = Q_PER_KV * Q_TILE
    q4 = _stack_heads(q_ref)

    s_c = _dot_nt(q4, kc_ref[...])
    tq = t0 + (lax.broadcasted_iota(jnp.int32, (rows, n_chunks), 0) & (Q_TILE - 1))
    n_id = lax.broadcasted_iota(jnp.int32, (rows, n_chunks), 1)
    mask_c = (n_id * CMP_STRIDE + (CMP_LEN - 1) <= tq) & (n_id < n_chunks - 1)
    p_c = _masked_softmax2(s_c, mask_c)
    o_c = jnp.dot(p_c.astype(BF16), vc_ref[...], preferred_element_type=F32)

    p_sum = p_c[0:Q_TILE]
    for z in range(1, Q_PER_KV):
        p_sum = p_sum + p_c[z * Q_TILE:(z + 1) * Q_TILE]
    p_hi = p_sum.astype(BF16)
    p_lo = (p_sum - p_hi.astype(F32)).astype(BF16)
    ovt = ovt_ref[...]
    imp_t = _dot_nt(ovt, p_hi) + _dot_nt(ovt, p_lo)
    j_id = lax.broadcasted_iota(jnp.int32, (LANES, Q_TILE), 0)
    cur = (t0 + lax.broadcasted_iota(jnp.int32, (LANES, Q_TILE), 1)) // SEL_BLOCK
    valid = j_id <= cur
    forced = (j_id == 0) | (valid & (j_id > cur - N_LOCAL_SEL))
    score = jnp.where(forced, FORCE_SCORE, jnp.where(valid, imp_t, -1.0))
    n_grp = LANES // SUBLANES
    grp = [score[gi * SUBLANES:(gi + 1) * SUBLANES, :] for gi in range(n_grp)]
    cnt = [jnp.zeros((SUBLANES, Q_TILE), F32) for _ in range(n_grp)]
    sub = lax.broadcasted_iota(jnp.int32, (SUBLANES, Q_TILE), 0)
    for jp in range(LANES):
        gj = jp // SUBLANES
        row = jnp.broadcast_to(grp[gj][jp % SUBLANES:jp % SUBLANES + 1, :], (SUBLANES, Q_TILE))
        for gi in range(n_grp):
            if gi < gj:
                inc = jnp.where(row > grp[gi], 1.0, 0.0)
            elif gi > gj:
                inc = jnp.where(row >= grp[gi], 1.0, 0.0)
            else:
                inc = jnp.where(sub > jp % SUBLANES, jnp.where(row >= grp[gi], 1.0, 0.0),
                                jnp.where(row > grp[gi], 1.0, 0.0))
            cnt[gi] = cnt[gi] + inc
    bias_t = jnp.where(jnp.concatenate(cnt, axis=0) < float(SEL_TOP), 0.0, NEG_INF)
    sel_ref[...] = bias_t.T.astype(sel_ref.dtype)

    s_w = jnp.concatenate([_dot_nt(q4, kw_refs[i][...]) for i in range(N_WIN_BLK)], axis=1)
    n_keys = N_WIN_BLK * Q_TILE
    tq_w = t0 + (lax.broadcasted_iota(jnp.int32, (rows, n_keys), 0) & (Q_TILE - 1))
    pos = t0 - WINDOW + lax.broadcasted_iota(jnp.int32, (rows, n_keys), 1)
    mask_w = (pos <= tq_w) & (pos > tq_w - WINDOW) & (pos >= 0)
    p_w = _masked_softmax2(s_w, mask_w).astype(BF16)
    o_w = jnp.dot(p_w[:, 0:Q_TILE], vw_refs[0][...], preferred_element_type=F32)
    for i in range(1, N_WIN_BLK):
        o_w = o_w + jnp.dot(p_w[:, i * Q_TILE:(i + 1) * Q_TILE], vw_refs[i][...],
                            preferred_element_type=F32)

    gates = g_ref[...]
    for z in range(Q_PER_KV):
        r = slice(z * Q_TILE, (z + 1) * Q_TILE)
        g_c = gates[:, z * N_NSA_BRANCHES:z * N_NSA_BRANCHES + 1]
        g_w = gates[:, z * N_NSA_BRANCHES + 2:z * N_NSA_BRANCHES + 3]
        ocw_ref[:, z * HEAD_DIM:(z + 1) * HEAD_DIM] = g_c * o_c[r] + g_w * o_w[r]


def _cw_attention(q, k_cmp, v_cmp, ovt, kw, vw, gates):
    s = q.shape[0]
    n_chunks = k_cmp.shape[1]
    n_qb = s // Q_TILE
    grp_w = Q_PER_KV * HEAD_DIM

    def win_spec(i):
        return pl.BlockSpec((Q_TILE, HEAD_DIM),
                            lambda g, qb: (jnp.maximum(qb - (N_WIN_BLK - 1) + i, 0), g))

    cmp_spec = pl.BlockSpec((None, n_chunks, HEAD_DIM), lambda g, qb: (g, 0, 0))
    in_specs = ([pl.BlockSpec((Q_TILE, grp_w), lambda g, qb: (qb, g)), cmp_spec, cmp_spec,
                 pl.BlockSpec((LANES, n_chunks), lambda g, qb: (0, 0))]
                + [win_spec(i) for i in range(N_WIN_BLK)] * 2
                + [pl.BlockSpec((Q_TILE, LANES), lambda g, qb: (qb, g))])
    return pl.pallas_call(
        functools.partial(_cw_attn_kernel, n_chunks=n_chunks),
        out_shape=[jax.ShapeDtypeStruct((s, ATTN_WIDTH), F32),
                   jax.ShapeDtypeStruct((N_KV_GROUPS, s, LANES), BF16)],
        grid=(N_KV_GROUPS, n_qb),
        in_specs=in_specs,
        out_specs=[pl.BlockSpec((Q_TILE, grp_w), lambda g, qb: (qb, g)),
                   pl.BlockSpec((None, Q_TILE, LANES), lambda g, qb: (g, qb, 0))],
        compiler_params=_params("parallel", "parallel"),
        name="cmp_win_attention",
    )(q, k_cmp, v_cmp, ovt, *([kw] * N_WIN_BLK), *([vw] * N_WIN_BLK), gates)


def _sel_attn_kernel(qb_ref, kt_ref, q_ref, sb_ref, k_ref, v_ref, ocw_ref, g_ref, o_ref,
                     qx_ref, m_ref, l_ref, acc_ref):
    step = pl.program_id(1)
    qb = qb_ref[step]
    kt = kt_ref[step]
    last_kt = (qb * Q_TILE) // K_TILE
    n_lane_blk = K_TILE // LANES
    half = SEL_ROWS // 2
    heads_per_half = half // Q_TILE

    @pl.when(kt == 0)
    def _():
        for z in range(Q_PER_KV):
            r = slice(z * Q_TILE, (z + 1) * Q_TILE)
            qx_ref[r, 0:HEAD_DIM] = q_ref[:, z * HEAD_DIM:(z + 1) * HEAD_DIM]
            qx_ref[r, HEAD_DIM:2 * HEAD_DIM] = sb_ref[...]
        m_ref[...] = jnp.full(m_ref.shape, NEG_INF, F32)
        l_ref[...] = jnp.zeros(l_ref.shape, F32)
        acc_ref[...] = jnp.zeros(acc_ref.shape, F32)

    def update(causal):
        k = k_ref[...]
        v = v_ref[...]
        if causal:
            tq = qb * Q_TILE + lax.broadcasted_iota(jnp.int32, (Q_TILE, K_TILE), 0)
            kpos = kt * K_TILE + lax.broadcasted_iota(jnp.int32, (Q_TILE, K_TILE), 1)
            cbias = jnp.where(kpos <= tq, 0.0, NEG_INF)
        for hf in range(2):
            r = slice(hf * half, (hf + 1) * half)
            s = _dot_nt(qx_ref[r, :], k)
            if causal:
                s = (s.reshape(heads_per_half, Q_TILE, K_TILE) + cbias[None]).reshape(half, K_TILE)
            blk = [s[:, c * LANES:(c + 1) * LANES] for c in range(n_lane_blk)]
            mx = blk[0]
            for c in range(1, n_lane_blk):
                mx = jnp.maximum(mx, blk[c])
            m_old = m_ref[r, :]
            m_new = jnp.maximum(m_old, jnp.max(mx, axis=-1, keepdims=True))
            alpha = jnp.exp2(m_old - m_new)
            ps = [jnp.exp2(b - m_new) for b in blk]
            l_add = ps[0]
            for c in range(1, n_lane_blk):
                l_add = l_add + ps[c]
            l_ref[r, :] = alpha * l_ref[r, :] + l_add
            p = jnp.concatenate([x.astype(BF16) for x in ps], axis=1)
            acc_ref[r, :] = alpha * acc_ref[r, :] + jnp.dot(p, v, preferred_element_type=F32)
            m_ref[r, :] = m_new

    @pl.when(kt < last_kt)
    def _():
        update(False)

    @pl.when(kt == last_kt)
    def _():
        update(True)
        gates = g_ref[...]
        for z in range(Q_PER_KV):
            r = slice(z * Q_TILE, (z + 1) * Q_TILE)
            cols = slice(z * HEAD_DIM, (z + 1) * HEAD_DIM)
            g_s = gates[:, z * N_NSA_BRANCHES + 1:z * N_NSA_BRANCHES + 2]
            l_row = jnp.sum(l_ref[r, :], axis=-1, keepdims=True)
            o_s = acc_ref[r, :] / jnp.maximum(l_row, 1e-30)
            o_ref[:, cols] = (ocw_ref[:, cols] + g_s * o_s).astype(o_ref.dtype)


def _sel_attention(q, sel_bias, ks_ext, vs, ocw, gates):
    s = q.shape[0]
    n_qb = s // Q_TILE
    grp_w = Q_PER_KV * HEAD_DIM
    qb_of, kt_of = [], []
    for qb in range(n_qb):
        for kt in range((qb * Q_TILE) // K_TILE + 1):
            qb_of.append(qb)
            kt_of.append(kt)
    qb_arr = jnp.asarray(np.asarray(qb_of, np.int32))
    kt_arr = jnp.asarray(np.asarray(kt_of, np.int32))
    grid_spec = pltpu.PrefetchScalarGridSpec(
        num_scalar_prefetch=2,
        grid=(N_KV_GROUPS, len(qb_of)),
        in_specs=[pl.BlockSpec((Q_TILE, grp_w), lambda g, i, qbr, ktr: (qbr[i], g)),
                  pl.BlockSpec((None, Q_TILE, LANES), lambda g, i, qbr, ktr: (g, qbr[i], 0)),
                  pl.BlockSpec((K_TILE, 2 * HEAD_DIM), lambda g, i, qbr, ktr: (ktr[i], g)),
                  pl.BlockSpec((K_TILE, HEAD_DIM), lambda g, i, qbr, ktr: (ktr[i], g)),
                  pl.BlockSpec((Q_TILE, grp_w), lambda g, i, qbr, ktr: (qbr[i], g)),
                  pl.BlockSpec((Q_TILE, LANES), lambda g, i, qbr, ktr: (qbr[i], g))],
        out_specs=pl.BlockSpec((Q_TILE, grp_w), lambda g, i, qbr, ktr: (qbr[i], g)),
        scratch_shapes=[pltpu.VMEM((SEL_ROWS, 2 * HEAD_DIM), BF16),
                        pltpu.VMEM((SEL_ROWS, LANES), F32),
                        pltpu.VMEM((SEL_ROWS, LANES), F32),
                        pltpu.VMEM((SEL_ROWS, HEAD_DIM), F32)],
    )
    return pl.pallas_call(
        _sel_attn_kernel,
        out_shape=jax.ShapeDtypeStruct((s, ATTN_WIDTH), BF16),
        grid_spec=grid_spec,
        compiler_params=_params("parallel", "arbitrary"),
        name="sel_attention",
    )(qb_arr, kt_arr, q, sel_bias, ks_ext, vs, ocw, gates)


def _rnn_kernel(rx_ref, ry_ref, cw_ref, cb_ref, wa_ref, ba_ref, wx_ref, bx_ref, lam_ref, o_ref,
                ext_ref, a_ref, b_ref, h_ref, *, tt, tc):
    ti = pl.program_id(1)
    n_blk = tc // RNN_BLOCK_DIM
    halo = SUBLANES

    @pl.when(ti == 0)
    def _():
        ext_ref[0:halo, :] = jnp.zeros((halo, tc), F32)
        h_ref[...] = jnp.zeros(h_ref.shape, F32)

    @pl.when(ti > 0)
    def _():
        ext_ref[0:halo, :] = ext_ref[tt:tt + halo, :]

    ext_ref[halo:halo + tt, :] = rx_ref[...]
    xr = cb_ref[...] + ext_ref[pl.ds(halo - (RNN_CONV - 1), tt), :] * cw_ref[0:1, :]
    for j in range(1, RNN_CONV):
        xr = xr + ext_ref[pl.ds(halo - (RNN_CONV - 1) + j, tt), :] * cw_ref[j:j + 1, :]

    sp = jnp.maximum(-lam_ref[...], 0.0) + jnp.log(1.0 + jnp.exp(-jnp.abs(lam_ref[...])))
    xb = xr.astype(BF16)
    for blk in range(n_blk):
        cols = slice(blk * RNN_BLOCK_DIM, (blk + 1) * RNN_BLOCK_DIM)
        xs = xb[:, cols]
        r = jax.nn.sigmoid(jnp.dot(xs, wa_ref[blk], preferred_element_type=F32) + ba_ref[:, cols])
        i = jax.nn.sigmoid(jnp.dot(xs, wx_ref[blk], preferred_element_type=F32) + bx_ref[:, cols])
        log_a = -RG_C * r * sp[:, cols]
        a_ref[:, cols] = jnp.exp(log_a)
        b_ref[:, cols] = jnp.sqrt(1.0 - jnp.exp(2.0 * log_a)) * (i * xr[:, cols])

    row = lax.broadcasted_iota(jnp.int32, (SUBLANES, tc), 0)

    def scan_rows(i, carry):
        r0 = pl.multiple_of(i * SUBLANES, SUBLANES)
        a8 = a_ref[pl.ds(r0, SUBLANES), :]
        b8 = b_ref[pl.ds(r0, SUBLANES), :]
        for d in (1, 2, 4):
            keep = row >= d
            a_sh = pltpu.roll(a8, d, axis=0)
            b_sh = pltpu.roll(b8, d, axis=0)
            b8 = jnp.where(keep, a8 * b_sh + b8, b8)
            a8 = jnp.where(keep, a8 * a_sh, a8)
        h8 = a8 * carry + b8
        b_ref[pl.ds(r0, SUBLANES), :] = h8
        return jnp.broadcast_to(h8[SUBLANES - 1:SUBLANES, :], (SUBLANES, tc))

    h_ref[...] = lax.fori_loop(0, tt // SUBLANES, scan_rows, h_ref[...])
    o_ref[...] = (b_ref[...] * jax.nn.gelu(ry_ref[...], approximate=True)).astype(o_ref.dtype)


def _rnn_branch(proj, cw, cb, wa, ba, wx, bx, lam, *, tt, tc):
    s = proj.shape[0]
    n_cb = RNN_WIDTH // tc
    blk_per = tc // RNN_BLOCK_DIM
    rx_blk = COL_RX // tc
    ry_blk = COL_RY // tc
    vec = pl.BlockSpec((1, tc), lambda c, t: (0, c))
    wspec = pl.BlockSpec((blk_per, RNN_BLOCK_DIM, RNN_BLOCK_DIM), lambda c, t: (c, 0, 0))
    return pl.pallas_call(
        functools.partial(_rnn_kernel, tt=tt, tc=tc),
        out_shape=jax.ShapeDtypeStruct((s, RNN_WIDTH), BF16),
        grid=(n_cb, s // tt),
        in_specs=[pl.BlockSpec((tt, tc), lambda c, t: (t, rx_blk + c)),
                  pl.BlockSpec((tt, tc), lambda c, t: (t, ry_blk + c)),
                  pl.BlockSpec((RNN_CONV, tc), lambda c, t: (0, c)),
                  vec, wspec, vec, wspec, vec, vec],
        out_specs=pl.BlockSpec((tt, tc), lambda c, t: (t, c)),
        scratch_shapes=[pltpu.VMEM((tt + SUBLANES, tc), F32),
                        pltpu.VMEM((tt, tc), F32),
                        pltpu.VMEM((tt, tc), F32),
                        pltpu.VMEM((SUBLANES, tc), F32)],
        compiler_params=_params("parallel", "arbitrary"),
        name="rg_lru",
    )(proj, proj, cw, cb, wa, ba, wx, bx, lam)


def _overlap_t(s):
    n_chunks = s // CMP_STRIDE
    n_cmp = n_chunks - 1
    n_sel = s // SEL_BLOCK
    cmp_start = np.arange(n_cmp) * CMP_STRIDE
    sel_start = np.arange(n_sel) * SEL_BLOCK
    ov = np.clip(np.minimum(cmp_start[:, None] + CMP_LEN, sel_start[None, :] + SEL_BLOCK)
                 - np.maximum(cmp_start[:, None], sel_start[None, :]), 0, None) / CMP_LEN
    out = np.zeros((LANES, n_chunks), np.float32)
    out[:n_sel, :n_cmp] = ov.T
    return out


def _block_onehot(s):
    return (np.arange(s)[:, None] // SEL_BLOCK == np.arange(LANES)[None, :]).astype(np.float32)


def _gate_weight(w_in):
    depth = w_in.shape[0]
    cols = w_in[:, :, GATE_SRC:GATE_SRC + N_Q_HEADS * N_NSA_BRANCHES]
    per = Q_PER_KV * N_NSA_BRANCHES
    cols = cols.reshape(depth, D_MODEL, N_KV_GROUPS, per)
    cols = jnp.pad(cols, ((0, 0), (0, 0), (0, 0), (0, GATE_LANES - per)))
    return cols.reshape(depth, D_MODEL, LANES).astype(BF16)


def kernel(x, c, positions, w_cond, b_cond, w_mod, b_mod, norm_mix, norm_ffn, w_in, q_norm, k_norm, cmp_pe_k, cmp_w_k, cmp_pe_v, cmp_w_v, rnn_conv_w, rnn_conv_b, rg_w_a, rg_b_a, rg_w_x, rg_b_x, rg_lambda, w_attn_up, w_rnn_up, w_out, w_ffn_in, ffn_conv_w, ffn_conv_b, w_ffn_down):
    b, s, d = x.shape
    depth = w_in.shape[0]
    assert b == 1 and d == D_MODEL and s % 2048 == 0 and s // SEL_BLOCK <= LANES

    c_emb = _vecmat(c.reshape(1, d, 1), w_cond[None], b_cond.reshape(1, 1, -1), silu=True, tn=w_cond.shape[1])
    c_col = jnp.broadcast_to(c_emb.reshape(1, -1, 1), (depth, c_emb.shape[-1], 1))
    mod = _vecmat(c_col, w_mod, b_mod[:, None, :], silu=False, tn=2048)

    inv_freq = ROPE_THETA ** (-jnp.arange(0, ROT_DIM, 2, dtype=jnp.float32) / ROT_DIM)
    freq_row = jnp.concatenate([inv_freq, inv_freq, jnp.zeros((LANES - ROT_DIM,), F32)])[None, :]
    rope_c, rope_sa, rope_sb = _rope_tables(positions.reshape(s, 1), freq_row, tr=512)

    ovt = jnp.asarray(_overlap_t(s), BF16)
    blk_onehot = jnp.asarray(_block_onehot(s), BF16)

    w_main = jnp.concatenate([w_in[:, :, :GATE_SRC],
                              w_in[:, :, GATE_SRC + N_Q_HEADS * N_NSA_BRANCHES:]], axis=2).astype(BF16)
    w_gate = _gate_weight(w_in)
    w_attn_up_b = w_attn_up.astype(BF16)
    w_rnn_up_b = w_rnn_up.astype(BF16)
    w_out_b = w_out.astype(BF16)
    w_ffn_in_b = w_ffn_in.astype(BF16)
    w_ffn_down_b = w_ffn_down.astype(BF16)

    h = x.reshape(s, d)
    for l in range(depth):
        sh1, sc1, g1, sh2, sc2, g2 = [mod[l, :, i * d:(i + 1) * d] for i in range(N_MOD)]
        proj, graw = _norm_matmul(h, norm_mix[l][None], sc1, sh1, w_main, w_gate,
                                  layer=l, tm=512, tn=1024, out_dtype=F32)
        q, kc, ks_ext, vs, kw, vw, gates = _prep(proj, graw, rope_c, rope_sa, rope_sb,
                                                 q_norm[l][None], k_norm[l], blk_onehot, tr=256)
        k_cmp = _compress(kc, 0, cmp_pe_k[l], cmp_w_k[l].astype(BF16), k_norm[l][0:1], do_norm=True)
        v_cmp = _compress(proj, COL_VC // HEAD_DIM, cmp_pe_v[l], cmp_w_v[l].astype(BF16),
                          k_norm[l][0:1], do_norm=False)
        ocw, sel_bias = _cw_attention(q, k_cmp, v_cmp, ovt, kw, vw, gates)
        attn = _sel_attention(q, sel_bias, ks_ext, vs, ocw, gates)
        rnn = _rnn_branch(proj, rnn_conv_w[l], rnn_conv_b[l][None], rg_w_a[l].astype(BF16), rg_b_a[l][None],
                          rg_w_x[l].astype(BF16), rg_b_x[l][None], rg_lambda[l][None], tt=512, tc=1024)
        merged = _merge(attn, rnn, w_attn_up_b, w_rnn_up_b, proj, layer=l, tm=512, tn=1024)
        h = _matmul_residual(merged, w_out_b, h, g1, layer=l, tm=512, tn=1024)
        act = _ffn_in(h, norm_ffn[l][None], sc2, sh2, w_ffn_in_b, ffn_conv_w[l], ffn_conv_b[l][None],
                      layer=l, tm=512, tn=512)
        h = _matmul_residual(act, w_ffn_down_b, h, g2, layer=l, tm=512, tn=512)
    return h.reshape(b, s, d)
```

```python
import functools

import numpy as np
import jax
import jax.numpy as jnp
from jax import lax
from jax.experimental import pallas as pl
from jax.experimental.pallas import tpu as pltpu

F32 = jnp.float32
BF16 = jnp.bfloat16

D_MODEL = 4096
N_Q_HEADS = 16
N_KV_GROUPS = 4
HEAD_DIM = 128
Q_PER_KV = N_Q_HEADS // N_KV_GROUPS
ATTN_WIDTH = N_Q_HEADS * HEAD_DIM
KV_WIDTH = N_KV_GROUPS * HEAD_DIM
N_NSA_BRANCHES = 3
ROT_DIM = HEAD_DIM // 4
ROPE_THETA = 500000.0
CMP_LEN = 32
CMP_STRIDE = 16
SEL_BLOCK = 64
SEL_TOP = 16
N_LOCAL_SEL = 2
WINDOW = 512
FORCE_SCORE = 2.0 * Q_PER_KV + 1.0
RNN_WIDTH = 2048
RNN_BLOCKS = 16
RNN_BLOCK_DIM = RNN_WIDTH // RNN_BLOCKS
RNN_CONV = 4
RG_C = 8.0
D_FF = 2 * D_MODEL
FFN_CONV = 3
N_MOD = 6
EPS = 1e-6
NEG_INF = -1e30
ATTN_SCALE = HEAD_DIM ** -0.5
LOG2_E = 1.4426950408889634
Q_SCALE = ATTN_SCALE * LOG2_E

LANES = 128
SUBLANES = 8
VMEM_LIMIT_BYTES = 56 * 1024 * 1024

COL_Q = 0
COL_KC = ATTN_WIDTH
COL_VC = COL_KC + KV_WIDTH
COL_KS = COL_VC + KV_WIDTH
COL_VS = COL_KS + KV_WIDTH
COL_KW = COL_VS + KV_WIDTH
COL_VW = COL_KW + KV_WIDTH
N_PROJ_ATTN = COL_VW + KV_WIDTH
GATE_SRC = N_PROJ_ATTN
REST_SRC = GATE_SRC + N_Q_HEADS * N_NSA_BRANCHES
COL_RX = 0
COL_RY = COL_RX + RNN_WIDTH
COL_GA = COL_RY + RNN_WIDTH
COL_GR = COL_GA + D_MODEL

Q_TILE = 128
K_TILE = 1024
SEL_Q_TILE = 256
SEL_ROWS = Q_PER_KV * SEL_Q_TILE
N_WIN_BLK = WINDOW // Q_TILE + 1
NORM_CHUNK = 64
GATE_LANES = LANES // N_KV_GROUPS


def _params(*sem):
    return pltpu.CompilerParams(dimension_semantics=sem, vmem_limit_bytes=VMEM_LIMIT_BYTES)


def _vecmat_kernel(x_ref, w_ref, b_ref, o_ref, *, silu):
    y = jnp.sum(w_ref[...] * x_ref[...], axis=0, keepdims=True) + b_ref[...]
    if silu:
        y = y * jax.nn.sigmoid(y)
    o_ref[...] = y


def _vecmat(x_col, w, b, *, silu, tn):
    n_l, k, n = w.shape
    return pl.pallas_call(
        functools.partial(_vecmat_kernel, silu=silu),
        out_shape=jax.ShapeDtypeStruct((n_l, 1, n), F32),
        grid=(n_l, n // tn),
        in_specs=[pl.BlockSpec((None, k, 1), lambda l, j: (l, 0, 0)),
                  pl.BlockSpec((None, k, tn), lambda l, j: (l, 0, j)),
                  pl.BlockSpec((None, 1, tn), lambda l, j: (l, 0, j))],
        out_specs=pl.BlockSpec((None, 1, tn), lambda l, j: (l, 0, j)),
        compiler_params=_params("parallel", "parallel"),
        name="vecmat",
    )(x_col, w, b)


def _rope_table_kernel(pos_ref, freq_ref, c_ref, sa_ref, sb_ref):
    ang = pos_ref[...].astype(F32) * freq_ref[...]
    lane = lax.broadcasted_iota(jnp.int32, ang.shape, 1)
    cos = jnp.cos(ang)
    sin = jnp.sin(ang)
    c_ref[...] = jnp.where(lane < ROT_DIM, cos, 1.0)
    sa_ref[...] = jnp.where(lane < ROT_DIM // 2, -sin, 0.0)
    sb_ref[...] = jnp.where((lane >= ROT_DIM // 2) & (lane < ROT_DIM), sin, 0.0)


def _rope_tables(pos_col, freq_row, tr):
    s = pos_col.shape[0]
    spec = pl.BlockSpec((tr, LANES), lambda i: (i, 0))
    return pl.pallas_call(
        _rope_table_kernel,
        out_shape=[jax.ShapeDtypeStruct((s, LANES), F32)] * 3,
        grid=(s // tr,),
        in_specs=[pl.BlockSpec((tr, 1), lambda i: (i, 0)),
                  pl.BlockSpec((1, LANES), lambda i: (0, 0))],
        out_specs=[spec, spec, spec],
        compiler_params=_params("parallel"),
        name="rope_tables",
    )(pos_col, freq_row)


def _ada_norm(h_ref, gain_ref, sc_ref, sh_ref, u_ref):
    for r0 in range(0, h_ref.shape[0], NORM_CHUNK):
        x = h_ref[r0:r0 + NORM_CHUNK, :]
        y = x * lax.rsqrt(jnp.mean(x * x, axis=-1, keepdims=True) + EPS)
        u = (y * gain_ref[...]) * (1.0 + sc_ref[...]) + sh_ref[...]
        u_ref[r0:r0 + NORM_CHUNK, :] = u.astype(BF16)


def _norm_mm_kernel(h_ref, gain_ref, sc_ref, sh_ref, w_ref, wg_ref, o_ref, og_ref, u_ref):
    @pl.when(pl.program_id(1) == 0)
    def _():
        _ada_norm(h_ref, gain_ref, sc_ref, sh_ref, u_ref)
        og_ref[...] = jnp.dot(u_ref[...], wg_ref[...], preferred_element_type=F32)

    o_ref[...] = jnp.dot(u_ref[...], w_ref[...], preferred_element_type=F32).astype(o_ref.dtype)


def _norm_matmul(h, gain, sc, sh, w, wg, *, layer, tm, tn, out_dtype):
    s, d = h.shape
    n = w.shape[2]
    ng = wg.shape[2]
    row = pl.BlockSpec((1, d), lambda i, j: (0, 0))
    return pl.pallas_call(
        _norm_mm_kernel,
        out_shape=[jax.ShapeDtypeStruct((s, n), out_dtype), jax.ShapeDtypeStruct((s, ng), F32),
                   jax.ShapeDtypeStruct((s, d), BF16)],
        grid=(s // tm, n // tn),
        in_specs=[pl.BlockSpec((tm, d), lambda i, j: (i, 0)), row, row, row,
                  pl.BlockSpec((None, d, tn), lambda i, j: (layer, 0, j)),
                  pl.BlockSpec((None, d, ng), lambda i, j: (layer, 0, 0))],
        out_specs=[pl.BlockSpec((tm, tn), lambda i, j: (i, j)),
                   pl.BlockSpec((tm, ng), lambda i, j: (i, 0)),
                   pl.BlockSpec((tm, d), lambda i, j: (i, 0))],
        compiler_params=_params("parallel", "arbitrary"),
        name="norm_matmul",
    )(h, gain, sc, sh, w, wg)


def _mm_kernel(a_ref, w_ref, o_ref):
    o_ref[...] = jnp.dot(a_ref[...], w_ref[...], preferred_element_type=F32).astype(o_ref.dtype)


def _matmul(a, w, *, layer, tm, tn, out_dtype):
    s, k = a.shape
    n = w.shape[2]
    return pl.pallas_call(
        _mm_kernel,
        out_shape=jax.ShapeDtypeStruct((s, n), out_dtype),
        grid=(s // tm, n // tn),
        in_specs=[pl.BlockSpec((tm, k), lambda i, j: (i, 0)),
                  pl.BlockSpec((None, k, tn), lambda i, j: (layer, 0, j))],
        out_specs=pl.BlockSpec((tm, tn), lambda i, j: (i, j)),
        compiler_params=_params("parallel", "parallel"),
        name="matmul",
    )(a, w)


def _ffn_in_kernel(h_ref, gain_ref, sc_ref, sh_ref, wg_ref, wu_ref, cw_ref, cb_ref, o_ref,
                   u_ref, ext_ref, carry_ref, *, tm):
    i = pl.program_id(0)
    j = pl.program_id(1)
    halo = SUBLANES

    @pl.when(j == 0)
    def _():
        _ada_norm(h_ref, gain_ref, sc_ref, sh_ref, u_ref)

    @pl.when(i == 0)
    def _():
        ext_ref[0:halo, :] = jnp.zeros((halo, ext_ref.shape[1]), F32)

    @pl.when(i > 0)
    def _():
        ext_ref[0:halo, :] = carry_ref[j]

    ext_ref[halo:halo + tm, :] = jnp.dot(u_ref[...], wg_ref[...], preferred_element_type=F32)
    carry_ref[j] = ext_ref[tm:tm + halo, :]
    y = cb_ref[...] + ext_ref[pl.ds(halo - (FFN_CONV - 1), tm), :] * cw_ref[0:1, :]
    for t in range(1, FFN_CONV):
        y = y + ext_ref[pl.ds(halo - (FFN_CONV - 1) + t, tm), :] * cw_ref[t:t + 1, :]
    up = jnp.dot(u_ref[...], wu_ref[...], preferred_element_type=F32)
    o_ref[...] = (y * jax.nn.sigmoid(y) * up).astype(o_ref.dtype)


def _ffn_in(h, gain, sc, sh, w, cw, cb, *, layer, tm, tn):
    s, d = h.shape
    n_j = D_FF // tn
    row = pl.BlockSpec((1, d), lambda i, j: (0, 0))
    return pl.pallas_call(
        functools.partial(_ffn_in_kernel, tm=tm),
        out_shape=jax.ShapeDtypeStruct((s, D_FF), BF16),
        grid=(s // tm, n_j),
        in_specs=[pl.BlockSpec((tm, d), lambda i, j: (i, 0)), row, row, row,
                  pl.BlockSpec((None, d, tn), lambda i, j: (layer, 0, j)),
                  pl.BlockSpec((None, d, tn), lambda i, j: (layer, 0, n_j + j)),
                  pl.BlockSpec((FFN_CONV, tn), lambda i, j: (0, j)),
                  pl.BlockSpec((1, tn), lambda i, j: (0, j))],
        out_specs=pl.BlockSpec((tm, tn), lambda i, j: (i, j)),
        scratch_shapes=[pltpu.VMEM((tm, d), BF16),
                        pltpu.VMEM((tm + SUBLANES, tn), F32),
                        pltpu.VMEM((n_j, SUBLANES, tn), F32)],
        compiler_params=_params("arbitrary", "arbitrary"),
        name="ffn_in",
    )(h, gain, sc, sh, w, w, cw, cb)


def _mm_res_kernel(a_ref, w_ref, h_ref, g_ref, o_ref):
    y = jnp.dot(a_ref[...], w_ref[...], preferred_element_type=F32)
    o_ref[...] = h_ref[...] + g_ref[...] * y


def _matmul_residual(a, w, h, g, *, layer, tm, tn):
    s, k = a.shape
    n = w.shape[2]
    return pl.pallas_call(
        _mm_res_kernel,
        out_shape=jax.ShapeDtypeStruct((s, n), F32),
        grid=(s // tm, n // tn),
        in_specs=[pl.BlockSpec((tm, k), lambda i, j: (i, 0)),
                  pl.BlockSpec((None, k, tn), lambda i, j: (layer, 0, j)),
                  pl.BlockSpec((tm, tn), lambda i, j: (i, j)),
                  pl.BlockSpec((1, tn), lambda i, j: (0, j))],
        out_specs=pl.BlockSpec((tm, tn), lambda i, j: (i, j)),
        compiler_params=_params("parallel", "parallel"),
        name="matmul_residual",
    )(a, w, h, g)


def _merge_kernel(attn_ref, rnn_ref, wa_ref, wr_ref, ga_ref, gr_ref, o_ref):
    ya = jnp.dot(attn_ref[...], wa_ref[...], preferred_element_type=F32)
    yr = jnp.dot(rnn_ref[...], wr_ref[...], preferred_element_type=F32)
    o_ref[...] = (jax.nn.sigmoid(ga_ref[...]) * ya + jax.nn.sigmoid(gr_ref[...]) * yr).astype(o_ref.dtype)


def _merge(attn, rnn, wa, wr, proj, *, layer, tm, tn):
    s, k = attn.shape
    n = wa.shape[2]
    ga_blk = COL_GA // tn
    gr_blk = COL_GR // tn
    return pl.pallas_call(
        _merge_kernel,
        out_shape=jax.ShapeDtypeStruct((s, n), BF16),
        grid=(s // tm, n // tn),
        in_specs=[pl.BlockSpec((tm, k), lambda i, j: (i, 0)),
                  pl.BlockSpec((tm, k), lambda i, j: (i, 0)),
                  pl.BlockSpec((None, k, tn), lambda i, j: (layer, 0, j)),
                  pl.BlockSpec((None, k, tn), lambda i, j: (layer, 0, j)),
                  pl.BlockSpec((tm, tn), lambda i, j: (i, ga_blk + j)),
                  pl.BlockSpec((tm, tn), lambda i, j: (i, gr_blk + j))],
        out_specs=pl.BlockSpec((tm, tn), lambda i, j: (i, j)),
        compiler_params=_params("parallel", "parallel"),
        name="merge",
    )(attn, rnn, wa, wr, proj, proj)


def _head_norm(x, gain):
    return x * lax.rsqrt(jnp.mean(x * x, axis=-1, keepdims=True) + EPS) * gain


def _rope(x, c, sa, sb):
    return (x * c + pltpu.roll(x, LANES - ROT_DIM // 2, axis=1) * sa
            + pltpu.roll(x, ROT_DIM // 2, axis=1) * sb)


def _prep_kernel(p_ref, graw_ref, c_ref, sa_ref, sb_ref, qn_ref, kn_ref, blk_ref,
                 q_ref, kc_ref, ks_ref, vs_ref, kw_ref, vw_ref, g_ref):
    c, sa, sb = c_ref[...], sa_ref[...], sb_ref[...]
    qn = qn_ref[...]
    for hd in range(N_Q_HEADS):
        cols = slice(hd * HEAD_DIM, (hd + 1) * HEAD_DIM)
        q_ref[:, cols] = (_rope(_head_norm(p_ref[:, cols], qn), c, sa, sb) * Q_SCALE).astype(BF16)
    for g in range(N_KV_GROUPS):
        cols = slice(g * HEAD_DIM, (g + 1) * HEAD_DIM)

        def src(base):
            return p_ref[:, base + g * HEAD_DIM:base + (g + 1) * HEAD_DIM]

        kc_ref[:, cols] = _rope(src(COL_KC), c, sa, sb)
        ks_ref[:, 2 * g * HEAD_DIM:(2 * g + 1) * HEAD_DIM] = _rope(
            _head_norm(src(COL_KS), kn_ref[1:2, :]), c, sa, sb).astype(BF16)
        ks_ref[:, (2 * g + 1) * HEAD_DIM:(2 * g + 2) * HEAD_DIM] = blk_ref[...]
        kw_ref[:, cols] = _rope(_head_norm(src(COL_KW), kn_ref[2:3, :]), c, sa, sb).astype(BF16)
        vs_ref[:, cols] = src(COL_VS).astype(BF16)
        vw_ref[:, cols] = src(COL_VW).astype(BF16)
    sig = jax.nn.sigmoid(graw_ref[...])
    for g in range(N_KV_GROUPS):
        g_ref[:, g * LANES:(g + 1) * LANES] = sig if g == 0 else pltpu.roll(sig, LANES - g * GATE_LANES, axis=1)


def _prep(proj, graw, c, sa, sb, qn, kn, blk_onehot, *, tr):
    s = proj.shape[0]
    ng = N_KV_GROUPS * LANES
    tab = pl.BlockSpec((tr, LANES), lambda i: (i, 0))
    kv_spec = pl.BlockSpec((tr, KV_WIDTH), lambda i: (i, 0))
    return pl.pallas_call(
        _prep_kernel,
        out_shape=[jax.ShapeDtypeStruct((s, ATTN_WIDTH), BF16),
                   jax.ShapeDtypeStruct((s, KV_WIDTH), F32),
                   jax.ShapeDtypeStruct((s, 2 * KV_WIDTH), BF16),
                   jax.ShapeDtypeStruct((s, KV_WIDTH), BF16),
                   jax.ShapeDtypeStruct((s, KV_WIDTH), BF16),
                   jax.ShapeDtypeStruct((s, KV_WIDTH), BF16),
                   jax.ShapeDtypeStruct((s, ng), F32)],
        grid=(s // tr,),
        in_specs=[pl.BlockSpec((tr, N_PROJ_ATTN), lambda i: (i, 0)),
                  tab, tab, tab, tab,
                  pl.BlockSpec((1, HEAD_DIM), lambda i: (0, 0)),
                  pl.BlockSpec((N_NSA_BRANCHES, HEAD_DIM), lambda i: (0, 0)),
                  tab],
        out_specs=[pl.BlockSpec((tr, ATTN_WIDTH), lambda i: (i, 0)),
                   kv_spec, pl.BlockSpec((tr, 2 * KV_WIDTH), lambda i: (i, 0)),
                   kv_spec, kv_spec, kv_spec,
                   pl.BlockSpec((tr, ng), lambda i: (i, 0))],
        compiler_params=_params("parallel"),
        name="qk_prep",
    )(proj, graw, c, sa, sb, qn, kn, blk_onehot)


def _compress_kernel(x_ref, pe_ref, w_ref, gain_ref, o_ref, shift_ref, *, do_norm, n_chunks):
    acc_a = jnp.zeros((n_chunks, HEAD_DIM), F32)
    acc_b = jnp.zeros((n_chunks, HEAD_DIM), F32)
    for l in range(CMP_STRIDE):
        xl = x_ref[pl.ds(l, n_chunks, stride=CMP_STRIDE), :]
        xa = (xl + pe_ref[l:l + 1, :]).astype(BF16)
        xb = (xl + pe_ref[CMP_STRIDE + l:CMP_STRIDE + l + 1, :]).astype(BF16)
        acc_a = acc_a + jnp.dot(xa, w_ref[l], preferred_element_type=F32)
        acc_b = acc_b + jnp.dot(xb, w_ref[CMP_STRIDE + l], preferred_element_type=F32)
    shift_ref[0:n_chunks, :] = acc_b
    shift_ref[n_chunks:n_chunks + SUBLANES, :] = jnp.zeros((SUBLANES, HEAD_DIM), F32)
    out = acc_a + shift_ref[1:n_chunks + 1, :]
    if do_norm:
        out = _head_norm(out, gain_ref[...])
    o_ref[...] = out.astype(o_ref.dtype)


def _compress(x, col_blk0, pe, w, gain, *, do_norm):
    s = x.shape[0]
    n_chunks = s // CMP_STRIDE
    return pl.pallas_call(
        functools.partial(_compress_kernel, do_norm=do_norm, n_chunks=n_chunks),
        out_shape=jax.ShapeDtypeStruct((N_KV_GROUPS, n_chunks, HEAD_DIM), BF16),
        grid=(N_KV_GROUPS,),
        in_specs=[pl.BlockSpec((s, HEAD_DIM), lambda g: (0, col_blk0 + g)),
                  pl.BlockSpec((CMP_LEN, HEAD_DIM), lambda g: (0, 0)),
                  pl.BlockSpec((CMP_LEN, HEAD_DIM, HEAD_DIM), lambda g: (0, 0, 0)),
                  pl.BlockSpec((1, HEAD_DIM), lambda g: (0, 0))],
        out_specs=pl.BlockSpec((None, n_chunks, HEAD_DIM), lambda g: (g, 0, 0)),
        scratch_shapes=[pltpu.VMEM((n_chunks + SUBLANES, HEAD_DIM), F32)],
        compiler_params=_params("parallel"),
        name="compress",
    )(x, pe, w, gain)


def _dot_nt(a, b):
    return lax.dot_general(a, b, (((1,), (1,)), ((), ())), preferred_element_type=F32)


def _biased_softmax2(s, bias):
    s = s + bias
    p = jnp.exp2(s - jnp.max(s, axis=-1, keepdims=True))
    return p, jnp.sum(p, axis=-1, keepdims=True)


def _stack_heads(q_ref):
    return jnp.concatenate([q_ref[:, z * HEAD_DIM:(z + 1) * HEAD_DIM] for z in range(Q_PER_KV)], axis=0)


def _cw_attn_kernel(*refs, n_chunks):
    q_ref, kc_ref, vc_ref, ovt_ref = refs[:4]
    kw_refs = refs[4:4 + N_WIN_BLK]
    vw_refs = refs[4 + N_WIN_BLK:4 + 2 * N_WIN_BLK]
    g_ref, ocw_ref, sel_ref, score_ref, cnt_ref = refs[4 + 2 * N_WIN_BLK:]
    qb = pl.program_id(1)
    t0 = qb * Q_TILE
    q4 = _stack_heads(q_ref)
    head_rows = [slice(z * Q_TILE, (z + 1) * Q_TILE) for z in range(Q_PER_KV)]

    tq = t0 + lax.broadcasted_iota(jnp.int32, (Q_TILE, n_chunks), 0)
    n_id = lax.broadcasted_iota(jnp.int32, (Q_TILE, n_chunks), 1)
    bias_c = jnp.where((n_id * CMP_STRIDE + (CMP_LEN - 1) <= tq) & (n_id < n_chunks - 1), 0.0, NEG_INF)
    row_ok = jnp.where(t0 + lax.broadcasted_iota(jnp.int32, (Q_TILE, 1), 0) >= CMP_LEN - 1, 1.0, 0.0)
    s_c = _dot_nt(q4, kc_ref[...])
    p_heads = []
    for r in head_rows:
        p, l = _biased_softmax2(s_c[r], bias_c)
        p_heads.append(p * (row_ok / jnp.maximum(l, 1e-30)))
    o_c = jnp.dot(jnp.concatenate(p_heads, axis=0).astype(BF16), vc_ref[...],
                  preferred_element_type=F32)

    p_sum = p_heads[0]
    for z in range(1, Q_PER_KV):
        p_sum = p_sum + p_heads[z]
    p_hi = p_sum.astype(BF16)
    p_lo = (p_sum - p_hi.astype(F32)).astype(BF16)
    ovt = ovt_ref[...]
    imp_t = _dot_nt(ovt, p_hi) + _dot_nt(ovt, p_lo)
    j_id = lax.broadcasted_iota(jnp.int32, (LANES, Q_TILE), 0)
    cur = (t0 + lax.broadcasted_iota(jnp.int32, (LANES, Q_TILE), 1)) // SEL_BLOCK
    valid = j_id <= cur
    forced = (j_id == 0) | (valid & (j_id > cur - N_LOCAL_SEL))
    score_ref[...] = jnp.where(forced, FORCE_SCORE, jnp.where(valid, imp_t, -1.0))
    cnt_ref[...] = jnp.zeros(cnt_ref.shape, F32)
    n_grp = LANES // SUBLANES
    last_src_grp = ((t0 + Q_TILE - 1) // SEL_BLOCK) // SUBLANES
    sub = lax.broadcasted_iota(jnp.int32, (SUBLANES, Q_TILE), 0)
    for gj in range(n_grp):
        @pl.when(gj <= last_src_grp)
        def _():
            src = score_ref[gj * SUBLANES:(gj + 1) * SUBLANES, :]
            rows_b = [jnp.broadcast_to(src[r:r + 1, :], (SUBLANES, Q_TILE)) for r in range(SUBLANES)]
            for gi in range(n_grp):
                tgt = score_ref[gi * SUBLANES:(gi + 1) * SUBLANES, :]
                acc = cnt_ref[gi * SUBLANES:(gi + 1) * SUBLANES, :]
                for r in range(SUBLANES):
                    if gi < gj:
                        inc = jnp.where(rows_b[r] > tgt, 1.0, 0.0)
                    elif gi > gj:
                        inc = jnp.where(rows_b[r] >= tgt, 1.0, 0.0)
                    else:
                        inc = jnp.where(sub > r, jnp.where(rows_b[r] >= tgt, 1.0, 0.0),
                                        jnp.where(rows_b[r] > tgt, 1.0, 0.0))
                    acc = acc + inc
                cnt_ref[gi * SUBLANES:(gi + 1) * SUBLANES, :] = acc
    bias_t = jnp.where(cnt_ref[...] < float(SEL_TOP), 0.0, NEG_INF)
    sel_ref[...] = bias_t.T.astype(sel_ref.dtype)

    n_keys = N_WIN_BLK * Q_TILE
    tq_w = t0 + lax.broadcasted_iota(jnp.int32, (Q_TILE, n_keys), 0)
    pos = t0 - WINDOW + lax.broadcasted_iota(jnp.int32, (Q_TILE, n_keys), 1)
    bias_w = jnp.where((pos <= tq_w) & (pos > tq_w - WINDOW) & (pos >= 0), 0.0, NEG_INF)
    s_w = jnp.concatenate([_dot_nt(q4, kw_refs[i][...]) for i in range(N_WIN_BLK)], axis=1)
    pw_heads = []
    for r in head_rows:
        p, l = _biased_softmax2(s_w[r], bias_w)
        pw_heads.append((p * (1.0 / jnp.maximum(l, 1e-30))).astype(BF16))
    p_w = jnp.concatenate(pw_heads, axis=0)
    o_w = jnp.dot(p_w[:, 0:Q_TILE], vw_refs[0][...], preferred_element_type=F32)
    for i in range(1, N_WIN_BLK):
        o_w = o_w + jnp.dot(p_w[:, i * Q_TILE:(i + 1) * Q_TILE], vw_refs[i][...],
                            preferred_element_type=F32)

    gates = g_ref[...]
    for z, r in enumerate(head_rows):
        g_c = gates[:, z * N_NSA_BRANCHES:z * N_NSA_BRANCHES + 1]
        g_w = gates[:, z * N_NSA_BRANCHES + 2:z * N_NSA_BRANCHES + 3]
        ocw_ref[:, z * HEAD_DIM:(z + 1) * HEAD_DIM] = g_c * o_c[r] + g_w * o_w[r]


def _cw_attention(q, k_cmp, v_cmp, ovt, kw, vw, gates):
    s = q.shape[0]
    n_chunks = k_cmp.shape[1]
    n_qb = s // Q_TILE
    grp_w = Q_PER_KV * HEAD_DIM

    def win_spec(i):
        return pl.BlockSpec((Q_TILE, HEAD_DIM),
                            lambda g, qb: (jnp.maximum(qb - (N_WIN_BLK - 1) + i, 0), g))

    cmp_spec = pl.BlockSpec((None, n_chunks, HEAD_DIM), lambda g, qb: (g, 0, 0))
    in_specs = ([pl.BlockSpec((Q_TILE, grp_w), lambda g, qb: (qb, g)), cmp_spec, cmp_spec,
                 pl.BlockSpec((LANES, n_chunks), lambda g, qb: (0, 0))]
                + [win_spec(i) for i in range(N_WIN_BLK)] * 2
                + [pl.BlockSpec((Q_TILE, LANES), lambda g, qb: (qb, g))])
    return pl.pallas_call(
        functools.partial(_cw_attn_kernel, n_chunks=n_chunks),
        out_shape=[jax.ShapeDtypeStruct((s, ATTN_WIDTH), F32),
                   jax.ShapeDtypeStruct((N_KV_GROUPS, s, LANES), BF16)],
        grid=(N_KV_GROUPS, n_qb),
        in_specs=in_specs,
        out_specs=[pl.BlockSpec((Q_TILE, grp_w), lambda g, qb: (qb, g)),
                   pl.BlockSpec((None, Q_TILE, LANES), lambda g, qb: (g, qb, 0))],
        scratch_shapes=[pltpu.VMEM((LANES, Q_TILE), F32), pltpu.VMEM((LANES, Q_TILE), F32)],
        compiler_params=_params("parallel", "parallel"),
        name="cmp_win_attention",
    )(q, k_cmp, v_cmp, ovt, *([kw] * N_WIN_BLK), *([vw] * N_WIN_BLK), gates)


def _sel_attn_kernel(qb_ref, kt_ref, q_ref, sb_ref, k_ref, vprev_ref, vlast_ref, ocw_ref, g_ref, o_ref,
                     qx_ref, m_ref, l_ref, acc_ref, p_ref, alpha_ref):
    step = pl.program_id(1)
    qb = qb_ref[step]
    kt = kt_ref[step]
    last_kt = (qb * SEL_Q_TILE) // K_TILE
    n_lane_blk = K_TILE // LANES
    slot = kt % 2
    head_rows = [slice(z * SEL_Q_TILE, (z + 1) * SEL_Q_TILE) for z in range(Q_PER_KV)]

    @pl.when(kt == 0)
    def _():
        for z, r in enumerate(head_rows):
            qx_ref[r, 0:HEAD_DIM] = q_ref[:, z * HEAD_DIM:(z + 1) * HEAD_DIM]
            qx_ref[r, HEAD_DIM:2 * HEAD_DIM] = sb_ref[...]
        m_ref[...] = jnp.full(m_ref.shape, NEG_INF, F32)
        l_ref[...] = jnp.zeros(l_ref.shape, F32)
        acc_ref[...] = jnp.zeros(acc_ref.shape, F32)
        p_ref[1] = jnp.zeros(p_ref.shape[1:], BF16)
        alpha_ref[1] = jnp.ones(alpha_ref.shape[1:], F32)

    def apply_pv(src_slot, v_ref):
        v = v_ref[...]
        for r in head_rows:
            acc_ref[r, :] = alpha_ref[src_slot, r, :] * acc_ref[r, :] + jnp.dot(
                p_ref[src_slot, r, :], v, preferred_element_type=F32)

    def update(causal):
        apply_pv(1 - slot, vprev_ref)
        k = k_ref[...]
        if causal:
            tq = qb * SEL_Q_TILE + lax.broadcasted_iota(jnp.int32, (SEL_Q_TILE, K_TILE), 0)
            kpos = kt * K_TILE + lax.broadcasted_iota(jnp.int32, (SEL_Q_TILE, K_TILE), 1)
            cbias = jnp.where(kpos <= tq, 0.0, NEG_INF)
        for r in head_rows:
            s = _dot_nt(qx_ref[r, :], k)
            if causal:
                s = s + cbias
            blk = [s[:, c * LANES:(c + 1) * LANES] for c in range(n_lane_blk)]
            mx = blk[0]
            for c in range(1, n_lane_blk):
                mx = jnp.maximum(mx, blk[c])
            m_old = m_ref[r, :]
            m_new = jnp.maximum(m_old, jnp.max(mx, axis=-1, keepdims=True))
            alpha = jnp.exp2(m_old - m_new)
            ps = [jnp.exp2(b - m_new) for b in blk]
            l_add = ps[0]
            for c in range(1, n_lane_blk):
                l_add = l_add + ps[c]
            l_ref[r, :] = alpha * l_ref[r, :] + l_add
            m_ref[r, :] = m_new
            alpha_ref[slot, r, :] = alpha
            p_ref[slot, r, :] = jnp.concatenate([x.astype(BF16) for x in ps], axis=1)

    @pl.when(kt < last_kt)
    def _():
        update(False)

    @pl.when(kt == last_kt)
    def _():
        update(True)
        apply_pv(slot, vlast_ref)
        gates = g_ref[...]
        for z, r in enumerate(head_rows):
            cols = slice(z * HEAD_DIM, (z + 1) * HEAD_DIM)
            g_s = gates[:, z * N_NSA_BRANCHES + 1:z * N_NSA_BRANCHES + 2]
            l_row = jnp.sum(l_ref[r, :], axis=-1, keepdims=True)
            o_s = acc_ref[r, :] / jnp.maximum(l_row, 1e-30)
            o_ref[:, cols] = (ocw_ref[:, cols] + g_s * o_s).astype(o_ref.dtype)


def _sel_attention(q, sel_bias, ks_ext, vs, ocw, gates):
    s = q.shape[0]
    n_qb = s // SEL_Q_TILE
    grp_w = Q_PER_KV * HEAD_DIM
    qb_of, kt_of = [], []
    for qb in range(n_qb):
        for kt in range((qb * SEL_Q_TILE) // K_TILE + 1):
            qb_of.append(qb)
            kt_of.append(kt)
    qb_arr = jnp.asarray(np.asarray(qb_of, np.int32))
    kt_arr = jnp.asarray(np.asarray(kt_of, np.int32))
    grid_spec = pltpu.PrefetchScalarGridSpec(
        num_scalar_prefetch=2,
        grid=(N_KV_GROUPS, len(qb_of)),
        in_specs=[pl.BlockSpec((SEL_Q_TILE, grp_w), lambda g, i, qbr, ktr: (qbr[i], g)),
                  pl.BlockSpec((None, SEL_Q_TILE, LANES), lambda g, i, qbr, ktr: (g, qbr[i], 0)),
                  pl.BlockSpec((K_TILE, 2 * HEAD_DIM), lambda g, i, qbr, ktr: (ktr[i], g)),
                  pl.BlockSpec((K_TILE, HEAD_DIM), lambda g, i, qbr, ktr: (jnp.maximum(ktr[i] - 1, 0), g)),
                  pl.BlockSpec((K_TILE, HEAD_DIM),
                               lambda g, i, qbr, ktr: ((qbr[i] * SEL_Q_TILE) // K_TILE, g)),
                  pl.BlockSpec((SEL_Q_TILE, grp_w), lambda g, i, qbr, ktr: (qbr[i], g)),
                  pl.BlockSpec((SEL_Q_TILE, LANES), lambda g, i, qbr, ktr: (qbr[i], g))],
        out_specs=pl.BlockSpec((SEL_Q_TILE, grp_w), lambda g, i, qbr, ktr: (qbr[i], g)),
        scratch_shapes=[pltpu.VMEM((SEL_ROWS, 2 * HEAD_DIM), BF16),
                        pltpu.VMEM((SEL_ROWS, LANES), F32),
                        pltpu.VMEM((SEL_ROWS, LANES), F32),
                        pltpu.VMEM((SEL_ROWS, HEAD_DIM), F32),
                        pltpu.VMEM((2, SEL_ROWS, K_TILE), BF16),
                        pltpu.VMEM((2, SEL_ROWS, LANES), F32)],
    )
    return pl.pallas_call(
        _sel_attn_kernel,
        out_shape=jax.ShapeDtypeStruct((s, ATTN_WIDTH), BF16),
        grid_spec=grid_spec,
        compiler_params=_params("parallel", "arbitrary"),
        name="sel_attention",
    )(qb_arr, kt_arr, q, sel_bias, ks_ext, vs, vs, ocw, gates)


def _rnn_kernel(rx_ref, ry_ref, cw_ref, cb_ref, wa_ref, ba_ref, wx_ref, bx_ref, lam_ref, o_ref,
                ext_ref, a_ref, b_ref, h_ref, *, tt, tc):
    ti = pl.program_id(1)
    n_blk = tc // RNN_BLOCK_DIM
    halo = SUBLANES

    @pl.when(ti == 0)
    def _():
        ext_ref[0:halo, :] = jnp.zeros((halo, tc), F32)
        h_ref[...] = jnp.zeros(h_ref.shape, F32)

    @pl.when(ti > 0)
    def _():
        ext_ref[0:halo, :] = ext_ref[tt:tt + halo, :]

    ext_ref[halo:halo + tt, :] = rx_ref[...]
    xr = cb_ref[...] + ext_ref[pl.ds(halo - (RNN_CONV - 1), tt), :] * cw_ref[0:1, :]
    for j in range(1, RNN_CONV):
        xr = xr + ext_ref[pl.ds(halo - (RNN_CONV - 1) + j, tt), :] * cw_ref[j:j + 1, :]

    sp = jnp.maximum(-lam_ref[...], 0.0) + jnp.log(1.0 + jnp.exp(-jnp.abs(lam_ref[...])))
    xb = xr.astype(BF16)
    for blk in range(n_blk):
        cols = slice(blk * RNN_BLOCK_DIM, (blk + 1) * RNN_BLOCK_DIM)
        xs = xb[:, cols]
        r = jax.nn.sigmoid(jnp.dot(xs, wa_ref[blk], preferred_element_type=F32) + ba_ref[:, cols])
        i = jax.nn.sigmoid(jnp.dot(xs, wx_ref[blk], preferred_element_type=F32) + bx_ref[:, cols])
        log_a = -RG_C * r * sp[:, cols]
        a_ref[:, cols] = jnp.exp(log_a)
        b_ref[:, cols] = jnp.sqrt(1.0 - jnp.exp(2.0 * log_a)) * (i * xr[:, cols])

    row = lax.broadcasted_iota(jnp.int32, (SUBLANES, tc), 0)

    def scan_rows(i, carry):
        r0 = pl.multiple_of(i * SUBLANES, SUBLANES)
        a8 = a_ref[pl.ds(r0, SUBLANES), :]
        b8 = b_ref[pl.ds(r0, SUBLANES), :]
        for d in (1, 2, 4):
            keep = row >= d
            a_sh = pltpu.roll(a8, d, axis=0)
            b_sh = pltpu.roll(b8, d, axis=0)
            b8 = jnp.where(keep, a8 * b_sh + b8, b8)
            a8 = jnp.where(keep, a8 * a_sh, a8)
        h8 = a8 * carry + b8
        b_ref[pl.ds(r0, SUBLANES), :] = h8
        return jnp.broadcast_to(h8[SUBLANES - 1:SUBLANES, :], (SUBLANES, tc))

    h_ref[...] = lax.fori_loop(0, tt // SUBLANES, scan_rows, h_ref[...])
    o_ref[...] = (b_ref[...] * jax.nn.gelu(ry_ref[...], approximate=True)).astype(o_ref.dtype)


def _rnn_branch(proj, cw, cb, wa, ba, wx, bx, lam, *, tt, tc):
    s = proj.shape[0]
    n_cb = RNN_WIDTH // tc
    blk_per = tc // RNN_BLOCK_DIM
    rx_blk = COL_RX // tc
    ry_blk = COL_RY // tc
    vec = pl.BlockSpec((1, tc), lambda c, t: (0, c))
    wspec = pl.BlockSpec((blk_per, RNN_BLOCK_DIM, RNN_BLOCK_DIM), lambda c, t: (c, 0, 0))
    return pl.pallas_call(
        functools.partial(_rnn_kernel, tt=tt, tc=tc),
        out_shape=jax.ShapeDtypeStruct((s, RNN_WIDTH), BF16),
        grid=(n_cb, s // tt),
        in_specs=[pl.BlockSpec((tt, tc), lambda c, t: (t, rx_blk + c)),
                  pl.BlockSpec((tt, tc), lambda c, t: (t, ry_blk + c)),
                  pl.BlockSpec((RNN_CONV, tc), lambda c, t: (0, c)),
                  vec, wspec, vec, wspec, vec, vec],
        out_specs=pl.BlockSpec((tt, tc), lambda c, t: (t, c)),
        scratch_shapes=[pltpu.VMEM((tt + SUBLANES, tc), F32),
                        pltpu.VMEM((tt, tc), F32),
                        pltpu.VMEM((tt, tc), F32),
                        pltpu.VMEM((SUBLANES, tc), F32)],
        compiler_params=_params("parallel", "arbitrary"),
        name="rg_lru",
    )(proj, proj, cw, cb, wa, ba, wx, bx, lam)


def _overlap_t(s):
    n_chunks = s // CMP_STRIDE
    n_cmp = n_chunks - 1
    n_sel = s // SEL_BLOCK
    cmp_start = np.arange(n_cmp) * CMP_STRIDE
    sel_start = np.arange(n_sel) * SEL_BLOCK
    ov = np.clip(np.minimum(cmp_start[:, None] + CMP_LEN, sel_start[None, :] + SEL_BLOCK)
                 - np.maximum(cmp_start[:, None], sel_start[None, :]), 0, None) / CMP_LEN
    out = np.zeros((LANES, n_chunks), np.float32)
    out[:n_sel, :n_cmp] = ov.T
    return out


def _block_onehot(s):
    return (np.arange(s)[:, None] // SEL_BLOCK == np.arange(LANES)[None, :]).astype(np.float32)


def _gate_weight(w_in):
    depth = w_in.shape[0]
    cols = w_in[:, :, GATE_SRC:GATE_SRC + LANES][:, :, :N_Q_HEADS * N_NSA_BRANCHES]
    per = Q_PER_KV * N_NSA_BRANCHES
    cols = cols.reshape(depth, D_MODEL, N_KV_GROUPS, per)
    cols = jnp.pad(cols, ((0, 0), (0, 0), (0, 0), (0, GATE_LANES - per)))
    return cols.reshape(depth, D_MODEL, LANES).astype(BF16)


def kernel(x, c, positions, w_cond, b_cond, w_mod, b_mod, norm_mix, norm_ffn, w_in, q_norm, k_norm, cmp_pe_k, cmp_w_k, cmp_pe_v, cmp_w_v, rnn_conv_w, rnn_conv_b, rg_w_a, rg_b_a, rg_w_x, rg_b_x, rg_lambda, w_attn_up, w_rnn_up, w_out, w_ffn_in, ffn_conv_w, ffn_conv_b, w_ffn_down):
    b, s, d = x.shape
    depth = w_in.shape[0]
    assert b == 1 and d == D_MODEL and s % 2048 == 0 and s // SEL_BLOCK <= LANES

    c_emb = _vecmat(c.reshape(1, d, 1), w_cond[None], b_cond.reshape(1, 1, -1), silu=True, tn=w_cond.shape[1])
    c_col = jnp.broadcast_to(c_emb.reshape(1, -1, 1), (depth, c_emb.shape[-1], 1))
    mod = _vecmat(c_col, w_mod, b_mod[:, None, :], silu=False, tn=2048)

    inv_freq = ROPE_THETA ** (-jnp.arange(0, ROT_DIM, 2, dtype=jnp.float32) / ROT_DIM)
    freq_row = jnp.concatenate([inv_freq, inv_freq, jnp.zeros((LANES - ROT_DIM,), F32)])[None, :]
    rope_c, rope_sa, rope_sb = _rope_tables(positions.reshape(s, 1), freq_row, tr=512)

    ovt = jnp.asarray(_overlap_t(s), BF16)
    blk_onehot = jnp.asarray(_block_onehot(s), BF16)

    w_attn_in = w_in[:, :, :GATE_SRC].astype(BF16)
    w_rest_in = w_in[:, :, REST_SRC:].astype(BF16)
    w_gate = _gate_weight(w_in)
    w_attn_up_b = w_attn_up.astype(BF16)
    w_rnn_up_b = w_rnn_up.astype(BF16)
    w_out_b = w_out.astype(BF16)
    w_ffn_in_b = w_ffn_in.astype(BF16)
    w_ffn_down_b = w_ffn_down.astype(BF16)

    h = x.reshape(s, d)
    for l in range(depth):
        sh1, sc1, g1, sh2, sc2, g2 = [mod[l, :, i * d:(i + 1) * d] for i in range(N_MOD)]
        proj_a, graw, u = _norm_matmul(h, norm_mix[l][None], sc1, sh1, w_attn_in, w_gate,
                                       layer=l, tm=512, tn=1024, out_dtype=F32)
        proj = _matmul(u, w_rest_in, layer=l, tm=512, tn=1024, out_dtype=F32)
        q, kc, ks_ext, vs, kw, vw, gates = _prep(proj_a, graw, rope_c, rope_sa, rope_sb,
                                                 q_norm[l][None], k_norm[l], blk_onehot, tr=256)
        k_cmp = _compress(kc, 0, cmp_pe_k[l], cmp_w_k[l].astype(BF16), k_norm[l][0:1], do_norm=True)
        v_cmp = _compress(proj_a, COL_VC // HEAD_DIM, cmp_pe_v[l], cmp_w_v[l].astype(BF16),
                          k_norm[l][0:1], do_norm=False)
        ocw, sel_bias = _cw_attention(q, k_cmp, v_cmp, ovt, kw, vw, gates)
        attn = _sel_attention(q, sel_bias, ks_ext, vs, ocw, gates)
        rnn = _rnn_branch(proj, rnn_conv_w[l], rnn_conv_b[l][None], rg_w_a[l].astype(BF16), rg_b_a[l][None],
                          rg_w_x[l].astype(BF16), rg_b_x[l][None], rg_lambda[l][None], tt=512, tc=1024)
        merged = _merge(attn, rnn, w_attn_up_b, w_rnn_up_b, proj, layer=l, tm=512, tn=1024)
        h = _matmul_residual(merged, w_out_b, h, g1, layer=l, tm=512, tn=1024)
        act = _ffn_in(h, norm_ffn[l][None], sc2, sh2, w_ffn_in_b, ffn_conv_w[l], ffn_conv_b[l][None],
                      layer=l, tm=512, tn=512)
        h = _matmul_residual(act, w_ffn_down_b, h, g2, layer=l, tm=512, tn=512)
    return h.reshape(b, s, d)
```

```python
import functools

import numpy as np
import jax
import jax.numpy as jnp
from jax import lax
from jax.experimental import pallas as pl
from jax.experimental.pallas import tpu as pltpu

F32 = jnp.float32
BF16 = jnp.bfloat16

D_MODEL = 4096
N_Q_HEADS = 16
N_KV_GROUPS = 4
HEAD_DIM = 128
Q_PER_KV = N_Q_HEADS // N_KV_GROUPS
ATTN_WIDTH = N_Q_HEADS * HEAD_DIM
KV_WIDTH = N_KV_GROUPS * HEAD_DIM
N_NSA_BRANCHES = 3
ROT_DIM = HEAD_DIM // 4
ROPE_THETA = 500000.0
CMP_LEN = 32
CMP_STRIDE = 16
SEL_BLOCK = 64
SEL_TOP = 16
N_LOCAL_SEL = 2
WINDOW = 512
FORCE_SCORE = 2.0 * Q_PER_KV + 1.0
RNN_WIDTH = 2048
RNN_BLOCKS = 16
RNN_BLOCK_DIM = RNN_WIDTH // RNN_BLOCKS
RNN_CONV = 4
RG_C = 8.0
D_FF = 2 * D_MODEL
FFN_CONV = 3
N_MOD = 6
EPS = 1e-6
NEG_INF = -1e30
ATTN_SCALE = HEAD_DIM ** -0.5
LOG2_E = 1.4426950408889634
Q_SCALE = ATTN_SCALE * LOG2_E

LANES = 128
SUBLANES = 8
VMEM_LIMIT_BYTES = 56 * 1024 * 1024

COL_Q = 0
COL_KC = ATTN_WIDTH
COL_VC = COL_KC + KV_WIDTH
COL_KS = COL_VC + KV_WIDTH
COL_VS = COL_KS + KV_WIDTH
COL_KW = COL_VS + KV_WIDTH
COL_VW = COL_KW + KV_WIDTH
N_PROJ_ATTN = COL_VW + KV_WIDTH
GATE_SRC = N_PROJ_ATTN
REST_SRC = GATE_SRC + N_Q_HEADS * N_NSA_BRANCHES
COL_RX = 0
COL_RY = COL_RX + RNN_WIDTH
COL_GA = COL_RY + RNN_WIDTH
COL_GR = COL_GA + D_MODEL

Q_TILE = 128
K_TILE = 1024
SEL_Q_TILE = 256
SEL_ROWS = Q_PER_KV * SEL_Q_TILE
N_WIN_BLK = WINDOW // Q_TILE + 1
NORM_CHUNK = 64
GATE_LANES = LANES // N_KV_GROUPS


def _params(*sem):
    return pltpu.CompilerParams(dimension_semantics=sem, vmem_limit_bytes=VMEM_LIMIT_BYTES)


def _vecmat_kernel(x_ref, w_ref, b_ref, o_ref, *, silu):
    y = jnp.sum(w_ref[...] * x_ref[...], axis=0, keepdims=True) + b_ref[...]
    if silu:
        y = y * jax.nn.sigmoid(y)
    o_ref[...] = y


def _vecmat(x_col, w, b, *, silu, tn):
    n_l, k, n = w.shape
    return pl.pallas_call(
        functools.partial(_vecmat_kernel, silu=silu),
        out_shape=jax.ShapeDtypeStruct((n_l, 1, n), F32),
        grid=(n_l, n // tn),
        in_specs=[pl.BlockSpec((None, k, 1), lambda l, j: (l, 0, 0)),
                  pl.BlockSpec((None, k, tn), lambda l, j: (l, 0, j)),
                  pl.BlockSpec((None, 1, tn), lambda l, j: (l, 0, j))],
        out_specs=pl.BlockSpec((None, 1, tn), lambda l, j: (l, 0, j)),
        compiler_params=_params("parallel", "parallel"),
        name="vecmat",
    )(x_col, w, b)


def _rope_table_kernel(pos_ref, freq_ref, c_ref, sa_ref, sb_ref):
    ang = pos_ref[...].astype(F32) * freq_ref[...]
    lane = lax.broadcasted_iota(jnp.int32, ang.shape, 1)
    cos = jnp.cos(ang)
    sin = jnp.sin(ang)
    c_ref[...] = jnp.where(lane < ROT_DIM, cos, 1.0)
    sa_ref[...] = jnp.where(lane < ROT_DIM // 2, -sin, 0.0)
    sb_ref[...] = jnp.where((lane >= ROT_DIM // 2) & (lane < ROT_DIM), sin, 0.0)


def _rope_tables(pos_col, freq_row, tr):
    s = pos_col.shape[0]
    spec = pl.BlockSpec((tr, LANES), lambda i: (i, 0))
    return pl.pallas_call(
        _rope_table_kernel,
        out_shape=[jax.ShapeDtypeStruct((s, LANES), F32)] * 3,
        grid=(s // tr,),
        in_specs=[pl.BlockSpec((tr, 1), lambda i: (i, 0)),
                  pl.BlockSpec((1, LANES), lambda i: (0, 0))],
        out_specs=[spec, spec, spec],
        compiler_params=_params("parallel"),
        name="rope_tables",
    )(pos_col, freq_row)


def _ada_norm(h_ref, gain_ref, sc_ref, sh_ref, u_ref):
    for r0 in range(0, h_ref.shape[0], NORM_CHUNK):
        x = h_ref[r0:r0 + NORM_CHUNK, :]
        y = x * lax.rsqrt(jnp.mean(x * x, axis=-1, keepdims=True) + EPS)
        u = (y * gain_ref[...]) * (1.0 + sc_ref[...]) + sh_ref[...]
        u_ref[r0:r0 + NORM_CHUNK, :] = u.astype(BF16)


def _norm_mm_kernel(h_ref, gain_ref, sc_ref, sh_ref, w_ref, wg_ref, o_ref, og_ref, u_ref):
    @pl.when(pl.program_id(1) == 0)
    def _():
        _ada_norm(h_ref, gain_ref, sc_ref, sh_ref, u_ref)
        og_ref[...] = jnp.dot(u_ref[...], wg_ref[...], preferred_element_type=F32)

    o_ref[...] = jnp.dot(u_ref[...], w_ref[...], preferred_element_type=F32).astype(o_ref.dtype)


def _norm_matmul(h, gain, sc, sh, w, wg, *, layer, tm, tn, out_dtype):
    s, d = h.shape
    n = w.shape[2]
    ng = wg.shape[2]
    row = pl.BlockSpec((1, d), lambda i, j: (0, 0))
    return pl.pallas_call(
        _norm_mm_kernel,
        out_shape=[jax.ShapeDtypeStruct((s, n), out_dtype), jax.ShapeDtypeStruct((s, ng), F32),
                   jax.ShapeDtypeStruct((s, d), BF16)],
        grid=(s // tm, n // tn),
        in_specs=[pl.BlockSpec((tm, d), lambda i, j: (i, 0)), row, row, row,
                  pl.BlockSpec((None, d, tn), lambda i, j: (layer, 0, j)),
                  pl.BlockSpec((None, d, ng), lambda i, j: (layer, 0, 0))],
        out_specs=[pl.BlockSpec((tm, tn), lambda i, j: (i, j)),
                   pl.BlockSpec((tm, ng), lambda i, j: (i, 0)),
                   pl.BlockSpec((tm, d), lambda i, j: (i, 0))],
        compiler_params=_params("parallel", "arbitrary"),
        name="norm_matmul",
    )(h, gain, sc, sh, w, wg)


def _mm_kernel(a_ref, w_ref, o_ref):
    o_ref[...] = jnp.dot(a_ref[...], w_ref[...], preferred_element_type=F32).astype(o_ref.dtype)


def _matmul(a, w, *, layer, tm, tn, out_dtype):
    s, k = a.shape
    n = w.shape[2]
    return pl.pallas_call(
        _mm_kernel,
        out_shape=jax.ShapeDtypeStruct((s, n), out_dtype),
        grid=(s // tm, n // tn),
        in_specs=[pl.BlockSpec((tm, k), lambda i, j: (i, 0)),
                  pl.BlockSpec((None, k, tn), lambda i, j: (layer, 0, j))],
        out_specs=pl.BlockSpec((tm, tn), lambda i, j: (i, j)),
        compiler_params=_params("parallel", "parallel"),
        name="matmul",
    )(a, w)


def _ffn_in_kernel(h_ref, gain_ref, sc_ref, sh_ref, wg_ref, wu_ref, cw_ref, cb_ref, o_ref,
                   u_ref, ext_ref, carry_ref, *, tm):
    i = pl.program_id(0)
    j = pl.program_id(1)
    halo = SUBLANES

    @pl.when(j == 0)
    def _():
        _ada_norm(h_ref, gain_ref, sc_ref, sh_ref, u_ref)

    @pl.when(i == 0)
    def _():
        ext_ref[0:halo, :] = jnp.zeros((halo, ext_ref.shape[1]), F32)

    @pl.when(i > 0)
    def _():
        ext_ref[0:halo, :] = carry_ref[j]

    ext_ref[halo:halo + tm, :] = jnp.dot(u_ref[...], wg_ref[...], preferred_element_type=F32)
    carry_ref[j] = ext_ref[tm:tm + halo, :]
    y = cb_ref[...] + ext_ref[pl.ds(halo - (FFN_CONV - 1), tm), :] * cw_ref[0:1, :]
    for t in range(1, FFN_CONV):
        y = y + ext_ref[pl.ds(halo - (FFN_CONV - 1) + t, tm), :] * cw_ref[t:t + 1, :]
    up = jnp.dot(u_ref[...], wu_ref[...], preferred_element_type=F32)
    o_ref[...] = (y * jax.nn.sigmoid(y) * up).astype(o_ref.dtype)


def _ffn_in(h, gain, sc, sh, w, cw, cb, *, layer, tm, tn):
    s, d = h.shape
    n_j = D_FF // tn
    row = pl.BlockSpec((1, d), lambda i, j: (0, 0))
    return pl.pallas_call(
        functools.partial(_ffn_in_kernel, tm=tm),
        out_shape=jax.ShapeDtypeStruct((s, D_FF), BF16),
        grid=(s // tm, n_j),
        in_specs=[pl.BlockSpec((tm, d), lambda i, j: (i, 0)), row, row, row,
                  pl.BlockSpec((None, d, tn), lambda i, j: (layer, 0, j)),
                  pl.BlockSpec((None, d, tn), lambda i, j: (layer, 0, n_j + j)),
                  pl.BlockSpec((FFN_CONV, tn), lambda i, j: (0, j)),
                  pl.BlockSpec((1, tn), lambda i, j: (0, j))],
        out_specs=pl.BlockSpec((tm, tn), lambda i, j: (i, j)),
        scratch_shapes=[pltpu.VMEM((tm, d), BF16),
                        pltpu.VMEM((tm + SUBLANES, tn), F32),
                        pltpu.VMEM((n_j, SUBLANES, tn), F32)],
        compiler_params=_params("arbitrary", "arbitrary"),
        name="ffn_in",
    )(h, gain, sc, sh, w, w, cw, cb)


def _mm_res_kernel(a_ref, w_ref, h_ref, g_ref, o_ref):
    y = jnp.dot(a_ref[...], w_ref[...], preferred_element_type=F32)
    o_ref[...] = h_ref[...] + g_ref[...] * y


def _matmul_residual(a, w, h, g, *, layer, tm, tn):
    s, k = a.shape
    n = w.shape[2]
    return pl.pallas_call(
        _mm_res_kernel,
        out_shape=jax.ShapeDtypeStruct((s, n), F32),
        grid=(s // tm, n // tn),
        in_specs=[pl.BlockSpec((tm, k), lambda i, j: (i, 0)),
                  pl.BlockSpec((None, k, tn), lambda i, j: (layer, 0, j)),
                  pl.BlockSpec((tm, tn), lambda i, j: (i, j)),
                  pl.BlockSpec((1, tn), lambda i, j: (0, j))],
        out_specs=pl.BlockSpec((tm, tn), lambda i, j: (i, j)),
        compiler_params=_params("parallel", "parallel"),
        name="matmul_residual",
    )(a, w, h, g)


def _merge_kernel(attn_ref, rnn_ref, wa_ref, wr_ref, ga_ref, gr_ref, o_ref):
    ya = jnp.dot(attn_ref[...], wa_ref[...], preferred_element_type=F32)
    yr = jnp.dot(rnn_ref[...], wr_ref[...], preferred_element_type=F32)
    o_ref[...] = (jax.nn.sigmoid(ga_ref[...]) * ya + jax.nn.sigmoid(gr_ref[...]) * yr).astype(o_ref.dtype)


def _merge(attn, rnn, wa, wr, proj, *, layer, tm, tn):
    s, k = attn.shape
    n = wa.shape[2]
    ga_blk = COL_GA // tn
    gr_blk = COL_GR // tn
    return pl.pallas_call(
        _merge_kernel,
        out_shape=jax.ShapeDtypeStruct((s, n), BF16),
        grid=(s // tm, n // tn),
        in_specs=[pl.BlockSpec((tm, k), lambda i, j: (i, 0)),
                  pl.BlockSpec((tm, k), lambda i, j: (i, 0)),
                  pl.BlockSpec((None, k, tn), lambda i, j: (layer, 0, j)),
                  pl.BlockSpec((None, k, tn), lambda i, j: (layer, 0, j)),
                  pl.BlockSpec((tm, tn), lambda i, j: (i, ga_blk + j)),
                  pl.BlockSpec((tm, tn), lambda i, j: (i, gr_blk + j))],
        out_specs=pl.BlockSpec((tm, tn), lambda i, j: (i, j)),
        compiler_params=_params("parallel", "parallel"),
        name="merge",
    )(attn, rnn, wa, wr, proj, proj)


def _head_norm(x, gain):
    return x * lax.rsqrt(jnp.mean(x * x, axis=-1, keepdims=True) + EPS) * gain


def _rope(x, c, sa, sb):
    return (x * c + pltpu.roll(x, LANES - ROT_DIM // 2, axis=1) * sa
            + pltpu.roll(x, ROT_DIM // 2, axis=1) * sb)


def _prep_kernel(p_ref, graw_ref, c_ref, sa_ref, sb_ref, qn_ref, kn_ref, blk_ref,
                 q_ref, kc_ref, ks_ref, vs_ref, kw_ref, vw_ref, g_ref):
    c, sa, sb = c_ref[...], sa_ref[...], sb_ref[...]
    qn = qn_ref[...]
    for hd in range(N_Q_HEADS):
        cols = slice(hd * HEAD_DIM, (hd + 1) * HEAD_DIM)
        q_ref[:, cols] = (_rope(_head_norm(p_ref[:, cols], qn), c, sa, sb) * Q_SCALE).astype(BF16)
    for g in range(N_KV_GROUPS):
        cols = slice(g * HEAD_DIM, (g + 1) * HEAD_DIM)

        def src(base):
            return p_ref[:, base + g * HEAD_DIM:base + (g + 1) * HEAD_DIM]

        kc_ref[:, cols] = _rope(src(COL_KC), c, sa, sb)
        ks_ref[:, 2 * g * HEAD_DIM:(2 * g + 1) * HEAD_DIM] = _rope(
            _head_norm(src(COL_KS), kn_ref[1:2, :]), c, sa, sb).astype(BF16)
        ks_ref[:, (2 * g + 1) * HEAD_DIM:(2 * g + 2) * HEAD_DIM] = blk_ref[...]
        kw_ref[:, cols] = _rope(_head_norm(src(COL_KW), kn_ref[2:3, :]), c, sa, sb).astype(BF16)
        vs_ref[:, cols] = src(COL_VS).astype(BF16)
        vw_ref[:, cols] = src(COL_VW).astype(BF16)
    sig = jax.nn.sigmoid(graw_ref[...])
    for g in range(N_KV_GROUPS):
        g_ref[:, g * LANES:(g + 1) * LANES] = sig if g == 0 else pltpu.roll(sig, LANES - g * GATE_LANES, axis=1)


def _prep(proj, graw, c, sa, sb, qn, kn, blk_onehot, *, tr):
    s = proj.shape[0]
    ng = N_KV_GROUPS * LANES
    tab = pl.BlockSpec((tr, LANES), lambda i: (i, 0))
    kv_spec = pl.BlockSpec((tr, KV_WIDTH), lambda i: (i, 0))
    return pl.pallas_call(
        _prep_kernel,
        out_shape=[jax.ShapeDtypeStruct((s, ATTN_WIDTH), BF16),
                   jax.ShapeDtypeStruct((s, KV_WIDTH), F32),
                   jax.ShapeDtypeStruct((s, 2 * KV_WIDTH), BF16),
                   jax.ShapeDtypeStruct((s, KV_WIDTH), BF16),
                   jax.ShapeDtypeStruct((s, KV_WIDTH), BF16),
                   jax.ShapeDtypeStruct((s, KV_WIDTH), BF16),
                   jax.ShapeDtypeStruct((s, ng), F32)],
        grid=(s // tr,),
        in_specs=[pl.BlockSpec((tr, N_PROJ_ATTN), lambda i: (i, 0)),
                  tab, tab, tab, tab,
                  pl.BlockSpec((1, HEAD_DIM), lambda i: (0, 0)),
                  pl.BlockSpec((N_NSA_BRANCHES, HEAD_DIM), lambda i: (0, 0)),
                  tab],
        out_specs=[pl.BlockSpec((tr, ATTN_WIDTH), lambda i: (i, 0)),
                   kv_spec, pl.BlockSpec((tr, 2 * KV_WIDTH), lambda i: (i, 0)),
                   kv_spec, kv_spec, kv_spec,
                   pl.BlockSpec((tr, ng), lambda i: (i, 0))],
        compiler_params=_params("parallel"),
        name="qk_prep",
    )(proj, graw, c, sa, sb, qn, kn, blk_onehot)


def _compress_kernel(x_ref, pe_ref, w_ref, gain_ref, o_ref, shift_ref, *, do_norm, n_chunks):
    acc_a = jnp.zeros((n_chunks, HEAD_DIM), F32)
    acc_b = jnp.zeros((n_chunks, HEAD_DIM), F32)
    for l in range(CMP_STRIDE):
        xl = x_ref[pl.ds(l, n_chunks, stride=CMP_STRIDE), :]
        xa = (xl + pe_ref[l:l + 1, :]).astype(BF16)
        xb = (xl + pe_ref[CMP_STRIDE + l:CMP_STRIDE + l + 1, :]).astype(BF16)
        acc_a = acc_a + jnp.dot(xa, w_ref[l], preferred_element_type=F32)
        acc_b = acc_b + jnp.dot(xb, w_ref[CMP_STRIDE + l], preferred_element_type=F32)
    shift_ref[0:n_chunks, :] = acc_b
    shift_ref[n_chunks:n_chunks + SUBLANES, :] = jnp.zeros((SUBLANES, HEAD_DIM), F32)
    out = acc_a + shift_ref[1:n_chunks + 1, :]
    if do_norm:
        out = _head_norm(out, gain_ref[...])
    o_ref[...] = out.astype(o_ref.dtype)


def _compress(x, col_blk0, pe, w, gain, *, do_norm):
    s = x.shape[0]
    n_chunks = s // CMP_STRIDE
    return pl.pallas_call(
        functools.partial(_compress_kernel, do_norm=do_norm, n_chunks=n_chunks),
        out_shape=jax.ShapeDtypeStruct((N_KV_GROUPS, n_chunks, HEAD_DIM), BF16),
        grid=(N_KV_GROUPS,),
        in_specs=[pl.BlockSpec((s, HEAD_DIM), lambda g: (0, col_blk0 + g)),
                  pl.BlockSpec((CMP_LEN, HEAD_DIM), lambda g: (0, 0)),
                  pl.BlockSpec((CMP_LEN, HEAD_DIM, HEAD_DIM), lambda g: (0, 0, 0)),
                  pl.BlockSpec((1, HEAD_DIM), lambda g: (0, 0))],
        out_specs=pl.BlockSpec((None, n_chunks, HEAD_DIM), lambda g: (g, 0, 0)),
        scratch_shapes=[pltpu.VMEM((n_chunks + SUBLANES, HEAD_DIM), F32)],
        compiler_params=_params("parallel"),
        name="compress",
    )(x, pe, w, gain)


def _dot_nt(a, b):
    return lax.dot_general(a, b, (((1,), (1,)), ((), ())), preferred_element_type=F32)


def _biased_softmax2(s, bias):
    s = s + bias
    p = jnp.exp2(s - jnp.max(s, axis=-1, keepdims=True))
    return p, jnp.sum(p, axis=-1, keepdims=True)


def _stack_heads(q_ref):
    return jnp.concatenate([q_ref[:, z * HEAD_DIM:(z + 1) * HEAD_DIM] for z in range(Q_PER_KV)], axis=0)


def _cw_attn_kernel(*refs, n_chunks):
    q_ref, kc_ref, vc_ref, ovt_ref = refs[:4]
    kw_refs = refs[4:4 + N_WIN_BLK]
    vw_refs = refs[4 + N_WIN_BLK:4 + 2 * N_WIN_BLK]
    g_ref, ocw_ref, sel_ref, score_ref, cnt_ref = refs[4 + 2 * N_WIN_BLK:]
    qb = pl.program_id(1)
    t0 = qb * Q_TILE
    q4 = _stack_heads(q_ref)
    head_rows = [slice(z * Q_TILE, (z + 1) * Q_TILE) for z in range(Q_PER_KV)]

    tq = t0 + lax.broadcasted_iota(jnp.int32, (Q_TILE, n_chunks), 0)
    n_id = lax.broadcasted_iota(jnp.int32, (Q_TILE, n_chunks), 1)
    bias_c = jnp.where((n_id * CMP_STRIDE + (CMP_LEN - 1) <= tq) & (n_id < n_chunks - 1), 0.0, NEG_INF)
    row_ok = jnp.where(t0 + lax.broadcasted_iota(jnp.int32, (Q_TILE, 1), 0) >= CMP_LEN - 1, 1.0, 0.0)
    s_c = _dot_nt(q4, kc_ref[...])
    p_heads = []
    for r in head_rows:
        p, l = _biased_softmax2(s_c[r], bias_c)
        p_heads.append(p * (row_ok / jnp.maximum(l, 1e-30)))
    o_c = jnp.dot(jnp.concatenate(p_heads, axis=0).astype(BF16), vc_ref[...],
                  preferred_element_type=F32)

    p_sum = p_heads[0]
    for z in range(1, Q_PER_KV):
        p_sum = p_sum + p_heads[z]
    p_hi = p_sum.astype(BF16)
    p_lo = (p_sum - p_hi.astype(F32)).astype(BF16)
    ovt = ovt_ref[...]
    imp_t = _dot_nt(ovt, p_hi) + _dot_nt(ovt, p_lo)
    j_id = lax.broadcasted_iota(jnp.int32, (LANES, Q_TILE), 0)
    cur = (t0 + lax.broadcasted_iota(jnp.int32, (LANES, Q_TILE), 1)) // SEL_BLOCK
    valid = j_id <= cur
    forced = (j_id == 0) | (valid & (j_id > cur - N_LOCAL_SEL))
    score_ref[...] = jnp.where(forced, FORCE_SCORE, jnp.where(valid, imp_t, -1.0))
    cnt_ref[...] = jnp.zeros(cnt_ref.shape, F32)
    n_grp = LANES // SUBLANES
    last_src_grp = ((t0 + Q_TILE - 1) // SEL_BLOCK) // SUBLANES
    sub = lax.broadcasted_iota(jnp.int32, (SUBLANES, Q_TILE), 0)
    for gj in range(n_grp):
        @pl.when(gj <= last_src_grp)
        def _():
            src = score_ref[gj * SUBLANES:(gj + 1) * SUBLANES, :]
            rows_b = [jnp.broadcast_to(src[r:r + 1, :], (SUBLANES, Q_TILE)) for r in range(SUBLANES)]
            for gi in range(n_grp):
                tgt = score_ref[gi * SUBLANES:(gi + 1) * SUBLANES, :]
                acc = cnt_ref[gi * SUBLANES:(gi + 1) * SUBLANES, :]
                for r in range(SUBLANES):
                    if gi < gj:
                        inc = jnp.where(rows_b[r] > tgt, 1.0, 0.0)
                    elif gi > gj:
                        inc = jnp.where(rows_b[r] >= tgt, 1.0, 0.0)
                    else:
                        inc = jnp.where(sub > r, jnp.where(rows_b[r] >= tgt, 1.0, 0.0),
                                        jnp.where(rows_b[r] > tgt, 1.0, 0.0))
                    acc = acc + inc
                cnt_ref[gi * SUBLANES:(gi + 1) * SUBLANES, :] = acc
    bias_t = jnp.where(cnt_ref[...] < float(SEL_TOP), 0.0, NEG_INF)
    sel_ref[...] = bias_t.T.astype(sel_ref.dtype)

    n_keys = N_WIN_BLK * Q_TILE
    tq_w = t0 + lax.broadcasted_iota(jnp.int32, (Q_TILE, n_keys), 0)
    pos = t0 - WINDOW + lax.broadcasted_iota(jnp.int32, (Q_TILE, n_keys), 1)
    bias_w = jnp.where((pos <= tq_w) & (pos > tq_w - WINDOW) & (pos >= 0), 0.0, NEG_INF)
    s_w = jnp.concatenate([_dot_nt(q4, kw_refs[i][...]) for i in range(N_WIN_BLK)], axis=1)
    pw_heads = []
    for r in head_rows:
        p, l = _biased_softmax2(s_w[r], bias_w)
        pw_heads.append((p * (1.0 / jnp.maximum(l, 1e-30))).astype(BF16))
    p_w = jnp.concatenate(pw_heads, axis=0)
    o_w = jnp.dot(p_w[:, 0:Q_TILE], vw_refs[0][...], preferred_element_type=F32)
    for i in range(1, N_WIN_BLK):
        o_w = o_w + jnp.dot(p_w[:, i * Q_TILE:(i + 1) * Q_TILE], vw_refs[i][...],
                            preferred_element_type=F32)

    gates = g_ref[...]
    for z, r in enumerate(head_rows):
        g_c = gates[:, z * N_NSA_BRANCHES:z * N_NSA_BRANCHES + 1]
        g_w = gates[:, z * N_NSA_BRANCHES + 2:z * N_NSA_BRANCHES + 3]
        ocw_ref[:, z * HEAD_DIM:(z + 1) * HEAD_DIM] = g_c * o_c[r] + g_w * o_w[r]


def _cw_attention(q, k_cmp, v_cmp, ovt, kw, vw, gates):
    s = q.shape[0]
    n_chunks = k_cmp.shape[1]
    n_qb = s // Q_TILE
    grp_w = Q_PER_KV * HEAD_DIM

    def win_spec(i):
        return pl.BlockSpec((Q_TILE, HEAD_DIM),
                            lambda g, qb: (jnp.maximum(qb - (N_WIN_BLK - 1) + i, 0), g))

    cmp_spec = pl.BlockSpec((None, n_chunks, HEAD_DIM), lambda g, qb: (g, 0, 0))
    in_specs = ([pl.BlockSpec((Q_TILE, grp_w), lambda g, qb: (qb, g)), cmp_spec, cmp_spec,
                 pl.BlockSpec((LANES, n_chunks), lambda g, qb: (0, 0))]
                + [win_spec(i) for i in range(N_WIN_BLK)] * 2
                + [pl.BlockSpec((Q_TILE, LANES), lambda g, qb: (qb, g))])
    return pl.pallas_call(
        functools.partial(_cw_attn_kernel, n_chunks=n_chunks),
        out_shape=[jax.ShapeDtypeStruct((s, ATTN_WIDTH), F32),
                   jax.ShapeDtypeStruct((N_KV_GROUPS, s, LANES), BF16)],
        grid=(N_KV_GROUPS, n_qb),
        in_specs=in_specs,
        out_specs=[pl.BlockSpec((Q_TILE, grp_w), lambda g, qb: (qb, g)),
                   pl.BlockSpec((None, Q_TILE, LANES), lambda g, qb: (g, qb, 0))],
        scratch_shapes=[pltpu.VMEM((LANES, Q_TILE), F32), pltpu.VMEM((LANES, Q_TILE), F32)],
        compiler_params=_params("parallel", "parallel"),
        name="cmp_win_attention",
    )(q, k_cmp, v_cmp, ovt, *([kw] * N_WIN_BLK), *([vw] * N_WIN_BLK), gates)


def _sel_attn_kernel(qb_ref, kt_ref, q_ref, sb_ref, k_ref, vprev_ref, vlast_ref, ocw_ref, g_ref, o_ref,
                     qx_ref, m_ref, l_ref, acc_ref, p0_ref, p1_ref, a0_ref, a1_ref):
    step = pl.program_id(1)
    qb = qb_ref[step]
    kt = kt_ref[step]
    last_kt = (qb * SEL_Q_TILE) // K_TILE
    n_lane_blk = K_TILE // LANES
    head_rows = [slice(z * SEL_Q_TILE, (z + 1) * SEL_Q_TILE) for z in range(Q_PER_KV)]
    p_bufs = (p0_ref, p1_ref)
    a_bufs = (a0_ref, a1_ref)

    @pl.when(kt == 0)
    def _():
        for z, r in enumerate(head_rows):
            qx_ref[r, 0:HEAD_DIM] = q_ref[:, z * HEAD_DIM:(z + 1) * HEAD_DIM]
            qx_ref[r, HEAD_DIM:2 * HEAD_DIM] = sb_ref[...]
        m_ref[...] = jnp.full(m_ref.shape, NEG_INF, F32)
        l_ref[...] = jnp.zeros(l_ref.shape, F32)
        acc_ref[...] = jnp.zeros(acc_ref.shape, F32)
        p1_ref[...] = jnp.zeros(p1_ref.shape, BF16)
        a1_ref[...] = jnp.ones(a1_ref.shape, F32)

    def apply_pv(slot, v_ref):
        v = v_ref[...]
        for r in head_rows:
            acc_ref[r, :] = a_bufs[slot][r, :] * acc_ref[r, :] + jnp.dot(
                p_bufs[slot][r, :], v, preferred_element_type=F32)

    def update(causal, slot):
        k = k_ref[...]
        scores = [_dot_nt(qx_ref[r, :], k) for r in head_rows]
        apply_pv(1 - slot, vprev_ref)
        if causal:
            tq = qb * SEL_Q_TILE + lax.broadcasted_iota(jnp.int32, (SEL_Q_TILE, K_TILE), 0)
            kpos = kt * K_TILE + lax.broadcasted_iota(jnp.int32, (SEL_Q_TILE, K_TILE), 1)
            cbias = jnp.where(kpos <= tq, 0.0, NEG_INF)
        for r, s in zip(head_rows, scores):
            if causal:
                s = s + cbias
            blk = [s[:, c * LANES:(c + 1) * LANES] for c in range(n_lane_blk)]
            mx = blk[0]
            for c in range(1, n_lane_blk):
                mx = jnp.maximum(mx, blk[c])
            m_old = m_ref[r, :]
            m_new = jnp.maximum(m_old, jnp.max(mx, axis=-1, keepdims=True))
            alpha = jnp.exp2(m_old - m_new)
            ps = [jnp.exp2(b - m_new) for b in blk]
            l_add = ps[0]
            for c in range(1, n_lane_blk):
                l_add = l_add + ps[c]
            l_ref[r, :] = alpha * l_ref[r, :] + l_add
            m_ref[r, :] = m_new
            a_bufs[slot][r, :] = alpha
            p_bufs[slot][r, :] = jnp.concatenate([x.astype(BF16) for x in ps], axis=1)

    def finish(slot):
        apply_pv(slot, vlast_ref)
        gates = g_ref[...]
        for z, r in enumerate(head_rows):
            cols = slice(z * HEAD_DIM, (z + 1) * HEAD_DIM)
            g_s = gates[:, z * N_NSA_BRANCHES + 1:z * N_NSA_BRANCHES + 2]
            l_row = jnp.sum(l_ref[r, :], axis=-1, keepdims=True)
            o_s = acc_ref[r, :] / jnp.maximum(l_row, 1e-30)
            o_ref[:, cols] = (ocw_ref[:, cols] + g_s * o_s).astype(o_ref.dtype)

    for slot in range(2):
        @pl.when((kt < last_kt) & (kt % 2 == slot))
        def _():
            update(False, slot)

        @pl.when((kt == last_kt) & (kt % 2 == slot))
        def _():
            update(True, slot)
            finish(slot)


def _sel_attention(q, sel_bias, ks_ext, vs, ocw, gates):
    s = q.shape[0]
    n_qb = s // SEL_Q_TILE
    grp_w = Q_PER_KV * HEAD_DIM
    qb_of, kt_of = [], []
    for qb in range(n_qb):
        for kt in range((qb * SEL_Q_TILE) // K_TILE + 1):
            qb_of.append(qb)
            kt_of.append(kt)
    qb_arr = jnp.asarray(np.asarray(qb_of, np.int32))
    kt_arr = jnp.asarray(np.asarray(kt_of, np.int32))
    grid_spec = pltpu.PrefetchScalarGridSpec(
        num_scalar_prefetch=2,
        grid=(N_KV_GROUPS, len(qb_of)),
        in_specs=[pl.BlockSpec((SEL_Q_TILE, grp_w), lambda g, i, qbr, ktr: (qbr[i], g)),
                  pl.BlockSpec((None, SEL_Q_TILE, LANES), lambda g, i, qbr, ktr: (g, qbr[i], 0)),
                  pl.BlockSpec((K_TILE, 2 * HEAD_DIM), lambda g, i, qbr, ktr: (ktr[i], g)),
                  pl.BlockSpec((K_TILE, HEAD_DIM), lambda g, i, qbr, ktr: (jnp.maximum(ktr[i] - 1, 0), g)),
                  pl.BlockSpec((K_TILE, HEAD_DIM),
                               lambda g, i, qbr, ktr: ((qbr[i] * SEL_Q_TILE) // K_TILE, g)),
                  pl.BlockSpec((SEL_Q_TILE, grp_w), lambda g, i, qbr, ktr: (qbr[i], g)),
                  pl.BlockSpec((SEL_Q_TILE, LANES), lambda g, i, qbr, ktr: (qbr[i], g))],
        out_specs=pl.BlockSpec((SEL_Q_TILE, grp_w), lambda g, i, qbr, ktr: (qbr[i], g)),
        scratch_shapes=[pltpu.VMEM((SEL_ROWS, 2 * HEAD_DIM), BF16),
                        pltpu.VMEM((SEL_ROWS, LANES), F32),
                        pltpu.VMEM((SEL_ROWS, LANES), F32),
                        pltpu.VMEM((SEL_ROWS, HEAD_DIM), F32),
                        pltpu.VMEM((SEL_ROWS, K_TILE), BF16),
                        pltpu.VMEM((SEL_ROWS, K_TILE), BF16),
                        pltpu.VMEM((SEL_ROWS, LANES), F32),
                        pltpu.VMEM((SEL_ROWS, LANES), F32)],
    )
    return pl.pallas_call(
        _sel_attn_kernel,
        out_shape=jax.ShapeDtypeStruct((s, ATTN_WIDTH), BF16),
        grid_spec=grid_spec,
        compiler_params=_params("parallel", "arbitrary"),
        name="sel_attention",
    )(qb_arr, kt_arr, q, sel_bias, ks_ext, vs, vs, ocw, gates)


def _rnn_kernel(rx_ref, ry_ref, cw_ref, cb_ref, wa_ref, ba_ref, wx_ref, bx_ref, lam_ref, o_ref,
                ext_ref, a_ref, b_ref, h_ref, *, tt, tc):
    ti = pl.program_id(1)
    n_blk = tc // RNN_BLOCK_DIM
    halo = SUBLANES

    @pl.when(ti == 0)
    def _():
        ext_ref[0:halo, :] = jnp.zeros((halo, tc), F32)
        h_ref[...] = jnp.zeros(h_ref.shape, F32)

    @pl.when(ti > 0)
    def _():
        ext_ref[0:halo, :] = ext_ref[tt:tt + halo, :]

    ext_ref[halo:halo + tt, :] = rx_ref[...]
    xr = cb_ref[...] + ext_ref[pl.ds(halo - (RNN_CONV - 1), tt), :] * cw_ref[0:1, :]
    for j in range(1, RNN_CONV):
        xr = xr + ext_ref[pl.ds(halo - (RNN_CONV - 1) + j, tt), :] * cw_ref[j:j + 1, :]

    sp = jnp.maximum(-lam_ref[...], 0.0) + jnp.log(1.0 + jnp.exp(-jnp.abs(lam_ref[...])))
    xb = xr.astype(BF16)
    for blk in range(n_blk):
        cols = slice(blk * RNN_BLOCK_DIM, (blk + 1) * RNN_BLOCK_DIM)
        xs = xb[:, cols]
        r = jax.nn.sigmoid(jnp.dot(xs, wa_ref[blk], preferred_element_type=F32) + ba_ref[:, cols])
        i = jax.nn.sigmoid(jnp.dot(xs, wx_ref[blk], preferred_element_type=F32) + bx_ref[:, cols])
        log_a = -RG_C * r * sp[:, cols]
        a_ref[:, cols] = jnp.exp(log_a)
        b_ref[:, cols] = jnp.sqrt(1.0 - jnp.exp(2.0 * log_a)) * (i * xr[:, cols])

    row = lax.broadcasted_iota(jnp.int32, (SUBLANES, tc), 0)

    def scan_rows(i, carry):
        r0 = pl.multiple_of(i * SUBLANES, SUBLANES)
        a8 = a_ref[pl.ds(r0, SUBLANES), :]
        b8 = b_ref[pl.ds(r0, SUBLANES), :]
        for d in (1, 2, 4):
            keep = row >= d
            a_sh = pltpu.roll(a8, d, axis=0)
            b_sh = pltpu.roll(b8, d, axis=0)
            b8 = jnp.where(keep, a8 * b_sh + b8, b8)
            a8 = jnp.where(keep, a8 * a_sh, a8)
        h8 = a8 * carry + b8
        b_ref[pl.ds(r0, SUBLANES), :] = h8
        return jnp.broadcast_to(h8[SUBLANES - 1:SUBLANES, :], (SUBLANES, tc))

    h_ref[...] = lax.fori_loop(0, tt // SUBLANES, scan_rows, h_ref[...])
    o_ref[...] = (b_ref[...] * jax.nn.gelu(ry_ref[...], approximate=True)).astype(o_ref.dtype)


def _rnn_branch(proj, cw, cb, wa, ba, wx, bx, lam, *, tt, tc):
    s = proj.shape[0]
    n_cb = RNN_WIDTH // tc
    blk_per = tc // RNN_BLOCK_DIM
    rx_blk = COL_RX // tc
    ry_blk = COL_RY // tc
    vec = pl.BlockSpec((1, tc), lambda c, t: (0, c))
    wspec = pl.BlockSpec((blk_per, RNN_BLOCK_DIM, RNN_BLOCK_DIM), lambda c, t: (c, 0, 0))
    return pl.pallas_call(
        functools.partial(_rnn_kernel, tt=tt, tc=tc),
        out_shape=jax.ShapeDtypeStruct((s, RNN_WIDTH), BF16),
        grid=(n_cb, s // tt),
        in_specs=[pl.BlockSpec((tt, tc), lambda c, t: (t, rx_blk + c)),
                  pl.BlockSpec((tt, tc), lambda c, t: (t, ry_blk + c)),
                  pl.BlockSpec((RNN_CONV, tc), lambda c, t: (0, c)),
                  vec, wspec, vec, wspec, vec, vec],
        out_specs=pl.BlockSpec((tt, tc), lambda c, t: (t, c)),
        scratch_shapes=[pltpu.VMEM((tt + SUBLANES, tc), F32),
                        pltpu.VMEM((tt, tc), F32),
                        pltpu.VMEM((tt, tc), F32),
                        pltpu.VMEM((SUBLANES, tc), F32)],
        compiler_params=_params("parallel", "arbitrary"),
        name="rg_lru",
    )(proj, proj, cw, cb, wa, ba, wx, bx, lam)


def _overlap_t(s):
    n_chunks = s // CMP_STRIDE
    n_cmp = n_chunks - 1
    n_sel = s // SEL_BLOCK
    cmp_start = np.arange(n_cmp) * CMP_STRIDE
    sel_start = np.arange(n_sel) * SEL_BLOCK
    ov = np.clip(np.minimum(cmp_start[:, None] + CMP_LEN, sel_start[None, :] + SEL_BLOCK)
                 - np.maximum(cmp_start[:, None], sel_start[None, :]), 0, None) / CMP_LEN
    out = np.zeros((LANES, n_chunks), np.float32)
    out[:n_sel, :n_cmp] = ov.T
    return out


def _block_onehot(s):
    return (np.arange(s)[:, None] // SEL_BLOCK == np.arange(LANES)[None, :]).astype(np.float32)


def _split_w_in_kernel(w_ref, attn_ref, gate_ref, rest_ref):
    attn_ref[...] = w_ref[:, :GATE_SRC].astype(BF16)
    rest_ref[...] = w_ref[:, REST_SRC:].astype(BF16)
    blk = w_ref[:, GATE_SRC:GATE_SRC + LANES]
    lane = lax.broadcasted_iota(jnp.int32, blk.shape, 1)
    per = Q_PER_KV * N_NSA_BRANCHES
    gates = jnp.zeros(blk.shape, F32)
    for g in range(N_KV_GROUPS):
        moved = blk if g == 0 else pltpu.roll(blk, g * (GATE_LANES - per), axis=1)
        gates = jnp.where((lane >= g * GATE_LANES) & (lane < g * GATE_LANES + per), moved, gates)
    gate_ref[...] = gates.astype(BF16)


def _split_w_in(w_in, *, tr):
    depth, d, n_in = w_in.shape
    n_rest = n_in - REST_SRC
    return pl.pallas_call(
        _split_w_in_kernel,
        out_shape=[jax.ShapeDtypeStruct((depth, d, GATE_SRC), BF16),
                   jax.ShapeDtypeStruct((depth, d, LANES), BF16),
                   jax.ShapeDtypeStruct((depth, d, n_rest), BF16)],
        grid=(depth, d // tr),
        in_specs=[pl.BlockSpec((None, tr, n_in), lambda l, i: (l, i, 0))],
        out_specs=[pl.BlockSpec((None, tr, GATE_SRC), lambda l, i: (l, i, 0)),
                   pl.BlockSpec((None, tr, LANES), lambda l, i: (l, i, 0)),
                   pl.BlockSpec((None, tr, n_rest), lambda l, i: (l, i, 0))],
        compiler_params=_params("parallel", "parallel"),
        name="split_w_in",
    )(w_in)


def kernel(x, c, positions, w_cond, b_cond, w_mod, b_mod, norm_mix, norm_ffn, w_in, q_norm, k_norm, cmp_pe_k, cmp_w_k, cmp_pe_v, cmp_w_v, rnn_conv_w, rnn_conv_b, rg_w_a, rg_b_a, rg_w_x, rg_b_x, rg_lambda, w_attn_up, w_rnn_up, w_out, w_ffn_in, ffn_conv_w, ffn_conv_b, w_ffn_down):
    b, s, d = x.shape
    depth = w_in.shape[0]
    assert b == 1 and d == D_MODEL and s % 2048 == 0 and s // SEL_BLOCK <= LANES

    c_emb = _vecmat(c.reshape(1, d, 1), w_cond[None], b_cond.reshape(1, 1, -1), silu=True, tn=w_cond.shape[1])
    c_col = jnp.broadcast_to(c_emb.reshape(1, -1, 1), (depth, c_emb.shape[-1], 1))
    mod = _vecmat(c_col, w_mod, b_mod[:, None, :], silu=False, tn=2048)

    inv_freq = ROPE_THETA ** (-jnp.arange(0, ROT_DIM, 2, dtype=jnp.float32) / ROT_DIM)
    freq_row = jnp.concatenate([inv_freq, inv_freq, jnp.zeros((LANES - ROT_DIM,), F32)])[None, :]
    rope_c, rope_sa, rope_sb = _rope_tables(positions.reshape(s, 1), freq_row, tr=512)

    ovt = jnp.asarray(_overlap_t(s), BF16)
    blk_onehot = jnp.asarray(_block_onehot(s), BF16)

    w_attn_in, w_gate, w_rest_in = _split_w_in(w_in, tr=64)
    w_attn_up_b = w_attn_up.astype(BF16)
    w_rnn_up_b = w_rnn_up.astype(BF16)
    w_out_b = w_out.astype(BF16)
    w_ffn_in_b = w_ffn_in.astype(BF16)
    w_ffn_down_b = w_ffn_down.astype(BF16)

    h = x.reshape(s, d)
    for l in range(depth):
        sh1, sc1, g1, sh2, sc2, g2 = [mod[l, :, i * d:(i + 1) * d] for i in range(N_MOD)]
        proj_a, graw, u = _norm_matmul(h, norm_mix[l][None], sc1, sh1, w_attn_in, w_gate,
                                       layer=l, tm=512, tn=1024, out_dtype=F32)
        proj = _matmul(u, w_rest_in, layer=l, tm=512, tn=1024, out_dtype=F32)
        q, kc, ks_ext, vs, kw, vw, gates = _prep(proj_a, graw, rope_c, rope_sa, rope_sb,
                                                 q_norm[l][None], k_norm[l], blk_onehot, tr=256)
        k_cmp = _compress(kc, 0, cmp_pe_k[l], cmp_w_k[l].astype(BF16), k_norm[l][0:1], do_norm=True)
        v_cmp = _compress(proj_a, COL_VC // HEAD_DIM, cmp_pe_v[l], cmp_w_v[l].astype(BF16),
                          k_norm[l][0:1], do_norm=False)
        ocw, sel_bias = _cw_attention(q, k_cmp, v_cmp, ovt, kw, vw, gates)
        attn = _sel_attention(q, sel_bias, ks_ext, vs, ocw, gates)
        rnn = _rnn_branch(proj, rnn_conv_w[l], rnn_conv_b[l][None], rg_w_a[l].astype(BF16), rg_b_a[l][None],
                          rg_w_x[l].astype(BF16), rg_b_x[l][None], rg_lambda[l][None], tt=512, tc=1024)
        merged = _merge(attn, rnn, w_attn_up_b, w_rnn_up_b, proj, layer=l, tm=512, tn=1024)
        h = _matmul_residual(merged, w_out_b, h, g1, layer=l, tm=512, tn=1024)
        act = _ffn_in(h, norm_ffn[l][None], sc2, sh2, w_ffn_in_b, ffn_conv_w[l], ffn_conv_b[l][None],
                      layer=l, tm=512, tn=512)
        h = _matmul_residual(act, w_ffn_down_b, h, g2, layer=l, tm=512, tn=512)
    return h.reshape(b, s, d)
```

```python
import functools

import numpy as np
import jax
import jax.numpy as jnp
from jax import lax
from jax.experimental import pallas as pl
from jax.experimental.pallas import tpu as pltpu

F32 = jnp.float32
BF16 = jnp.bfloat16

D_MODEL = 4096
N_Q_HEADS = 16
N_KV_GROUPS = 4
HEAD_DIM = 128
Q_PER_KV = N_Q_HEADS // N_KV_GROUPS
ATTN_WIDTH = N_Q_HEADS * HEAD_DIM
KV_WIDTH = N_KV_GROUPS * HEAD_DIM
N_NSA_BRANCHES = 3
ROT_DIM = HEAD_DIM // 4
ROPE_THETA = 500000.0
CMP_LEN = 32
CMP_STRIDE = 16
SEL_BLOCK = 64
SEL_TOP = 16
N_LOCAL_SEL = 2
WINDOW = 512
FORCE_SCORE = 2.0 * Q_PER_KV + 1.0
RNN_WIDTH = 2048
RNN_BLOCKS = 16
RNN_BLOCK_DIM = RNN_WIDTH // RNN_BLOCKS
RNN_CONV = 4
RG_C = 8.0
D_FF = 2 * D_MODEL
FFN_CONV = 3
N_MOD = 6
EPS = 1e-6
NEG_INF = -1e30
ATTN_SCALE = HEAD_DIM ** -0.5
LOG2_E = 1.4426950408889634
Q_SCALE = ATTN_SCALE * LOG2_E

LANES = 128
SUBLANES = 8
VMEM_LIMIT_BYTES = 56 * 1024 * 1024

COL_Q = 0
COL_KC = ATTN_WIDTH
COL_VC = COL_KC + KV_WIDTH
COL_KS = COL_VC + KV_WIDTH
COL_VS = COL_KS + KV_WIDTH
COL_KW = COL_VS + KV_WIDTH
COL_VW = COL_KW + KV_WIDTH
N_PROJ_ATTN = COL_VW + KV_WIDTH
GATE_SRC = N_PROJ_ATTN
PROJ_SHIFT = N_Q_HEADS * N_NSA_BRANCHES
N_PROJ_REST = 2 * RNN_WIDTH + 2 * D_MODEL
COL_RX = 0
COL_RY = COL_RX + RNN_WIDTH
COL_GA = COL_RY + RNN_WIDTH
COL_GR = COL_GA + D_MODEL

Q_TILE = 128
K_TILE = 1024
SEL_Q_TILE = 256
SEL_ROWS = Q_PER_KV * SEL_Q_TILE
N_WIN_BLK = WINDOW // Q_TILE + 1
NORM_CHUNK = 64


def _unshift(main, spill):
    return jnp.concatenate([main, spill], axis=1)[:, PROJ_SHIFT:PROJ_SHIFT + main.shape[1]]


def _params(*sem):
    return pltpu.CompilerParams(dimension_semantics=sem, vmem_limit_bytes=VMEM_LIMIT_BYTES)


def _vecmat_kernel(x_ref, w_ref, b_ref, o_ref, *, silu):
    y = jnp.sum(w_ref[...] * x_ref[...], axis=0, keepdims=True) + b_ref[...]
    if silu:
        y = y * jax.nn.sigmoid(y)
    o_ref[...] = y


def _vecmat(x_col, w, b, *, silu, tn):
    n_l, k, n = w.shape
    return pl.pallas_call(
        functools.partial(_vecmat_kernel, silu=silu),
        out_shape=jax.ShapeDtypeStruct((n_l, 1, n), F32),
        grid=(n_l, n // tn),
        in_specs=[pl.BlockSpec((None, k, 1), lambda l, j: (l, 0, 0)),
                  pl.BlockSpec((None, k, tn), lambda l, j: (l, 0, j)),
                  pl.BlockSpec((None, 1, tn), lambda l, j: (l, 0, j))],
        out_specs=pl.BlockSpec((None, 1, tn), lambda l, j: (l, 0, j)),
        compiler_params=_params("parallel", "parallel"),
        name="vecmat",
    )(x_col, w, b)


def _rope_table_kernel(pos_ref, freq_ref, c_ref, sa_ref, sb_ref):
    ang = pos_ref[...].astype(F32) * freq_ref[...]
    lane = lax.broadcasted_iota(jnp.int32, ang.shape, 1)
    cos = jnp.cos(ang)
    sin = jnp.sin(ang)
    c_ref[...] = jnp.where(lane < ROT_DIM, cos, 1.0)
    sa_ref[...] = jnp.where(lane < ROT_DIM // 2, -sin, 0.0)
    sb_ref[...] = jnp.where((lane >= ROT_DIM // 2) & (lane < ROT_DIM), sin, 0.0)


def _rope_tables(pos_col, freq_row, tr):
    s = pos_col.shape[0]
    spec = pl.BlockSpec((tr, LANES), lambda i: (i, 0))
    return pl.pallas_call(
        _rope_table_kernel,
        out_shape=[jax.ShapeDtypeStruct((s, LANES), F32)] * 3,
        grid=(s // tr,),
        in_specs=[pl.BlockSpec((tr, 1), lambda i: (i, 0)),
                  pl.BlockSpec((1, LANES), lambda i: (0, 0))],
        out_specs=[spec, spec, spec],
        compiler_params=_params("parallel"),
        name="rope_tables",
    )(pos_col, freq_row)


def _ada_norm(h_ref, gain_ref, sc_ref, sh_ref, u_ref):
    for r0 in range(0, h_ref.shape[0], NORM_CHUNK):
        x = h_ref[r0:r0 + NORM_CHUNK, :]
        y = x * lax.rsqrt(jnp.mean(x * x, axis=-1, keepdims=True) + EPS)
        u = (y * gain_ref[...]) * (1.0 + sc_ref[...]) + sh_ref[...]
        u_ref[r0:r0 + NORM_CHUNK, :] = u.astype(BF16)


def _norm_mm_kernel(h_ref, gain_ref, sc_ref, sh_ref, w_ref, wg_ref, o_ref, og_ref, u_ref):
    @pl.when(pl.program_id(1) == 0)
    def _():
        _ada_norm(h_ref, gain_ref, sc_ref, sh_ref, u_ref)
        og_ref[...] = jnp.dot(u_ref[...], wg_ref[...], preferred_element_type=F32)

    o_ref[...] = jnp.dot(u_ref[...], w_ref[...], preferred_element_type=F32).astype(o_ref.dtype)


def _norm_matmul(h, gain, sc, sh, w, wg, *, layer, tm, tn, out_dtype):
    s, d = h.shape
    n = w.shape[2]
    ng = wg.shape[2]
    row = pl.BlockSpec((1, d), lambda i, j: (0, 0))
    return pl.pallas_call(
        _norm_mm_kernel,
        out_shape=[jax.ShapeDtypeStruct((s, n), out_dtype), jax.ShapeDtypeStruct((s, ng), F32),
                   jax.ShapeDtypeStruct((s, d), BF16)],
        grid=(s // tm, n // tn),
        in_specs=[pl.BlockSpec((tm, d), lambda i, j: (i, 0)), row, row, row,
                  pl.BlockSpec((None, d, tn), lambda i, j: (layer, 0, j)),
                  pl.BlockSpec((None, d, ng), lambda i, j: (layer, 0, 0))],
        out_specs=[pl.BlockSpec((tm, tn), lambda i, j: (i, j)),
                   pl.BlockSpec((tm, ng), lambda i, j: (i, 0)),
                   pl.BlockSpec((tm, d), lambda i, j: (i, 0))],
        compiler_params=_params("parallel", "arbitrary"),
        name="norm_matmul",
    )(h, gain, sc, sh, w, wg)


def _mm_kernel(a_ref, w_ref, o_ref):
    o_ref[...] = jnp.dot(a_ref[...], w_ref[...], preferred_element_type=F32).astype(o_ref.dtype)


def _matmul(a, w, *, layer, tm, tn, out_dtype):
    s, k = a.shape
    n = w.shape[2]
    return pl.pallas_call(
        _mm_kernel,
        out_shape=jax.ShapeDtypeStruct((s, n), out_dtype),
        grid=(s // tm, n // tn),
        in_specs=[pl.BlockSpec((tm, k), lambda i, j: (i, 0)),
                  pl.BlockSpec((None, k, tn), lambda i, j: (layer, 0, j))],
        out_specs=pl.BlockSpec((tm, tn), lambda i, j: (i, j)),
        compiler_params=_params("parallel", "parallel"),
        name="matmul",
    )(a, w)


def _ffn_in_kernel(h_ref, gain_ref, sc_ref, sh_ref, wg_ref, wu_ref, cw_ref, cb_ref, o_ref,
                   u_ref, ext_ref, carry_ref, *, tm):
    i = pl.program_id(0)
    j = pl.program_id(1)
    halo = SUBLANES

    @pl.when(j == 0)
    def _():
        _ada_norm(h_ref, gain_ref, sc_ref, sh_ref, u_ref)

    @pl.when(i == 0)
    def _():
        ext_ref[0:halo, :] = jnp.zeros((halo, ext_ref.shape[1]), F32)

    @pl.when(i > 0)
    def _():
        ext_ref[0:halo, :] = carry_ref[j]

    ext_ref[halo:halo + tm, :] = jnp.dot(u_ref[...], wg_ref[...], preferred_element_type=F32)
    carry_ref[j] = ext_ref[tm:tm + halo, :]
    y = cb_ref[...] + ext_ref[pl.ds(halo - (FFN_CONV - 1), tm), :] * cw_ref[0:1, :]
    for t in range(1, FFN_CONV):
        y = y + ext_ref[pl.ds(halo - (FFN_CONV - 1) + t, tm), :] * cw_ref[t:t + 1, :]
    up = jnp.dot(u_ref[...], wu_ref[...], preferred_element_type=F32)
    o_ref[...] = (y * jax.nn.sigmoid(y) * up).astype(o_ref.dtype)


def _ffn_in(h, gain, sc, sh, w, cw, cb, *, layer, tm, tn):
    s, d = h.shape
    n_j = D_FF // tn
    row = pl.BlockSpec((1, d), lambda i, j: (0, 0))
    return pl.pallas_call(
        functools.partial(_ffn_in_kernel, tm=tm),
        out_shape=jax.ShapeDtypeStruct((s, D_FF), BF16),
        grid=(s // tm, n_j),
        in_specs=[pl.BlockSpec((tm, d), lambda i, j: (i, 0)), row, row, row,
                  pl.BlockSpec((None, d, tn), lambda i, j: (layer, 0, j)),
                  pl.BlockSpec((None, d, tn), lambda i, j: (layer, 0, n_j + j)),
                  pl.BlockSpec((FFN_CONV, tn), lambda i, j: (0, j)),
                  pl.BlockSpec((1, tn), lambda i, j: (0, j))],
        out_specs=pl.BlockSpec((tm, tn), lambda i, j: (i, j)),
        scratch_shapes=[pltpu.VMEM((tm, d), BF16),
                        pltpu.VMEM((tm + SUBLANES, tn), F32),
                        pltpu.VMEM((n_j, SUBLANES, tn), F32)],
        compiler_params=_params("arbitrary", "arbitrary"),
        name="ffn_in",
    )(h, gain, sc, sh, w, w, cw, cb)


def _mm_res_kernel(a_ref, w_ref, h_ref, g_ref, o_ref):
    y = jnp.dot(a_ref[...], w_ref[...], preferred_element_type=F32)
    o_ref[...] = h_ref[...] + g_ref[...] * y


def _matmul_residual(a, w, h, g, *, layer, tm, tn):
    s, k = a.shape
    n = w.shape[2]
    return pl.pallas_call(
        _mm_res_kernel,
        out_shape=jax.ShapeDtypeStruct((s, n), F32),
        grid=(s // tm, n // tn),
        in_specs=[pl.BlockSpec((tm, k), lambda i, j: (i, 0)),
                  pl.BlockSpec((None, k, tn), lambda i, j: (layer, 0, j)),
                  pl.BlockSpec((tm, tn), lambda i, j: (i, j)),
                  pl.BlockSpec((1, tn), lambda i, j: (0, j))],
        out_specs=pl.BlockSpec((tm, tn), lambda i, j: (i, j)),
        compiler_params=_params("parallel", "parallel"),
        name="matmul_residual",
    )(a, w, h, g)


def _merge_kernel(attn_ref, rnn_ref, wa_ref, wr_ref, ga_ref, gas_ref, gr_ref, grs_ref, tail_ref, o_ref):
    ya = jnp.dot(attn_ref[...], wa_ref[...], preferred_element_type=F32)
    yr = jnp.dot(rnn_ref[...], wr_ref[...], preferred_element_type=F32)
    g_attn = _unshift(ga_ref[...], gas_ref[...])
    is_last = pl.program_id(1) == pl.num_programs(1) - 1
    g_rnn = _unshift(gr_ref[...], jnp.where(is_last, tail_ref[...], grs_ref[...]))
    o_ref[...] = (jax.nn.sigmoid(g_attn) * ya + jax.nn.sigmoid(g_rnn) * yr).astype(o_ref.dtype)


def _merge(attn, rnn, wa, wr, proj, tail, *, layer, tm, tn):
    s, k = attn.shape
    n = wa.shape[2]
    ga_blk = COL_GA // tn
    gr_blk = COL_GR // tn
    per = tn // LANES
    last_spill = proj.shape[1] // LANES - 1
    return pl.pallas_call(
        _merge_kernel,
        out_shape=jax.ShapeDtypeStruct((s, n), BF16),
        grid=(s // tm, n // tn),
        in_specs=[pl.BlockSpec((tm, k), lambda i, j: (i, 0)),
                  pl.BlockSpec((tm, k), lambda i, j: (i, 0)),
                  pl.BlockSpec((None, k, tn), lambda i, j: (layer, 0, j)),
                  pl.BlockSpec((None, k, tn), lambda i, j: (layer, 0, j)),
                  pl.BlockSpec((tm, tn), lambda i, j: (i, ga_blk + j)),
                  pl.BlockSpec((tm, LANES), lambda i, j: (i, (ga_blk + j + 1) * per)),
                  pl.BlockSpec((tm, tn), lambda i, j: (i, gr_blk + j)),
                  pl.BlockSpec((tm, LANES), lambda i, j: (i, jnp.minimum((gr_blk + j + 1) * per, last_spill))),
                  pl.BlockSpec((tm, LANES), lambda i, j: (i, 0))],
        out_specs=pl.BlockSpec((tm, tn), lambda i, j: (i, j)),
        compiler_params=_params("parallel", "parallel"),
        name="merge",
    )(attn, rnn, wa, wr, proj, proj, proj, proj, tail)


def _head_norm(x, gain):
    return x * lax.rsqrt(jnp.mean(x * x, axis=-1, keepdims=True) + EPS) * gain


def _rope(x, c, sa, sb):
    return (x * c + pltpu.roll(x, LANES - ROT_DIM // 2, axis=1) * sa
            + pltpu.roll(x, ROT_DIM // 2, axis=1) * sb)


def _prep_kernel(p_ref, graw_ref, c_ref, sa_ref, sb_ref, qn_ref, kn_ref, blk_ref,
                 q_ref, kc_ref, ks_ref, vs_ref, kw_ref, vw_ref, g_ref):
    c, sa, sb = c_ref[...], sa_ref[...], sb_ref[...]
    qn = qn_ref[...]
    for hd in range(N_Q_HEADS):
        cols = slice(hd * HEAD_DIM, (hd + 1) * HEAD_DIM)
        q_ref[:, cols] = (_rope(_head_norm(p_ref[:, cols], qn), c, sa, sb) * Q_SCALE).astype(BF16)
    for g in range(N_KV_GROUPS):
        cols = slice(g * HEAD_DIM, (g + 1) * HEAD_DIM)

        def src(base):
            return p_ref[:, base + g * HEAD_DIM:base + (g + 1) * HEAD_DIM]

        kc_ref[:, cols] = _rope(src(COL_KC), c, sa, sb)
        ks_ref[:, 2 * g * HEAD_DIM:(2 * g + 1) * HEAD_DIM] = _rope(
            _head_norm(src(COL_KS), kn_ref[1:2, :]), c, sa, sb).astype(BF16)
        ks_ref[:, (2 * g + 1) * HEAD_DIM:(2 * g + 2) * HEAD_DIM] = blk_ref[...]
        kw_ref[:, cols] = _rope(_head_norm(src(COL_KW), kn_ref[2:3, :]), c, sa, sb).astype(BF16)
        vs_ref[:, cols] = src(COL_VS).astype(BF16)
        vw_ref[:, cols] = src(COL_VW).astype(BF16)
    sig = jax.nn.sigmoid(graw_ref[...])
    per = Q_PER_KV * N_NSA_BRANCHES
    for g in range(N_KV_GROUPS):
        g_ref[:, g * LANES:(g + 1) * LANES] = sig if g == 0 else pltpu.roll(sig, LANES - g * per, axis=1)


def _prep(proj, graw, c, sa, sb, qn, kn, blk_onehot, *, tr):
    s = proj.shape[0]
    ng = N_KV_GROUPS * LANES
    tab = pl.BlockSpec((tr, LANES), lambda i: (i, 0))
    kv_spec = pl.BlockSpec((tr, KV_WIDTH), lambda i: (i, 0))
    return pl.pallas_call(
        _prep_kernel,
        out_shape=[jax.ShapeDtypeStruct((s, ATTN_WIDTH), BF16),
                   jax.ShapeDtypeStruct((s, KV_WIDTH), F32),
                   jax.ShapeDtypeStruct((s, 2 * KV_WIDTH), BF16),
                   jax.ShapeDtypeStruct((s, KV_WIDTH), BF16),
                   jax.ShapeDtypeStruct((s, KV_WIDTH), BF16),
                   jax.ShapeDtypeStruct((s, KV_WIDTH), BF16),
                   jax.ShapeDtypeStruct((s, ng), F32)],
        grid=(s // tr,),
        in_specs=[pl.BlockSpec((tr, N_PROJ_ATTN), lambda i: (i, 0)),
                  tab, tab, tab, tab,
                  pl.BlockSpec((1, HEAD_DIM), lambda i: (0, 0)),
                  pl.BlockSpec((N_NSA_BRANCHES, HEAD_DIM), lambda i: (0, 0)),
                  tab],
        out_specs=[pl.BlockSpec((tr, ATTN_WIDTH), lambda i: (i, 0)),
                   kv_spec, pl.BlockSpec((tr, 2 * KV_WIDTH), lambda i: (i, 0)),
                   kv_spec, kv_spec, kv_spec,
                   pl.BlockSpec((tr, ng), lambda i: (i, 0))],
        compiler_params=_params("parallel"),
        name="qk_prep",
    )(proj, graw, c, sa, sb, qn, kn, blk_onehot)


def _compress_kernel(x_ref, pe_ref, w_ref, gain_ref, o_ref, shift_ref, *, do_norm, n_chunks):
    acc_a = jnp.zeros((n_chunks, HEAD_DIM), F32)
    acc_b = jnp.zeros((n_chunks, HEAD_DIM), F32)
    for l in range(CMP_STRIDE):
        xl = x_ref[pl.ds(l, n_chunks, stride=CMP_STRIDE), :]
        xa = (xl + pe_ref[l:l + 1, :]).astype(BF16)
        xb = (xl + pe_ref[CMP_STRIDE + l:CMP_STRIDE + l + 1, :]).astype(BF16)
        acc_a = acc_a + jnp.dot(xa, w_ref[l], preferred_element_type=F32)
        acc_b = acc_b + jnp.dot(xb, w_ref[CMP_STRIDE + l], preferred_element_type=F32)
    shift_ref[0:n_chunks, :] = acc_b
    shift_ref[n_chunks:n_chunks + SUBLANES, :] = jnp.zeros((SUBLANES, HEAD_DIM), F32)
    out = acc_a + shift_ref[1:n_chunks + 1, :]
    if do_norm:
        out = _head_norm(out, gain_ref[...])
    o_ref[...] = out.astype(o_ref.dtype)


def _compress(x, col_blk0, pe, w, gain, *, do_norm):
    s = x.shape[0]
    n_chunks = s // CMP_STRIDE
    return pl.pallas_call(
        functools.partial(_compress_kernel, do_norm=do_norm, n_chunks=n_chunks),
        out_shape=jax.ShapeDtypeStruct((N_KV_GROUPS, n_chunks, HEAD_DIM), BF16),
        grid=(N_KV_GROUPS,),
        in_specs=[pl.BlockSpec((s, HEAD_DIM), lambda g: (0, col_blk0 + g)),
                  pl.BlockSpec((CMP_LEN, HEAD_DIM), lambda g: (0, 0)),
                  pl.BlockSpec((CMP_LEN, HEAD_DIM, HEAD_DIM), lambda g: (0, 0, 0)),
                  pl.BlockSpec((1, HEAD_DIM), lambda g: (0, 0))],
        out_specs=pl.BlockSpec((None, n_chunks, HEAD_DIM), lambda g: (g, 0, 0)),
        scratch_shapes=[pltpu.VMEM((n_chunks + SUBLANES, HEAD_DIM), F32)],
        compiler_params=_params("parallel"),
        name="compress",
    )(x, pe, w, gain)


def _dot_nt(a, b):
    return lax.dot_general(a, b, (((1,), (1,)), ((), ())), preferred_element_type=F32)


def _biased_softmax2(s, bias):
    s = s + bias
    p = jnp.exp2(s - jnp.max(s, axis=-1, keepdims=True))
    return p, jnp.sum(p, axis=-1, keepdims=True)


def _stack_heads(q_ref):
    return jnp.concatenate([q_ref[:, z * HEAD_DIM:(z + 1) * HEAD_DIM] for z in range(Q_PER_KV)], axis=0)


def _cw_attn_kernel(*refs, n_chunks):
    q_ref, kc_ref, vc_ref, ovt_ref = refs[:4]
    kw_refs = refs[4:4 + N_WIN_BLK]
    vw_refs = refs[4 + N_WIN_BLK:4 + 2 * N_WIN_BLK]
    g_ref, ocw_ref, sel_ref, score_ref, cnt_ref = refs[4 + 2 * N_WIN_BLK:]
    qb = pl.program_id(1)
    t0 = qb * Q_TILE
    q4 = _stack_heads(q_ref)
    head_rows = [slice(z * Q_TILE, (z + 1) * Q_TILE) for z in range(Q_PER_KV)]

    tq = t0 + lax.broadcasted_iota(jnp.int32, (Q_TILE, n_chunks), 0)
    n_id = lax.broadcasted_iota(jnp.int32, (Q_TILE, n_chunks), 1)
    bias_c = jnp.where((n_id * CMP_STRIDE + (CMP_LEN - 1) <= tq) & (n_id < n_chunks - 1), 0.0, NEG_INF)
    row_ok = jnp.where(t0 + lax.broadcasted_iota(jnp.int32, (Q_TILE, 1), 0) >= CMP_LEN - 1, 1.0, 0.0)
    s_c = _dot_nt(q4, kc_ref[...])
    p_heads = []
    for r in head_rows:
        p, l = _biased_softmax2(s_c[r], bias_c)
        p_heads.append(p * (row_ok / jnp.maximum(l, 1e-30)))
    o_c = jnp.dot(jnp.concatenate(p_heads, axis=0).astype(BF16), vc_ref[...],
                  preferred_element_type=F32)

    p_sum = p_heads[0]
    for z in range(1, Q_PER_KV):
        p_sum = p_sum + p_heads[z]
    p_hi = p_sum.astype(BF16)
    p_lo = (p_sum - p_hi.astype(F32)).astype(BF16)
    ovt = ovt_ref[...]
    imp_t = _dot_nt(ovt, p_hi) + _dot_nt(ovt, p_lo)
    j_id = lax.broadcasted_iota(jnp.int32, (LANES, Q_TILE), 0)
    cur = (t0 + lax.broadcasted_iota(jnp.int32, (LANES, Q_TILE), 1)) // SEL_BLOCK
    valid = j_id <= cur
    forced = (j_id == 0) | (valid & (j_id > cur - N_LOCAL_SEL))
    score_ref[...] = jnp.where(forced, FORCE_SCORE, jnp.where(valid, imp_t, -1.0))
    cnt_ref[...] = jnp.zeros(cnt_ref.shape, F32)
    n_grp = LANES // SUBLANES
    last_src_grp = ((t0 + Q_TILE - 1) // SEL_BLOCK) // SUBLANES
    sub = lax.broadcasted_iota(jnp.int32, (SUBLANES, Q_TILE), 0)
    for gj in range(n_grp):
        @pl.when(gj <= last_src_grp)
        def _():
            src = score_ref[gj * SUBLANES:(gj + 1) * SUBLANES, :]
            rows_b = [jnp.broadcast_to(src[r:r + 1, :], (SUBLANES, Q_TILE)) for r in range(SUBLANES)]
            for gi in range(n_grp):
                tgt = score_ref[gi * SUBLANES:(gi + 1) * SUBLANES, :]
                acc = cnt_ref[gi * SUBLANES:(gi + 1) * SUBLANES, :]
                for r in range(SUBLANES):
                    if gi < gj:
                        inc = jnp.where(rows_b[r] > tgt, 1.0, 0.0)
                    elif gi > gj:
                        inc = jnp.where(rows_b[r] >= tgt, 1.0, 0.0)
                    else:
                        inc = jnp.where(sub > r, jnp.where(rows_b[r] >= tgt, 1.0, 0.0),
                                        jnp.where(rows_b[r] > tgt, 1.0, 0.0))
                    acc = acc + inc
                cnt_ref[gi * SUBLANES:(gi + 1) * SUBLANES, :] = acc
    bias_t = jnp.where(cnt_ref[...] < float(SEL_TOP), 0.0, NEG_INF)
    sel_ref[...] = bias_t.T.astype(sel_ref.dtype)

    n_keys = N_WIN_BLK * Q_TILE
    tq_w = t0 + lax.broadcasted_iota(jnp.int32, (Q_TILE, n_keys), 0)
    pos = t0 - WINDOW + lax.broadcasted_iota(jnp.int32, (Q_TILE, n_keys), 1)
    bias_w = jnp.where((pos <= tq_w) & (pos > tq_w - WINDOW) & (pos >= 0), 0.0, NEG_INF)
    s_w = jnp.concatenate([_dot_nt(q4, kw_refs[i][...]) for i in range(N_WIN_BLK)], axis=1)
    pw_heads = []
    for r in head_rows:
        p, l = _biased_softmax2(s_w[r], bias_w)
        pw_heads.append((p * (1.0 / jnp.maximum(l, 1e-30))).astype(BF16))
    p_w = jnp.concatenate(pw_heads, axis=0)
    o_w = jnp.dot(p_w[:, 0:Q_TILE], vw_refs[0][...], preferred_element_type=F32)
    for i in range(1, N_WIN_BLK):
        o_w = o_w + jnp.dot(p_w[:, i * Q_TILE:(i + 1) * Q_TILE], vw_refs[i][...],
                            preferred_element_type=F32)

    gates = g_ref[...]
    for z, r in enumerate(head_rows):
        g_c = gates[:, z * N_NSA_BRANCHES:z * N_NSA_BRANCHES + 1]
        g_w = gates[:, z * N_NSA_BRANCHES + 2:z * N_NSA_BRANCHES + 3]
        ocw_ref[:, z * HEAD_DIM:(z + 1) * HEAD_DIM] = g_c * o_c[r] + g_w * o_w[r]


def _cw_attention(q, k_cmp, v_cmp, ovt, kw, vw, gates):
    s = q.shape[0]
    n_chunks = k_cmp.shape[1]
    n_qb = s // Q_TILE
    grp_w = Q_PER_KV * HEAD_DIM

    def win_spec(i):
        return pl.BlockSpec((Q_TILE, HEAD_DIM),
                            lambda g, qb: (jnp.maximum(qb - (N_WIN_BLK - 1) + i, 0), g))

    cmp_spec = pl.BlockSpec((None, n_chunks, HEAD_DIM), lambda g, qb: (g, 0, 0))
    in_specs = ([pl.BlockSpec((Q_TILE, grp_w), lambda g, qb: (qb, g)), cmp_spec, cmp_spec,
                 pl.BlockSpec((LANES, n_chunks), lambda g, qb: (0, 0))]
                + [win_spec(i) for i in range(N_WIN_BLK)] * 2
                + [pl.BlockSpec((Q_TILE, LANES), lambda g, qb: (qb, g))])
    return pl.pallas_call(
        functools.partial(_cw_attn_kernel, n_chunks=n_chunks),
        out_shape=[jax.ShapeDtypeStruct((s, ATTN_WIDTH), F32),
                   jax.ShapeDtypeStruct((N_KV_GROUPS, s, LANES), BF16)],
        grid=(N_KV_GROUPS, n_qb),
        in_specs=in_specs,
        out_specs=[pl.BlockSpec((Q_TILE, grp_w), lambda g, qb: (qb, g)),
                   pl.BlockSpec((None, Q_TILE, LANES), lambda g, qb: (g, qb, 0))],
        scratch_shapes=[pltpu.VMEM((LANES, Q_TILE), F32), pltpu.VMEM((LANES, Q_TILE), F32)],
        compiler_params=_params("parallel", "parallel"),
        name="cmp_win_attention",
    )(q, k_cmp, v_cmp, ovt, *([kw] * N_WIN_BLK), *([vw] * N_WIN_BLK), gates)


def _sel_attn_kernel(qb_ref, kt_ref, q_ref, sb_ref, k_ref, vprev_ref, vlast_ref, ocw_ref, g_ref, o_ref,
                     qx_ref, m_ref, l_ref, acc_ref, p_ref, alpha_ref):
    step = pl.program_id(1)
    qb = qb_ref[step]
    kt = kt_ref[step]
    last_kt = (qb * SEL_Q_TILE) // K_TILE
    n_lane_blk = K_TILE // LANES
    slot = kt % 2
    head_rows = [slice(z * SEL_Q_TILE, (z + 1) * SEL_Q_TILE) for z in range(Q_PER_KV)]

    @pl.when(kt == 0)
    def _():
        for z, r in enumerate(head_rows):
            qx_ref[r, 0:HEAD_DIM] = q_ref[:, z * HEAD_DIM:(z + 1) * HEAD_DIM]
            qx_ref[r, HEAD_DIM:2 * HEAD_DIM] = sb_ref[...]
        m_ref[...] = jnp.full(m_ref.shape, NEG_INF, F32)
        l_ref[...] = jnp.zeros(l_ref.shape, F32)
        acc_ref[...] = jnp.zeros(acc_ref.shape, F32)
        p_ref[1] = jnp.zeros(p_ref.shape[1:], BF16)
        alpha_ref[1] = jnp.ones(alpha_ref.shape[1:], F32)

    def apply_pv(src_slot, v_ref):
        v = v_ref[...]
        for r in head_rows:
            acc_ref[r, :] = alpha_ref[src_slot, r, :] * acc_ref[r, :] + jnp.dot(
                p_ref[src_slot, r, :], v, preferred_element_type=F32)

    def update(causal):
        apply_pv(1 - slot, vprev_ref)
        k = k_ref[...]
        if causal:
            tq = qb * SEL_Q_TILE + lax.broadcasted_iota(jnp.int32, (SEL_Q_TILE, K_TILE), 0)
            kpos = kt * K_TILE + lax.broadcasted_iota(jnp.int32, (SEL_Q_TILE, K_TILE), 1)
            cbias = jnp.where(kpos <= tq, 0.0, NEG_INF)
        for r in head_rows:
            s = _dot_nt(qx_ref[r, :], k)
            if causal:
                s = s + cbias
            blk = [s[:, c * LANES:(c + 1) * LANES] for c in range(n_lane_blk)]
            mx = blk[0]
            for c in range(1, n_lane_blk):
                mx = jnp.maximum(mx, blk[c])
            m_old = m_ref[r, :]
            m_new = jnp.maximum(m_old, jnp.max(mx, axis=-1, keepdims=True))
            alpha = jnp.exp2(m_old - m_new)
            ps = [jnp.exp2(b - m_new) for b in blk]
            l_add = ps[0]
            for c in range(1, n_lane_blk):
                l_add = l_add + ps[c]
            l_ref[r, :] = alpha * l_ref[r, :] + l_add
            m_ref[r, :] = m_new
            alpha_ref[slot, r, :] = alpha
            p_ref[slot, r, :] = jnp.concatenate([x.astype(BF16) for x in ps], axis=1)

    @pl.when(kt < last_kt)
    def _():
        update(False)

    @pl.when(kt == last_kt)
    def _():
        update(True)
        apply_pv(slot, vlast_ref)
        gates = g_ref[...]
        for z, r in enumerate(head_rows):
            cols = slice(z * HEAD_DIM, (z + 1) * HEAD_DIM)
            g_s = gates[:, z * N_NSA_BRANCHES + 1:z * N_NSA_BRANCHES + 2]
            l_row = jnp.sum(l_ref[r, :], axis=-1, keepdims=True)
            o_s = acc_ref[r, :] / jnp.maximum(l_row, 1e-30)
            o_ref[:, cols] = (ocw_ref[:, cols] + g_s * o_s).astype(o_ref.dtype)


def _sel_attention(q, sel_bias, ks_ext, vs, ocw, gates):
    s = q.shape[0]
    n_qb = s // SEL_Q_TILE
    grp_w = Q_PER_KV * HEAD_DIM
    qb_of, kt_of = [], []
    for qb in range(n_qb):
        for kt in range((qb * SEL_Q_TILE) // K_TILE + 1):
            qb_of.append(qb)
            kt_of.append(kt)
    qb_arr = jnp.asarray(np.asarray(qb_of, np.int32))
    kt_arr = jnp.asarray(np.asarray(kt_of, np.int32))
    grid_spec = pltpu.PrefetchScalarGridSpec(
        num_scalar_prefetch=2,
        grid=(N_KV_GROUPS, len(qb_of)),
        in_specs=[pl.BlockSpec((SEL_Q_TILE, grp_w), lambda g, i, qbr, ktr: (qbr[i], g)),
                  pl.BlockSpec((None, SEL_Q_TILE, LANES), lambda g, i, qbr, ktr: (g, qbr[i], 0)),
                  pl.BlockSpec((K_TILE, 2 * HEAD_DIM), lambda g, i, qbr, ktr: (ktr[i], g)),
                  pl.BlockSpec((K_TILE, HEAD_DIM), lambda g, i, qbr, ktr: (jnp.maximum(ktr[i] - 1, 0), g)),
                  pl.BlockSpec((K_TILE, HEAD_DIM),
                               lambda g, i, qbr, ktr: ((qbr[i] * SEL_Q_TILE) // K_TILE, g)),
                  pl.BlockSpec((SEL_Q_TILE, grp_w), lambda g, i, qbr, ktr: (qbr[i], g)),
                  pl.BlockSpec((SEL_Q_TILE, LANES), lambda g, i, qbr, ktr: (qbr[i], g))],
        out_specs=pl.BlockSpec((SEL_Q_TILE, grp_w), lambda g, i, qbr, ktr: (qbr[i], g)),
        scratch_shapes=[pltpu.VMEM((SEL_ROWS, 2 * HEAD_DIM), BF16),
                        pltpu.VMEM((SEL_ROWS, LANES), F32),
                        pltpu.VMEM((SEL_ROWS, LANES), F32),
                        pltpu.VMEM((SEL_ROWS, HEAD_DIM), F32),
                        pltpu.VMEM((2, SEL_ROWS, K_TILE), BF16),
                        pltpu.VMEM((2, SEL_ROWS, LANES), F32)],
    )
    return pl.pallas_call(
        _sel_attn_kernel,
        out_shape=jax.ShapeDtypeStruct((s, ATTN_WIDTH), BF16),
        grid_spec=grid_spec,
        compiler_params=_params("parallel", "arbitrary"),
        name="sel_attention",
    )(qb_arr, kt_arr, q, sel_bias, ks_ext, vs, vs, ocw, gates)


def _rnn_kernel(rx_ref, rxs_ref, ry_ref, rys_ref, cw_ref, cb_ref, wa_ref, ba_ref, wx_ref, bx_ref, lam_ref,
                o_ref, ext_ref, a_ref, b_ref, h_ref, *, tt, tc):
    ti = pl.program_id(1)
    n_blk = tc // RNN_BLOCK_DIM
    halo = SUBLANES

    @pl.when(ti == 0)
    def _():
        ext_ref[0:halo, :] = jnp.zeros((halo, tc), F32)
        h_ref[...] = jnp.zeros(h_ref.shape, F32)

    @pl.when(ti > 0)
    def _():
        ext_ref[0:halo, :] = ext_ref[tt:tt + halo, :]

    ext_ref[halo:halo + tt, :] = _unshift(rx_ref[...], rxs_ref[...])
    xr = cb_ref[...] + ext_ref[pl.ds(halo - (RNN_CONV - 1), tt), :] * cw_ref[0:1, :]
    for j in range(1, RNN_CONV):
        xr = xr + ext_ref[pl.ds(halo - (RNN_CONV - 1) + j, tt), :] * cw_ref[j:j + 1, :]

    sp = jnp.maximum(-lam_ref[...], 0.0) + jnp.log(1.0 + jnp.exp(-jnp.abs(lam_ref[...])))
    xb = xr.astype(BF16)
    for blk in range(n_blk):
        cols = slice(blk * RNN_BLOCK_DIM, (blk + 1) * RNN_BLOCK_DIM)
        xs = xb[:, cols]
        r = jax.nn.sigmoid(jnp.dot(xs, wa_ref[blk], preferred_element_type=F32) + ba_ref[:, cols])
        i = jax.nn.sigmoid(jnp.dot(xs, wx_ref[blk], preferred_element_type=F32) + bx_ref[:, cols])
        log_a = -RG_C * r * sp[:, cols]
        a_ref[:, cols] = jnp.exp(log_a)
        b_ref[:, cols] = jnp.sqrt(1.0 - jnp.exp(2.0 * log_a)) * (i * xr[:, cols])

    row = lax.broadcasted_iota(jnp.int32, (SUBLANES, tc), 0)

    def scan_rows(i, carry):
        r0 = pl.multiple_of(i * SUBLANES, SUBLANES)
        a8 = a_ref[pl.ds(r0, SUBLANES), :]
        b8 = b_ref[pl.ds(r0, SUBLANES), :]
        for d in (1, 2, 4):
            keep = row >= d
            a_sh = pltpu.roll(a8, d, axis=0)
            b_sh = pltpu.roll(b8, d, axis=0)
            b8 = jnp.where(keep, a8 * b_sh + b8, b8)
            a8 = jnp.where(keep, a8 * a_sh, a8)
        h8 = a8 * carry + b8
        b_ref[pl.ds(r0, SUBLANES), :] = h8
        return jnp.broadcast_to(h8[SUBLANES - 1:SUBLANES, :], (SUBLANES, tc))

    h_ref[...] = lax.fori_loop(0, tt // SUBLANES, scan_rows, h_ref[...])
    ry = _unshift(ry_ref[...], rys_ref[...])
    o_ref[...] = (b_ref[...] * jax.nn.gelu(ry, approximate=True)).astype(o_ref.dtype)


def _rnn_branch(proj, cw, cb, wa, ba, wx, bx, lam, *, tt, tc):
    s = proj.shape[0]
    n_cb = RNN_WIDTH // tc
    blk_per = tc // RNN_BLOCK_DIM
    rx_blk = COL_RX // tc
    ry_blk = COL_RY // tc
    per = tc // LANES
    vec = pl.BlockSpec((1, tc), lambda c, t: (0, c))
    wspec = pl.BlockSpec((blk_per, RNN_BLOCK_DIM, RNN_BLOCK_DIM), lambda c, t: (c, 0, 0))
    return pl.pallas_call(
        functools.partial(_rnn_kernel, tt=tt, tc=tc),
        out_shape=jax.ShapeDtypeStruct((s, RNN_WIDTH), BF16),
        grid=(n_cb, s // tt),
        in_specs=[pl.BlockSpec((tt, tc), lambda c, t: (t, rx_blk + c)),
                  pl.BlockSpec((tt, LANES), lambda c, t: (t, (rx_blk + c + 1) * per)),
                  pl.BlockSpec((tt, tc), lambda c, t: (t, ry_blk + c)),
                  pl.BlockSpec((tt, LANES), lambda c, t: (t, (ry_blk + c + 1) * per)),
                  pl.BlockSpec((RNN_CONV, tc), lambda c, t: (0, c)),
                  vec, wspec, vec, wspec, vec, vec],
        out_specs=pl.BlockSpec((tt, tc), lambda c, t: (t, c)),
        scratch_shapes=[pltpu.VMEM((tt + SUBLANES, tc), F32),
                        pltpu.VMEM((tt, tc), F32),
                        pltpu.VMEM((tt, tc), F32),
                        pltpu.VMEM((SUBLANES, tc), F32)],
        compiler_params=_params("parallel", "arbitrary"),
        name="rg_lru",
    )(proj, proj, proj, proj, cw, cb, wa, ba, wx, bx, lam)


def _overlap_t(s):
    n_chunks = s // CMP_STRIDE
    n_cmp = n_chunks - 1
    n_sel = s // SEL_BLOCK
    cmp_start = np.arange(n_cmp) * CMP_STRIDE
    sel_start = np.arange(n_sel) * SEL_BLOCK
    ov = np.clip(np.minimum(cmp_start[:, None] + CMP_LEN, sel_start[None, :] + SEL_BLOCK)
                 - np.maximum(cmp_start[:, None], sel_start[None, :]), 0, None) / CMP_LEN
    out = np.zeros((LANES, n_chunks), np.float32)
    out[:n_sel, :n_cmp] = ov.T
    return out


def _block_onehot(s):
    return (np.arange(s)[:, None] // SEL_BLOCK == np.arange(LANES)[None, :]).astype(np.float32)


def kernel(x, c, positions, w_cond, b_cond, w_mod, b_mod, norm_mix, norm_ffn, w_in, q_norm, k_norm, cmp_pe_k, cmp_w_k, cmp_pe_v, cmp_w_v, rnn_conv_w, rnn_conv_b, rg_w_a, rg_b_a, rg_w_x, rg_b_x, rg_lambda, w_attn_up, w_rnn_up, w_out, w_ffn_in, ffn_conv_w, ffn_conv_b, w_ffn_down):
    b, s, d = x.shape
    depth = w_in.shape[0]
    assert b == 1 and d == D_MODEL and s % 2048 == 0 and s // SEL_BLOCK <= LANES

    c_emb = _vecmat(c.reshape(1, d, 1), w_cond[None], b_cond.reshape(1, 1, -1), silu=True, tn=w_cond.shape[1])
    c_col = jnp.broadcast_to(c_emb.reshape(1, -1, 1), (depth, c_emb.shape[-1], 1))
    mod = _vecmat(c_col, w_mod, b_mod[:, None, :], silu=False, tn=2048)

    inv_freq = ROPE_THETA ** (-jnp.arange(0, ROT_DIM, 2, dtype=jnp.float32) / ROT_DIM)
    freq_row = jnp.concatenate([inv_freq, inv_freq, jnp.zeros((LANES - ROT_DIM,), F32)])[None, :]
    rope_c, rope_sa, rope_sb = _rope_tables(positions.reshape(s, 1), freq_row, tr=512)

    ovt = jnp.asarray(_overlap_t(s), BF16)
    blk_onehot = jnp.asarray(_block_onehot(s), BF16)

    w_attn_in = w_in[:, :, :GATE_SRC].astype(BF16)
    w_rest_in = w_in[:, :, GATE_SRC:GATE_SRC + N_PROJ_REST].astype(BF16)
    n_tail = w_in.shape[2] - (GATE_SRC + N_PROJ_REST)
    assert n_tail == PROJ_SHIFT
    w_tail = jnp.pad(w_in[:, :, GATE_SRC + N_PROJ_REST:], ((0, 0), (0, 0), (0, LANES - n_tail))).astype(BF16)
    w_attn_up_b = w_attn_up.astype(BF16)
    w_rnn_up_b = w_rnn_up.astype(BF16)
    w_out_b = w_out.astype(BF16)
    w_ffn_in_b = w_ffn_in.astype(BF16)
    w_ffn_down_b = w_ffn_down.astype(BF16)

    h = x.reshape(s, d)
    for l in range(depth):
        sh1, sc1, g1, sh2, sc2, g2 = [mod[l, :, i * d:(i + 1) * d] for i in range(N_MOD)]
        proj_a, gr_tail, u = _norm_matmul(h, norm_mix[l][None], sc1, sh1, w_attn_in, w_tail,
                                          layer=l, tm=512, tn=1024, out_dtype=F32)
        proj = _matmul(u, w_rest_in, layer=l, tm=512, tn=1024, out_dtype=F32)
        q, kc, ks_ext, vs, kw, vw, gates = _prep(proj_a, proj, rope_c, rope_sa, rope_sb,
                                                 q_norm[l][None], k_norm[l], blk_onehot, tr=256)
        k_cmp = _compress(kc, 0, cmp_pe_k[l], cmp_w_k[l].astype(BF16), k_norm[l][0:1], do_norm=True)
        v_cmp = _compress(proj_a, COL_VC // HEAD_DIM, cmp_pe_v[l], cmp_w_v[l].astype(BF16),
                          k_norm[l][0:1], do_norm=False)
        ocw, sel_bias = _cw_attention(q, k_cmp, v_cmp, ovt, kw, vw, gates)
        attn = _sel_attention(q, sel_bias, ks_ext, vs, ocw, gates)
        rnn = _rnn_branch(proj, rnn_conv_w[l], rnn_conv_b[l][None], rg_w_a[l].astype(BF16), rg_b_a[l][None],
                          rg_w_x[l].astype(BF16), rg_b_x[l][None], rg_lambda[l][None], tt=512, tc=1024)
        merged = _merge(attn, rnn, w_attn_up_b, w_rnn_up_b, proj, gr_tail, layer=l, tm=512, tn=1024)
        h = _matmul_residual(merged, w_out_b, h, g1, layer=l, tm=512, tn=1024)
        act = _ffn_in(h, norm_ffn[l][None], sc2, sh2, w_ffn_in_b, ffn_conv_w[l], ffn_conv_b[l][None],
                      layer=l, tm=512, tn=512)
        h = _matmul_residual(act, w_ffn_down_b, h, g2, layer=l, tm=512, tn=512)
    return h.reshape(b, s, d)
```

```python
import functools

import numpy as np
import jax
import jax.numpy as jnp
from jax import lax
from jax.experimental import pallas as pl
from jax.experimental.pallas import tpu as pltpu

F32 = jnp.float32
BF16 = jnp.bfloat16

D_MODEL = 4096
N_Q_HEADS = 16
N_KV_GROUPS = 4
HEAD_DIM = 128
Q_PER_KV = N_Q_HEADS // N_KV_GROUPS
ATTN_WIDTH = N_Q_HEADS * HEAD_DIM
KV_WIDTH = N_KV_GROUPS * HEAD_DIM
N_NSA_BRANCHES = 3
ROT_DIM = HEAD_DIM // 4
ROPE_THETA = 500000.0
CMP_LEN = 32
CMP_STRIDE = 16
SEL_BLOCK = 64
SEL_TOP = 16
N_LOCAL_SEL = 2
WINDOW = 512
FORCE_SCORE = 2.0 * Q_PER_KV + 1.0
RNN_WIDTH = 2048
RNN_BLOCKS = 16
RNN_BLOCK_DIM = RNN_WIDTH // RNN_BLOCKS
RNN_CONV = 4
RG_C = 8.0
D_FF = 2 * D_MODEL
FFN_CONV = 3
N_MOD = 6
EPS = 1e-6
NEG_INF = -1e30
ATTN_SCALE = HEAD_DIM ** -0.5
LOG2_E = 1.4426950408889634
Q_SCALE = ATTN_SCALE * LOG2_E

LANES = 128
SUBLANES = 8
VMEM_LIMIT_BYTES = 56 * 1024 * 1024

COL_Q = 0
COL_KC = ATTN_WIDTH
COL_VC = COL_KC + KV_WIDTH
COL_KS = COL_VC + KV_WIDTH
COL_VS = COL_KS + KV_WIDTH
COL_KW = COL_VS + KV_WIDTH
COL_VW = COL_KW + KV_WIDTH
N_PROJ_ATTN = COL_VW + KV_WIDTH
GATE_SRC = N_PROJ_ATTN
PROJ_SHIFT = N_Q_HEADS * N_NSA_BRANCHES
N_PROJ_REST = 2 * RNN_WIDTH + 2 * D_MODEL
COL_RX = 0
COL_RY = COL_RX + RNN_WIDTH
COL_GA = COL_RY + RNN_WIDTH
COL_GR = COL_GA + D_MODEL

Q_TILE = 256
K_TILE = 1024
SEL_Q_TILE = 512
SEL_ROWS = Q_PER_KV * SEL_Q_TILE
N_WIN_BLK = WINDOW // Q_TILE + 1
NORM_CHUNK = 64


def _unshift(main, spill):
    return jnp.concatenate([main, spill], axis=1)[:, PROJ_SHIFT:PROJ_SHIFT + main.shape[1]]


def _params(*sem):
    return pltpu.CompilerParams(dimension_semantics=sem, vmem_limit_bytes=VMEM_LIMIT_BYTES)


def _vecmat_kernel(x_ref, w_ref, b_ref, o_ref, *, silu):
    y = jnp.sum(w_ref[...] * x_ref[...], axis=0, keepdims=True) + b_ref[...]
    if silu:
        y = y * jax.nn.sigmoid(y)
    o_ref[...] = y


def _vecmat(x_col, w, b, *, silu, tn):
    n_l, k, n = w.shape
    return pl.pallas_call(
        functools.partial(_vecmat_kernel, silu=silu),
        out_shape=jax.ShapeDtypeStruct((n_l, 1, n), F32),
        grid=(n_l, n // tn),
        in_specs=[pl.BlockSpec((None, k, 1), lambda l, j: (l, 0, 0)),
                  pl.BlockSpec((None, k, tn), lambda l, j: (l, 0, j)),
                  pl.BlockSpec((None, 1, tn), lambda l, j: (l, 0, j))],
        out_specs=pl.BlockSpec((None, 1, tn), lambda l, j: (l, 0, j)),
        compiler_params=_params("parallel", "parallel"),
        name="vecmat",
    )(x_col, w, b)


def _rope_table_kernel(pos_ref, freq_ref, c_ref, sa_ref, sb_ref):
    ang = pos_ref[...].astype(F32) * freq_ref[...]
    lane = lax.broadcasted_iota(jnp.int32, ang.shape, 1)
    cos = jnp.cos(ang)
    sin = jnp.sin(ang)
    c_ref[...] = jnp.where(lane < ROT_DIM, cos, 1.0)
    sa_ref[...] = jnp.where(lane < ROT_DIM // 2, -sin, 0.0)
    sb_ref[...] = jnp.where((lane >= ROT_DIM // 2) & (lane < ROT_DIM), sin, 0.0)


def _rope_tables(pos_col, freq_row, tr):
    s = pos_col.shape[0]
    spec = pl.BlockSpec((tr, LANES), lambda i: (i, 0))
    return pl.pallas_call(
        _rope_table_kernel,
        out_shape=[jax.ShapeDtypeStruct((s, LANES), F32)] * 3,
        grid=(s // tr,),
        in_specs=[pl.BlockSpec((tr, 1), lambda i: (i, 0)),
                  pl.BlockSpec((1, LANES), lambda i: (0, 0))],
        out_specs=[spec, spec, spec],
        compiler_params=_params("parallel"),
        name="rope_tables",
    )(pos_col, freq_row)


def _ada_norm(h_ref, gain_ref, sc_ref, sh_ref, u_ref):
    for r0 in range(0, h_ref.shape[0], NORM_CHUNK):
        x = h_ref[r0:r0 + NORM_CHUNK, :]
        y = x * lax.rsqrt(jnp.mean(x * x, axis=-1, keepdims=True) + EPS)
        u = (y * gain_ref[...]) * (1.0 + sc_ref[...]) + sh_ref[...]
        u_ref[r0:r0 + NORM_CHUNK, :] = u.astype(BF16)


def _norm_mm_kernel(h_ref, gain_ref, sc_ref, sh_ref, w_ref, wg_ref, o_ref, og_ref, u_ref):
    @pl.when(pl.program_id(1) == 0)
    def _():
        _ada_norm(h_ref, gain_ref, sc_ref, sh_ref, u_ref)
        og_ref[...] = jnp.dot(u_ref[...], wg_ref[...], preferred_element_type=F32)

    o_ref[...] = jnp.dot(u_ref[...], w_ref[...], preferred_element_type=F32).astype(o_ref.dtype)


def _norm_matmul(h, gain, sc, sh, w, wg, *, layer, n, tm, tn, out_dtype):
    s, d = h.shape
    ng = wg.shape[2]
    row = pl.BlockSpec((1, d), lambda i, j: (0, 0))
    return pl.pallas_call(
        _norm_mm_kernel,
        out_shape=[jax.ShapeDtypeStruct((s, n), out_dtype), jax.ShapeDtypeStruct((s, ng), F32),
                   jax.ShapeDtypeStruct((s, d), BF16)],
        grid=(s // tm, n // tn),
        in_specs=[pl.BlockSpec((tm, d), lambda i, j: (i, 0)), row, row, row,
                  pl.BlockSpec((None, d, tn), lambda i, j: (layer, 0, j)),
                  pl.BlockSpec((None, d, ng), lambda i, j: (layer, 0, 0))],
        out_specs=[pl.BlockSpec((tm, tn), lambda i, j: (i, j)),
                   pl.BlockSpec((tm, ng), lambda i, j: (i, 0)),
                   pl.BlockSpec((tm, d), lambda i, j: (i, 0))],
        compiler_params=_params("parallel", "arbitrary"),
        name="norm_matmul",
    )(h, gain, sc, sh, w, wg)


def _mm_kernel(a_ref, w_ref, o_ref):
    o_ref[...] = jnp.dot(a_ref[...], w_ref[...], preferred_element_type=F32).astype(o_ref.dtype)


def _matmul(a, w, *, layer, col0, n, tm, tn, out_dtype):
    s, k = a.shape
    blk0 = col0 // tn
    assert blk0 * tn == col0
    return pl.pallas_call(
        _mm_kernel,
        out_shape=jax.ShapeDtypeStruct((s, n), out_dtype),
        grid=(s // tm, n // tn),
        in_specs=[pl.BlockSpec((tm, k), lambda i, j: (i, 0)),
                  pl.BlockSpec((None, k, tn), lambda i, j: (layer, 0, blk0 + j))],
        out_specs=pl.BlockSpec((tm, tn), lambda i, j: (i, j)),
        compiler_params=_params("parallel", "parallel"),
        name="matmul",
    )(a, w)


def _ffn_in_kernel(h_ref, gain_ref, sc_ref, sh_ref, wg_ref, wu_ref, cw_ref, cb_ref, o_ref,
                   u_ref, ext_ref, carry_ref, *, tm):
    i = pl.program_id(0)
    j = pl.program_id(1)
    halo = SUBLANES

    @pl.when(j == 0)
    def _():
        _ada_norm(h_ref, gain_ref, sc_ref, sh_ref, u_ref)

    @pl.when(i == 0)
    def _():
        ext_ref[0:halo, :] = jnp.zeros((halo, ext_ref.shape[1]), F32)

    @pl.when(i > 0)
    def _():
        ext_ref[0:halo, :] = carry_ref[j]

    ext_ref[halo:halo + tm, :] = jnp.dot(u_ref[...], wg_ref[...], preferred_element_type=F32)
    carry_ref[j] = ext_ref[tm:tm + halo, :]
    y = cb_ref[...] + ext_ref[pl.ds(halo - (FFN_CONV - 1), tm), :] * cw_ref[0:1, :]
    for t in range(1, FFN_CONV):
        y = y + ext_ref[pl.ds(halo - (FFN_CONV - 1) + t, tm), :] * cw_ref[t:t + 1, :]
    up = jnp.dot(u_ref[...], wu_ref[...], preferred_element_type=F32)
    o_ref[...] = (y * jax.nn.sigmoid(y) * up).astype(o_ref.dtype)


def _ffn_in(h, gain, sc, sh, w, cw, cb, *, layer, tm, tn):
    s, d = h.shape
    n_j = D_FF // tn
    row = pl.BlockSpec((1, d), lambda i, j: (0, 0))
    return pl.pallas_call(
        functools.partial(_ffn_in_kernel, tm=tm),
        out_shape=jax.ShapeDtypeStruct((s, D_FF), BF16),
        grid=(s // tm, n_j),
        in_specs=[pl.BlockSpec((tm, d), lambda i, j: (i, 0)), row, row, row,
                  pl.BlockSpec((None, d, tn), lambda i, j: (layer, 0, j)),
                  pl.BlockSpec((None, d, tn), lambda i, j: (layer, 0, n_j + j)),
                  pl.BlockSpec((FFN_CONV, tn), lambda i, j: (0, j)),
                  pl.BlockSpec((1, tn), lambda i, j: (0, j))],
        out_specs=pl.BlockSpec((tm, tn), lambda i, j: (i, j)),
        scratch_shapes=[pltpu.VMEM((tm, d), BF16),
                        pltpu.VMEM((tm + SUBLANES, tn), F32),
                        pltpu.VMEM((n_j, SUBLANES, tn), F32)],
        compiler_params=_params("arbitrary", "arbitrary"),
        name="ffn_in",
    )(h, gain, sc, sh, w, w, cw, cb)


def _mm_res_kernel(a_ref, w_ref, h_ref, g_ref, o_ref):
    y = jnp.dot(a_ref[...], w_ref[...], preferred_element_type=F32)
    o_ref[...] = h_ref[...] + g_ref[...] * y


def _matmul_residual(a, w, h, g, *, layer, tm, tn):
    s, k = a.shape
    n = w.shape[2]
    return pl.pallas_call(
        _mm_res_kernel,
        out_shape=jax.ShapeDtypeStruct((s, n), F32),
        grid=(s // tm, n // tn),
        in_specs=[pl.BlockSpec((tm, k), lambda i, j: (i, 0)),
                  pl.BlockSpec((None, k, tn), lambda i, j: (layer, 0, j)),
                  pl.BlockSpec((tm, tn), lambda i, j: (i, j)),
                  pl.BlockSpec((1, tn), lambda i, j: (0, j))],
        out_specs=pl.BlockSpec((tm, tn), lambda i, j: (i, j)),
        compiler_params=_params("parallel", "parallel"),
        name="matmul_residual",
    )(a, w, h, g)


def _merge_kernel(attn_ref, rnn_ref, wa_ref, wr_ref, ga_ref, gas_ref, gr_ref, grs_ref, tail_ref, o_ref):
    ya = jnp.dot(attn_ref[...], wa_ref[...], preferred_element_type=F32)
    yr = jnp.dot(rnn_ref[...], wr_ref[...], preferred_element_type=F32)
    g_attn = _unshift(ga_ref[...], gas_ref[...])
    is_last = pl.program_id(1) == pl.num_programs(1) - 1
    g_rnn = _unshift(gr_ref[...], jnp.where(is_last, tail_ref[...], grs_ref[...]))
    o_ref[...] = (jax.nn.sigmoid(g_attn) * ya + jax.nn.sigmoid(g_rnn) * yr).astype(o_ref.dtype)


def _merge(attn, rnn, wa, wr, proj, tail, *, layer, tm, tn):
    s, k = attn.shape
    n = wa.shape[2]
    ga_blk = COL_GA // tn
    gr_blk = COL_GR // tn
    per = tn // LANES
    last_spill = proj.shape[1] // LANES - 1
    return pl.pallas_call(
        _merge_kernel,
        out_shape=jax.ShapeDtypeStruct((s, n), BF16),
        grid=(s // tm, n // tn),
        in_specs=[pl.BlockSpec((tm, k), lambda i, j: (i, 0)),
                  pl.BlockSpec((tm, k), lambda i, j: (i, 0)),
                  pl.BlockSpec((None, k, tn), lambda i, j: (layer, 0, j)),
                  pl.BlockSpec((None, k, tn), lambda i, j: (layer, 0, j)),
                  pl.BlockSpec((tm, tn), lambda i, j: (i, ga_blk + j)),
                  pl.BlockSpec((tm, LANES), lambda i, j: (i, (ga_blk + j + 1) * per)),
                  pl.BlockSpec((tm, tn), lambda i, j: (i, gr_blk + j)),
                  pl.BlockSpec((tm, LANES), lambda i, j: (i, jnp.minimum((gr_blk + j + 1) * per, last_spill))),
                  pl.BlockSpec((tm, LANES), lambda i, j: (i, 0))],
        out_specs=pl.BlockSpec((tm, tn), lambda i, j: (i, j)),
        compiler_params=_params("parallel", "parallel"),
        name="merge",
    )(attn, rnn, wa, wr, proj, proj, proj, proj, tail)


def _head_norm(x, gain):
    return x * lax.rsqrt(jnp.mean(x * x, axis=-1, keepdims=True) + EPS) * gain


def _rope(x, c, sa, sb):
    return (x * c + pltpu.roll(x, LANES - ROT_DIM // 2, axis=1) * sa
            + pltpu.roll(x, ROT_DIM // 2, axis=1) * sb)


def _prep_kernel(p_ref, graw_ref, c_ref, sa_ref, sb_ref, qn_ref, kn_ref, blk_ref,
                 q_ref, kc_ref, ks_ref, vs_ref, kw_ref, vw_ref, g_ref):
    c, sa, sb = c_ref[...], sa_ref[...], sb_ref[...]
    qn = qn_ref[...]
    for hd in range(N_Q_HEADS):
        cols = slice(hd * HEAD_DIM, (hd + 1) * HEAD_DIM)
        q_ref[:, cols] = (_rope(_head_norm(p_ref[:, cols], qn), c, sa, sb) * Q_SCALE).astype(BF16)
    for g in range(N_KV_GROUPS):
        cols = slice(g * HEAD_DIM, (g + 1) * HEAD_DIM)

        def src(base):
            return p_ref[:, base + g * HEAD_DIM:base + (g + 1) * HEAD_DIM]

        kc_ref[:, cols] = _rope(src(COL_KC), c, sa, sb)
        ks_ref[:, 2 * g * HEAD_DIM:(2 * g + 1) * HEAD_DIM] = _rope(
            _head_norm(src(COL_KS), kn_ref[1:2, :]), c, sa, sb).astype(BF16)
        ks_ref[:, (2 * g + 1) * HEAD_DIM:(2 * g + 2) * HEAD_DIM] = blk_ref[...]
        kw_ref[:, cols] = _rope(_head_norm(src(COL_KW), kn_ref[2:3, :]), c, sa, sb).astype(BF16)
        vs_ref[:, cols] = src(COL_VS).astype(BF16)
        vw_ref[:, cols] = src(COL_VW).astype(BF16)
    sig = jax.nn.sigmoid(graw_ref[...])
    per = Q_PER_KV * N_NSA_BRANCHES
    for g in range(N_KV_GROUPS):
        g_ref[:, g * LANES:(g + 1) * LANES] = sig if g == 0 else pltpu.roll(sig, LANES - g * per, axis=1)


def _prep(proj, graw, c, sa, sb, qn, kn, blk_onehot, *, tr):
    s = proj.shape[0]
    ng = N_KV_GROUPS * LANES
    tab = pl.BlockSpec((tr, LANES), lambda i: (i, 0))
    kv_spec = pl.BlockSpec((tr, KV_WIDTH), lambda i: (i, 0))
    return pl.pallas_call(
        _prep_kernel,
        out_shape=[jax.ShapeDtypeStruct((s, ATTN_WIDTH), BF16),
                   jax.ShapeDtypeStruct((s, KV_WIDTH), F32),
                   jax.ShapeDtypeStruct((s, 2 * KV_WIDTH), BF16),
                   jax.ShapeDtypeStruct((s, KV_WIDTH), BF16),
                   jax.ShapeDtypeStruct((s, KV_WIDTH), BF16),
                   jax.ShapeDtypeStruct((s, KV_WIDTH), BF16),
                   jax.ShapeDtypeStruct((s, ng), F32)],
        grid=(s // tr,),
        in_specs=[pl.BlockSpec((tr, N_PROJ_ATTN), lambda i: (i, 0)),
                  tab, tab, tab, tab,
                  pl.BlockSpec((1, HEAD_DIM), lambda i: (0, 0)),
                  pl.BlockSpec((N_NSA_BRANCHES, HEAD_DIM), lambda i: (0, 0)),
                  tab],
        out_specs=[pl.BlockSpec((tr, ATTN_WIDTH), lambda i: (i, 0)),
                   kv_spec, pl.BlockSpec((tr, 2 * KV_WIDTH), lambda i: (i, 0)),
                   kv_spec, kv_spec, kv_spec,
                   pl.BlockSpec((tr, ng), lambda i: (i, 0))],
        compiler_params=_params("parallel"),
        name="qk_prep",
    )(proj, graw, c, sa, sb, qn, kn, blk_onehot)


def _compress_kernel(x_ref, pe_ref, w_ref, gain_ref, o_ref, shift_ref, *, do_norm, n_chunks):
    acc_a = jnp.zeros((n_chunks, HEAD_DIM), F32)
    acc_b = jnp.zeros((n_chunks, HEAD_DIM), F32)
    for l in range(CMP_STRIDE):
        xl = x_ref[pl.ds(l, n_chunks, stride=CMP_STRIDE), :]
        xa = (xl + pe_ref[l:l + 1, :]).astype(BF16)
        xb = (xl + pe_ref[CMP_STRIDE + l:CMP_STRIDE + l + 1, :]).astype(BF16)
        acc_a = acc_a + jnp.dot(xa, w_ref[l], preferred_element_type=F32)
        acc_b = acc_b + jnp.dot(xb, w_ref[CMP_STRIDE + l], preferred_element_type=F32)
    shift_ref[0:n_chunks, :] = acc_b
    shift_ref[n_chunks:n_chunks + SUBLANES, :] = jnp.zeros((SUBLANES, HEAD_DIM), F32)
    out = acc_a + shift_ref[1:n_chunks + 1, :]
    if do_norm:
        out = _head_norm(out, gain_ref[...])
    o_ref[...] = out.astype(o_ref.dtype)


def _compress(x, col_blk0, pe, w, gain, *, do_norm):
    s = x.shape[0]
    n_chunks = s // CMP_STRIDE
    return pl.pallas_call(
        functools.partial(_compress_kernel, do_norm=do_norm, n_chunks=n_chunks),
        out_shape=jax.ShapeDtypeStruct((N_KV_GROUPS, n_chunks, HEAD_DIM), BF16),
        grid=(N_KV_GROUPS,),
        in_specs=[pl.BlockSpec((s, HEAD_DIM), lambda g: (0, col_blk0 + g)),
                  pl.BlockSpec((CMP_LEN, HEAD_DIM), lambda g: (0, 0)),
                  pl.BlockSpec((CMP_LEN, HEAD_DIM, HEAD_DIM), lambda g: (0, 0, 0)),
                  pl.BlockSpec((1, HEAD_DIM), lambda g: (0, 0))],
        out_specs=pl.BlockSpec((None, n_chunks, HEAD_DIM), lambda g: (g, 0, 0)),
        scratch_shapes=[pltpu.VMEM((n_chunks + SUBLANES, HEAD_DIM), F32)],
        compiler_params=_params("parallel"),
        name="compress",
    )(x, pe, w, gain)


def _dot_nt(a, b):
    return lax.dot_general(a, b, (((1,), (1,)), ((), ())), preferred_element_type=F32)


def _biased_softmax2(s, bias):
    s = s + bias
    p = jnp.exp2(s - jnp.max(s, axis=-1, keepdims=True))
    return p, jnp.sum(p, axis=-1, keepdims=True)


def _stack_heads(q_ref):
    return jnp.concatenate([q_ref[:, z * HEAD_DIM:(z + 1) * HEAD_DIM] for z in range(Q_PER_KV)], axis=0)


def _cw_attn_kernel(*refs, n_chunks):
    q_ref, kc_ref, vc_ref, ovt_ref = refs[:4]
    kw_refs = refs[4:4 + N_WIN_BLK]
    vw_refs = refs[4 + N_WIN_BLK:4 + 2 * N_WIN_BLK]
    g_ref, ocw_ref, sel_ref, score_ref, cnt_ref = refs[4 + 2 * N_WIN_BLK:]
    qb = pl.program_id(1)
    t0 = qb * Q_TILE
    q4 = _stack_heads(q_ref)
    head_rows = [slice(z * Q_TILE, (z + 1) * Q_TILE) for z in range(Q_PER_KV)]

    tq = t0 + lax.broadcasted_iota(jnp.int32, (Q_TILE, n_chunks), 0)
    n_id = lax.broadcasted_iota(jnp.int32, (Q_TILE, n_chunks), 1)
    bias_c = jnp.where((n_id * CMP_STRIDE + (CMP_LEN - 1) <= tq) & (n_id < n_chunks - 1), 0.0, NEG_INF)
    row_ok = jnp.where(t0 + lax.broadcasted_iota(jnp.int32, (Q_TILE, 1), 0) >= CMP_LEN - 1, 1.0, 0.0)
    s_c = _dot_nt(q4, kc_ref[...])
    p_heads = []
    for r in head_rows:
        p, l = _biased_softmax2(s_c[r], bias_c)
        p_heads.append(p * (row_ok / jnp.maximum(l, 1e-30)))
    o_c = jnp.dot(jnp.concatenate(p_heads, axis=0).astype(BF16), vc_ref[...],
                  preferred_element_type=F32)

    p_sum = p_heads[0]
    for z in range(1, Q_PER_KV):
        p_sum = p_sum + p_heads[z]
    p_hi = p_sum.astype(BF16)
    p_lo = (p_sum - p_hi.astype(F32)).astype(BF16)
    ovt = ovt_ref[...]
    imp_t = _dot_nt(ovt, p_hi) + _dot_nt(ovt, p_lo)
    j_id = lax.broadcasted_iota(jnp.int32, (LANES, Q_TILE), 0)
    cur = (t0 + lax.broadcasted_iota(jnp.int32, (LANES, Q_TILE), 1)) // SEL_BLOCK
    valid = j_id <= cur
    forced = (j_id == 0) | (valid & (j_id > cur - N_LOCAL_SEL))
    score_ref[...] = jnp.where(forced, FORCE_SCORE, jnp.where(valid, imp_t, -1.0))
    cnt_ref[...] = jnp.zeros(cnt_ref.shape, F32)
    n_grp = LANES // SUBLANES
    last_src_grp = ((t0 + Q_TILE - 1) // SEL_BLOCK) // SUBLANES
    sub = lax.broadcasted_iota(jnp.int32, (SUBLANES, Q_TILE), 0)
    for gj in range(n_grp):
        @pl.when(gj <= last_src_grp)
        def _():
            src = score_ref[gj * SUBLANES:(gj + 1) * SUBLANES, :]
            rows_b = [jnp.broadcast_to(src[r:r + 1, :], (SUBLANES, Q_TILE)) for r in range(SUBLANES)]
            for gi in range(n_grp):
                tgt = score_ref[gi * SUBLANES:(gi + 1) * SUBLANES, :]
                acc = cnt_ref[gi * SUBLANES:(gi + 1) * SUBLANES, :]
                for r in range(SUBLANES):
                    if gi < gj:
                        inc = jnp.where(rows_b[r] > tgt, 1.0, 0.0)
                    elif gi > gj:
                        inc = jnp.where(rows_b[r] >= tgt, 1.0, 0.0)
                    else:
                        inc = jnp.where(sub > r, jnp.where(rows_b[r] >= tgt, 1.0, 0.0),
                                        jnp.where(rows_b[r] > tgt, 1.0, 0.0))
                    acc = acc + inc
                cnt_ref[gi * SUBLANES:(gi + 1) * SUBLANES, :] = acc
    bias_t = jnp.where(cnt_ref[...] < float(SEL_TOP), 0.0, NEG_INF)
    sel_ref[...] = bias_t.T.astype(sel_ref.dtype)

    n_keys = N_WIN_BLK * Q_TILE
    tq_w = t0 + lax.broadcasted_iota(jnp.int32, (Q_TILE, n_keys), 0)
    pos = t0 - WINDOW + lax.broadcasted_iota(jnp.int32, (Q_TILE, n_keys), 1)
    bias_w = jnp.where((pos <= tq_w) & (pos > tq_w - WINDOW) & (pos >= 0), 0.0, NEG_INF)
    s_w = jnp.concatenate([_dot_nt(q4, kw_refs[i][...]) for i in range(N_WIN_BLK)], axis=1)
    pw_heads = []
    for r in head_rows:
        p, l = _biased_softmax2(s_w[r], bias_w)
        pw_heads.append((p * (1.0 / jnp.maximum(l, 1e-30))).astype(BF16))
    p_w = jnp.concatenate(pw_heads, axis=0)
    o_w = jnp.dot(p_w[:, 0:Q_TILE], vw_refs[0][...], preferred_element_type=F32)
    for i in range(1, N_WIN_BLK):
        o_w = o_w + jnp.dot(p_w[:, i * Q_TILE:(i + 1) * Q_TILE], vw_refs[i][...],
                            preferred_element_type=F32)

    gates = g_ref[...]
    for z, r in enumerate(head_rows):
        g_c = gates[:, z * N_NSA_BRANCHES:z * N_NSA_BRANCHES + 1]
        g_w = gates[:, z * N_NSA_BRANCHES + 2:z * N_NSA_BRANCHES + 3]
        ocw_ref[:, z * HEAD_DIM:(z + 1) * HEAD_DIM] = g_c * o_c[r] + g_w * o_w[r]


def _cw_attention(q, k_cmp, v_cmp, ovt, kw, vw, gates):
    s = q.shape[0]
    n_chunks = k_cmp.shape[1]
    n_qb = s // Q_TILE
    grp_w = Q_PER_KV * HEAD_DIM

    def win_spec(i):
        return pl.BlockSpec((Q_TILE, HEAD_DIM),
                            lambda g, qb: (jnp.maximum(qb - (N_WIN_BLK - 1) + i, 0), g))

    cmp_spec = pl.BlockSpec((None, n_chunks, HEAD_DIM), lambda g, qb: (g, 0, 0))
    in_specs = ([pl.BlockSpec((Q_TILE, grp_w), lambda g, qb: (qb, g)), cmp_spec, cmp_spec,
                 pl.BlockSpec((LANES, n_chunks), lambda g, qb: (0, 0))]
                + [win_spec(i) for i in range(N_WIN_BLK)] * 2
                + [pl.BlockSpec((Q_TILE, LANES), lambda g, qb: (qb, g))])
    return pl.pallas_call(
        functools.partial(_cw_attn_kernel, n_chunks=n_chunks),
        out_shape=[jax.ShapeDtypeStruct((s, ATTN_WIDTH), F32),
                   jax.ShapeDtypeStruct((N_KV_GROUPS, s, LANES), BF16)],
        grid=(N_KV_GROUPS, n_qb),
        in_specs=in_specs,
        out_specs=[pl.BlockSpec((Q_TILE, grp_w), lambda g, qb: (qb, g)),
                   pl.BlockSpec((None, Q_TILE, LANES), lambda g, qb: (g, qb, 0))],
        scratch_shapes=[pltpu.VMEM((LANES, Q_TILE), F32), pltpu.VMEM((LANES, Q_TILE), F32)],
        compiler_params=_params("parallel", "parallel"),
        name="cmp_win_attention",
    )(q, k_cmp, v_cmp, ovt, *([kw] * N_WIN_BLK), *([vw] * N_WIN_BLK), gates)


def _sel_attn_kernel(qb_ref, kt_ref, q_ref, sb_ref, k_ref, vprev_ref, vlast_ref, ocw_ref, g_ref, o_ref,
                     qx_ref, m_ref, l_ref, acc_ref, p_ref, alpha_ref):
    step = pl.program_id(1)
    qb = qb_ref[step]
    kt = kt_ref[step]
    last_kt = (qb * SEL_Q_TILE) // K_TILE
    n_lane_blk = K_TILE // LANES
    slot = kt % 2
    head_rows = [slice(z * SEL_Q_TILE, (z + 1) * SEL_Q_TILE) for z in range(Q_PER_KV)]

    @pl.when(kt == 0)
    def _():
        for z, r in enumerate(head_rows):
            qx_ref[r, 0:HEAD_DIM] = q_ref[:, z * HEAD_DIM:(z + 1) * HEAD_DIM]
            qx_ref[r, HEAD_DIM:2 * HEAD_DIM] = sb_ref[...]
        m_ref[...] = jnp.full(m_ref.shape, NEG_INF, F32)
        l_ref[...] = jnp.zeros(l_ref.shape, F32)
        acc_ref[...] = jnp.zeros(acc_ref.shape, F32)
        p_ref[1] = jnp.zeros(p_ref.shape[1:], BF16)
        alpha_ref[1] = jnp.ones(alpha_ref.shape[1:], F32)

    def apply_pv(src_slot, v_ref):
        v = v_ref[...]
        for r in head_rows:
            acc_ref[r, :] = alpha_ref[src_slot, r, :] * acc_ref[r, :] + jnp.dot(
                p_ref[src_slot, r, :], v, preferred_element_type=F32)

    def update(causal):
        apply_pv(1 - slot, vprev_ref)
        k = k_ref[...]
        if causal:
            tq = qb * SEL_Q_TILE + lax.broadcasted_iota(jnp.int32, (SEL_Q_TILE, K_TILE), 0)
            kpos = kt * K_TILE + lax.broadcasted_iota(jnp.int32, (SEL_Q_TILE, K_TILE), 1)
            cbias = jnp.where(kpos <= tq, 0.0, NEG_INF)
        for r in head_rows:
            s = _dot_nt(qx_ref[r, :], k)
            if causal:
                s = s + cbias
            blk = [s[:, c * LANES:(c + 1) * LANES] for c in range(n_lane_blk)]
            mx = blk[0]
            for c in range(1, n_lane_blk):
                mx = jnp.maximum(mx, blk[c])
            m_old = m_ref[r, :]
            m_new = jnp.maximum(m_old, jnp.max(mx, axis=-1, keepdims=True))
            alpha = jnp.exp2(m_old - m_new)
            ps = [jnp.exp2(b - m_new) for b in blk]
            l_add = ps[0]
            for c in range(1, n_lane_blk):
                l_add = l_add + ps[c]
            l_ref[r, :] = alpha * l_ref[r, :] + l_add
            m_ref[r, :] = m_new
            alpha_ref[slot, r, :] = alpha
            p_ref[slot, r, :] = jnp.concatenate([x.astype(BF16) for x in ps], axis=1)

    @pl.when(kt < last_kt)
    def _():
        update(False)

    @pl.when(kt == last_kt)
    def _():
        update(True)
        apply_pv(slot, vlast_ref)
        gates = g_ref[...]
        for z, r in enumerate(head_rows):
            cols = slice(z * HEAD_DIM, (z + 1) * HEAD_DIM)
            g_s = gates[:, z * N_NSA_BRANCHES + 1:z * N_NSA_BRANCHES + 2]
            l_row = jnp.sum(l_ref[r, :], axis=-1, keepdims=True)
            o_s = acc_ref[r, :] / jnp.maximum(l_row, 1e-30)
            o_ref[:, cols] = (ocw_ref[:, cols] + g_s * o_s).astype(o_ref.dtype)


def _sel_attention(q, sel_bias, ks_ext, vs, ocw, gates):
    s = q.shape[0]
    n_qb = s // SEL_Q_TILE
    grp_w = Q_PER_KV * HEAD_DIM
    qb_of, kt_of = [], []
    for qb in range(n_qb):
        for kt in range((qb * SEL_Q_TILE) // K_TILE + 1):
            qb_of.append(qb)
            kt_of.append(kt)
    qb_arr = jnp.asarray(np.asarray(qb_of, np.int32))
    kt_arr = jnp.asarray(np.asarray(kt_of, np.int32))
    grid_spec = pltpu.PrefetchScalarGridSpec(
        num_scalar_prefetch=2,
        grid=(N_KV_GROUPS, len(qb_of)),
        in_specs=[pl.BlockSpec((SEL_Q_TILE, grp_w), lambda g, i, qbr, ktr: (qbr[i], g)),
                  pl.BlockSpec((None, SEL_Q_TILE, LANES), lambda g, i, qbr, ktr: (g, qbr[i], 0)),
                  pl.BlockSpec((K_TILE, 2 * HEAD_DIM), lambda g, i, qbr, ktr: (ktr[i], g)),
                  pl.BlockSpec((K_TILE, HEAD_DIM), lambda g, i, qbr, ktr: (jnp.maximum(ktr[i] - 1, 0), g)),
                  pl.BlockSpec((K_TILE, HEAD_DIM),
                               lambda g, i, qbr, ktr: ((qbr[i] * SEL_Q_TILE) // K_TILE, g)),
                  pl.BlockSpec((SEL_Q_TILE, grp_w), lambda g, i, qbr, ktr: (qbr[i], g)),
                  pl.BlockSpec((SEL_Q_TILE, LANES), lambda g, i, qbr, ktr: (qbr[i], g))],
        out_specs=pl.BlockSpec((SEL_Q_TILE, grp_w), lambda g, i, qbr, ktr: (qbr[i], g)),
        scratch_shapes=[pltpu.VMEM((SEL_ROWS, 2 * HEAD_DIM), BF16),
                        pltpu.VMEM((SEL_ROWS, LANES), F32),
                        pltpu.VMEM((SEL_ROWS, LANES), F32),
                        pltpu.VMEM((SEL_ROWS, HEAD_DIM), F32),
                        pltpu.VMEM((2, SEL_ROWS, K_TILE), BF16),
                        pltpu.VMEM((2, SEL_ROWS, LANES), F32)],
    )
    return pl.pallas_call(
        _sel_attn_kernel,
        out_shape=jax.ShapeDtypeStruct((s, ATTN_WIDTH), BF16),
        grid_spec=grid_spec,
        compiler_params=_params("parallel", "arbitrary"),
        name="sel_attention",
    )(qb_arr, kt_arr, q, sel_bias, ks_ext, vs, vs, ocw, gates)


def _rnn_kernel(rx_ref, rxs_ref, ry_ref, rys_ref, cw_ref, cb_ref, wa_ref, ba_ref, wx_ref, bx_ref, lam_ref,
                o_ref, ext_ref, a_ref, b_ref, h_ref, *, tt, tc):
    ti = pl.program_id(1)
    n_blk = tc // RNN_BLOCK_DIM
    halo = SUBLANES

    @pl.when(ti == 0)
    def _():
        ext_ref[0:halo, :] = jnp.zeros((halo, tc), F32)
        h_ref[...] = jnp.zeros(h_ref.shape, F32)

    @pl.when(ti > 0)
    def _():
        ext_ref[0:halo, :] = ext_ref[tt:tt + halo, :]

    ext_ref[halo:halo + tt, :] = _unshift(rx_ref[...], rxs_ref[...])
    xr = cb_ref[...] + ext_ref[pl.ds(halo - (RNN_CONV - 1), tt), :] * cw_ref[0:1, :]
    for j in range(1, RNN_CONV):
        xr = xr + ext_ref[pl.ds(halo - (RNN_CONV - 1) + j, tt), :] * cw_ref[j:j + 1, :]

    sp = jnp.maximum(-lam_ref[...], 0.0) + jnp.log(1.0 + jnp.exp(-jnp.abs(lam_ref[...])))
    xb = xr.astype(BF16)
    for blk in range(n_blk):
        cols = slice(blk * RNN_BLOCK_DIM, (blk + 1) * RNN_BLOCK_DIM)
        xs = xb[:, cols]
        r = jax.nn.sigmoid(jnp.dot(xs, wa_ref[blk], preferred_element_type=F32) + ba_ref[:, cols])
        i = jax.nn.sigmoid(jnp.dot(xs, wx_ref[blk], preferred_element_type=F32) + bx_ref[:, cols])
        log_a = -RG_C * r * sp[:, cols]
        a_ref[:, cols] = jnp.exp(log_a)
        b_ref[:, cols] = jnp.sqrt(1.0 - jnp.exp(2.0 * log_a)) * (i * xr[:, cols])

    row = lax.broadcasted_iota(jnp.int32, (SUBLANES, tc), 0)

    def scan_rows(i, carry):
        r0 = pl.multiple_of(i * SUBLANES, SUBLANES)
        a8 = a_ref[pl.ds(r0, SUBLANES), :]
        b8 = b_ref[pl.ds(r0, SUBLANES), :]
        for d in (1, 2, 4):
            keep = row >= d
            a_sh = pltpu.roll(a8, d, axis=0)
            b_sh = pltpu.roll(b8, d, axis=0)
            b8 = jnp.where(keep, a8 * b_sh + b8, b8)
            a8 = jnp.where(keep, a8 * a_sh, a8)
        h8 = a8 * carry + b8
        b_ref[pl.ds(r0, SUBLANES), :] = h8
        return jnp.broadcast_to(h8[SUBLANES - 1:SUBLANES, :], (SUBLANES, tc))

    h_ref[...] = lax.fori_loop(0, tt // SUBLANES, scan_rows, h_ref[...])
    ry = _unshift(ry_ref[...], rys_ref[...])
    o_ref[...] = (b_ref[...] * jax.nn.gelu(ry, approximate=True)).astype(o_ref.dtype)


def _rnn_branch(proj, cw, cb, wa, ba, wx, bx, lam, *, tt, tc):
    s = proj.shape[0]
    n_cb = RNN_WIDTH // tc
    blk_per = tc // RNN_BLOCK_DIM
    rx_blk = COL_RX // tc
    ry_blk = COL_RY // tc
    per = tc // LANES
    vec = pl.BlockSpec((1, tc), lambda c, t: (0, c))
    wspec = pl.BlockSpec((blk_per, RNN_BLOCK_DIM, RNN_BLOCK_DIM), lambda c, t: (c, 0, 0))
    return pl.pallas_call(
        functools.partial(_rnn_kernel, tt=tt, tc=tc),
        out_shape=jax.ShapeDtypeStruct((s, RNN_WIDTH), BF16),
        grid=(n_cb, s // tt),
        in_specs=[pl.BlockSpec((tt, tc), lambda c, t: (t, rx_blk + c)),
                  pl.BlockSpec((tt, LANES), lambda c, t: (t, (rx_blk + c + 1) * per)),
                  pl.BlockSpec((tt, tc), lambda c, t: (t, ry_blk + c)),
                  pl.BlockSpec((tt, LANES), lambda c, t: (t, (ry_blk + c + 1) * per)),
                  pl.BlockSpec((RNN_CONV, tc), lambda c, t: (0, c)),
                  vec, wspec, vec, wspec, vec, vec],
        out_specs=pl.BlockSpec((tt, tc), lambda c, t: (t, c)),
        scratch_shapes=[pltpu.VMEM((tt + SUBLANES, tc), F32),
                        pltpu.VMEM((tt, tc), F32),
                        pltpu.VMEM((tt, tc), F32),
                        pltpu.VMEM((SUBLANES, tc), F32)],
        compiler_params=_params("parallel", "arbitrary"),
        name="rg_lru",
    )(proj, proj, proj, proj, cw, cb, wa, ba, wx, bx, lam)


def _overlap_t(s):
    n_chunks = s // CMP_STRIDE
    n_cmp = n_chunks - 1
    n_sel = s // SEL_BLOCK
    cmp_start = np.arange(n_cmp) * CMP_STRIDE
    sel_start = np.arange(n_sel) * SEL_BLOCK
    ov = np.clip(np.minimum(cmp_start[:, None] + CMP_LEN, sel_start[None, :] + SEL_BLOCK)
                 - np.maximum(cmp_start[:, None], sel_start[None, :]), 0, None) / CMP_LEN
    out = np.zeros((LANES, n_chunks), np.float32)
    out[:n_sel, :n_cmp] = ov.T
    return out


def _block_onehot(s):
    return (np.arange(s)[:, None] // SEL_BLOCK == np.arange(LANES)[None, :]).astype(np.float32)


def kernel(x, c, positions, w_cond, b_cond, w_mod, b_mod, norm_mix, norm_ffn, w_in, q_norm, k_norm, cmp_pe_k, cmp_w_k, cmp_pe_v, cmp_w_v, rnn_conv_w, rnn_conv_b, rg_w_a, rg_b_a, rg_w_x, rg_b_x, rg_lambda, w_attn_up, w_rnn_up, w_out, w_ffn_in, ffn_conv_w, ffn_conv_b, w_ffn_down):
    b, s, d = x.shape
    depth = w_in.shape[0]
    assert b == 1 and d == D_MODEL and s % 2048 == 0 and s // SEL_BLOCK <= LANES

    c_emb = _vecmat(c.reshape(1, d, 1), w_cond[None], b_cond.reshape(1, 1, -1), silu=True, tn=w_cond.shape[1])
    c_col = jnp.broadcast_to(c_emb.reshape(1, -1, 1), (depth, c_emb.shape[-1], 1))
    mod = _vecmat(c_col, w_mod, b_mod[:, None, :], silu=False, tn=2048)

    inv_freq = ROPE_THETA ** (-jnp.arange(0, ROT_DIM, 2, dtype=jnp.float32) / ROT_DIM)
    freq_row = jnp.concatenate([inv_freq, inv_freq, jnp.zeros((LANES - ROT_DIM,), F32)])[None, :]
    rope_c, rope_sa, rope_sb = _rope_tables(positions.reshape(s, 1), freq_row, tr=512)

    ovt = jnp.asarray(_overlap_t(s), BF16)
    blk_onehot = jnp.asarray(_block_onehot(s), BF16)

    w_in_b = w_in.astype(BF16)
    n_tail = w_in.shape[2] - (GATE_SRC + N_PROJ_REST)
    assert n_tail == PROJ_SHIFT
    w_tail = jnp.pad(w_in[:, :, GATE_SRC + N_PROJ_REST:], ((0, 0), (0, 0), (0, LANES - n_tail))).astype(BF16)
    w_attn_up_b = w_attn_up.astype(BF16)
    w_rnn_up_b = w_rnn_up.astype(BF16)
    w_out_b = w_out.astype(BF16)
    w_ffn_in_b = w_ffn_in.astype(BF16)
    w_ffn_down_b = w_ffn_down.astype(BF16)

    h = x.reshape(s, d)
    for l in range(depth):
        sh1, sc1, g1, sh2, sc2, g2 = [mod[l, :, i * d:(i + 1) * d] for i in range(N_MOD)]
        proj_a, gr_tail, u = _norm_matmul(h, norm_mix[l][None], sc1, sh1, w_in_b, w_tail,
                                          layer=l, n=N_PROJ_ATTN, tm=512, tn=1024, out_dtype=F32)
        proj = _matmul(u, w_in_b, layer=l, col0=GATE_SRC, n=N_PROJ_REST, tm=512, tn=1024, out_dtype=F32)
        q, kc, ks_ext, vs, kw, vw, gates = _prep(proj_a, proj, rope_c, rope_sa, rope_sb,
                                                 q_norm[l][None], k_norm[l], blk_onehot, tr=256)
        k_cmp = _compress(kc, 0, cmp_pe_k[l], cmp_w_k[l].astype(BF16), k_norm[l][0:1], do_norm=True)
        v_cmp = _compress(proj_a, COL_VC // HEAD_DIM, cmp_pe_v[l], cmp_w_v[l].astype(BF16),
                          k_norm[l][0:1], do_norm=False)
        ocw, sel_bias = _cw_attention(q, k_cmp, v_cmp, ovt, kw, vw, gates)
        attn = _sel_attention(q, sel_bias, ks_ext, vs, ocw, gates)
        rnn = _rnn_branch(proj, rnn_conv_w[l], rnn_conv_b[l][None], rg_w_a[l].astype(BF16), rg_b_a[l][None],
                          rg_w_x[l].astype(BF16), rg_b_x[l][None], rg_lambda[l][None], tt=512, tc=1024)
        merged = _merge(attn, rnn, w_attn_up_b, w_rnn_up_b, proj, gr_tail, layer=l, tm=512, tn=1024)
        h = _matmul_residual(merged, w_out_b, h, g1, layer=l, tm=512, tn=1024)
        act = _ffn_in(h, norm_ffn[l][None], sc2, sh2, w_ffn_in_b, ffn_conv_w[l], ffn_conv_b[l][None],
                      layer=l, tm=512, tn=512)
        h = _matmul_residual(act, w_ffn_down_b, h, g2, layer=l, tm=512, tn=512)
    return h.reshape(b, s, d)
```

```python
import functools

import numpy as np
import jax
import jax.numpy as jnp
from jax import lax
from jax.experimental import pallas as pl
from jax.experimental.pallas import tpu as pltpu

F32 = jnp.float32
BF16 = jnp.bfloat16

D_MODEL = 4096
N_Q_HEADS = 16
N_KV_GROUPS = 4
HEAD_DIM = 128
Q_PER_KV = N_Q_HEADS // N_KV_GROUPS
ATTN_WIDTH = N_Q_HEADS * HEAD_DIM
KV_WIDTH = N_KV_GROUPS * HEAD_DIM
N_NSA_BRANCHES = 3
ROT_DIM = HEAD_DIM // 4
ROPE_THETA = 500000.0
CMP_LEN = 32
CMP_STRIDE = 16
SEL_BLOCK = 64
SEL_TOP = 16
N_LOCAL_SEL = 2
WINDOW = 512
FORCE_SCORE = 2.0 * Q_PER_KV + 1.0
RNN_WIDTH = 2048
RNN_BLOCKS = 16
RNN_BLOCK_DIM = RNN_WIDTH // RNN_BLOCKS
RNN_CONV = 4
RG_C = 8.0
D_FF = 2 * D_MODEL
FFN_CONV = 3
N_MOD = 6
EPS = 1e-6
NEG_INF = -1e30
ATTN_SCALE = HEAD_DIM ** -0.5
LOG2_E = 1.4426950408889634
Q_SCALE = ATTN_SCALE * LOG2_E

LANES = 128
SUBLANES = 8
VMEM_LIMIT_BYTES = 56 * 1024 * 1024

COL_Q = 0
COL_KC = ATTN_WIDTH
COL_VC = COL_KC + KV_WIDTH
COL_KS = COL_VC + KV_WIDTH
COL_VS = COL_KS + KV_WIDTH
COL_KW = COL_VS + KV_WIDTH
COL_VW = COL_KW + KV_WIDTH
N_PROJ_ATTN = COL_VW + KV_WIDTH
GATE_SRC = N_PROJ_ATTN
PROJ_SHIFT = N_Q_HEADS * N_NSA_BRANCHES
N_PROJ_REST = 2 * RNN_WIDTH + 2 * D_MODEL
COL_RX = 0
COL_RY = COL_RX + RNN_WIDTH
COL_GA = COL_RY + RNN_WIDTH
COL_GR = COL_GA + D_MODEL

Q_TILE = 256
K_TILE = 1024
SEL_Q_TILE = 512
SEL_ROWS = Q_PER_KV * SEL_Q_TILE
N_WIN_BLK = WINDOW // Q_TILE + 1
NORM_CHUNK = 64


def _unshift(main, spill):
    return jnp.concatenate([main, spill], axis=1)[:, PROJ_SHIFT:PROJ_SHIFT + main.shape[1]]


def _params(*sem):
    return pltpu.CompilerParams(dimension_semantics=sem, vmem_limit_bytes=VMEM_LIMIT_BYTES)


def _vecmat_kernel(x_ref, w_ref, b_ref, o_ref, *, silu):
    y = jnp.sum(w_ref[...] * x_ref[...], axis=0, keepdims=True) + b_ref[...]
    if silu:
        y = y * jax.nn.sigmoid(y)
    o_ref[...] = y


def _vecmat(x_col, w, b, *, silu, tn):
    n_l, k, n = w.shape
    return pl.pallas_call(
        functools.partial(_vecmat_kernel, silu=silu),
        out_shape=jax.ShapeDtypeStruct((n_l, 1, n), F32),
        grid=(n_l, n // tn),
        in_specs=[pl.BlockSpec((None, k, 1), lambda l, j: (l, 0, 0)),
                  pl.BlockSpec((None, k, tn), lambda l, j: (l, 0, j)),
                  pl.BlockSpec((None, 1, tn), lambda l, j: (l, 0, j))],
        out_specs=pl.BlockSpec((None, 1, tn), lambda l, j: (l, 0, j)),
        compiler_params=_params("parallel", "parallel"),
        name="vecmat",
    )(x_col, w, b)


def _rope_table_kernel(pos_ref, freq_ref, c_ref, sa_ref, sb_ref):
    ang = pos_ref[...].astype(F32) * freq_ref[...]
    lane = lax.broadcasted_iota(jnp.int32, ang.shape, 1)
    cos = jnp.cos(ang)
    sin = jnp.sin(ang)
    c_ref[...] = jnp.where(lane < ROT_DIM, cos, 1.0)
    sa_ref[...] = jnp.where(lane < ROT_DIM // 2, -sin, 0.0)
    sb_ref[...] = jnp.where((lane >= ROT_DIM // 2) & (lane < ROT_DIM), sin, 0.0)


def _rope_tables(pos_col, freq_row, tr):
    s = pos_col.shape[0]
    spec = pl.BlockSpec((tr, LANES), lambda i: (i, 0))
    return pl.pallas_call(
        _rope_table_kernel,
        out_shape=[jax.ShapeDtypeStruct((s, LANES), F32)] * 3,
        grid=(s // tr,),
        in_specs=[pl.BlockSpec((tr, 1), lambda i: (i, 0)),
                  pl.BlockSpec((1, LANES), lambda i: (0, 0))],
        out_specs=[spec, spec, spec],
        compiler_params=_params("parallel"),
        name="rope_tables",
    )(pos_col, freq_row)


def _ada_norm(h_ref, gain_ref, sc_ref, sh_ref, u_ref):
    scale = gain_ref[...] * (1.0 + sc_ref[...])
    shift = sh_ref[...]
    for r0 in range(0, h_ref.shape[0], NORM_CHUNK):
        x = h_ref[r0:r0 + NORM_CHUNK, :]
        y = x * lax.rsqrt(jnp.mean(x * x, axis=-1, keepdims=True) + EPS)
        u_ref[r0:r0 + NORM_CHUNK, :] = (y * scale + shift).astype(BF16)


def _norm_mm_kernel(h_ref, gain_ref, sc_ref, sh_ref, w_ref, wg_ref, o_ref, og_ref, u_ref):
    @pl.when(pl.program_id(1) == 0)
    def _():
        _ada_norm(h_ref, gain_ref, sc_ref, sh_ref, u_ref)
        og_ref[...] = jnp.dot(u_ref[...], wg_ref[...], preferred_element_type=F32)

    o_ref[...] = jnp.dot(u_ref[...], w_ref[...], preferred_element_type=F32).astype(o_ref.dtype)


def _norm_matmul(h, gain, sc, sh, w, wg, *, layer, n, tm, tn, out_dtype):
    s, d = h.shape
    ng = wg.shape[2]
    row = pl.BlockSpec((1, d), lambda i, j: (0, 0))
    return pl.pallas_call(
        _norm_mm_kernel,
        out_shape=[jax.ShapeDtypeStruct((s, n), out_dtype), jax.ShapeDtypeStruct((s, ng), F32),
                   jax.ShapeDtypeStruct((s, d), BF16)],
        grid=(s // tm, n // tn),
        in_specs=[pl.BlockSpec((tm, d), lambda i, j: (i, 0)), row, row, row,
                  pl.BlockSpec((None, d, tn), lambda i, j: (layer, 0, j)),
                  pl.BlockSpec((None, d, ng), lambda i, j: (layer, 0, 0))],
        out_specs=[pl.BlockSpec((tm, tn), lambda i, j: (i, j)),
                   pl.BlockSpec((tm, ng), lambda i, j: (i, 0)),
                   pl.BlockSpec((tm, d), lambda i, j: (i, 0))],
        compiler_params=_params("parallel", "arbitrary"),
        name="norm_matmul",
    )(h, gain, sc, sh, w, wg)


def _mm_kernel(a_ref, w_ref, o_ref):
    o_ref[...] = jnp.dot(a_ref[...], w_ref[...], preferred_element_type=F32).astype(o_ref.dtype)


def _matmul(a, w, *, layer, col0, n, tm, tn, out_dtype):
    s, k = a.shape
    blk0 = col0 // tn
    assert blk0 * tn == col0
    return pl.pallas_call(
        _mm_kernel,
        out_shape=jax.ShapeDtypeStruct((s, n), out_dtype),
        grid=(s // tm, n // tn),
        in_specs=[pl.BlockSpec((tm, k), lambda i, j: (i, 0)),
                  pl.BlockSpec((None, k, tn), lambda i, j: (layer, 0, blk0 + j))],
        out_specs=pl.BlockSpec((tm, tn), lambda i, j: (i, j)),
        compiler_params=_params("parallel", "parallel"),
        name="matmul",
    )(a, w)


def _ffn_in_kernel(h_ref, gain_ref, sc_ref, sh_ref, wg_ref, wu_ref, cw_ref, cb_ref, o_ref,
                   u_ref, ext_ref, carry_ref, *, tm):
    i = pl.program_id(0)
    j = pl.program_id(1)
    halo = SUBLANES

    @pl.when(j == 0)
    def _():
        _ada_norm(h_ref, gain_ref, sc_ref, sh_ref, u_ref)

    @pl.when(i == 0)
    def _():
        ext_ref[0:halo, :] = jnp.zeros((halo, ext_ref.shape[1]), F32)

    @pl.when(i > 0)
    def _():
        ext_ref[0:halo, :] = carry_ref[j]

    ext_ref[halo:halo + tm, :] = jnp.dot(u_ref[...], wg_ref[...], preferred_element_type=F32)
    carry_ref[j] = ext_ref[tm:tm + halo, :]
    y = cb_ref[...] + ext_ref[pl.ds(halo - (FFN_CONV - 1), tm), :] * cw_ref[0:1, :]
    for t in range(1, FFN_CONV):
        y = y + ext_ref[pl.ds(halo - (FFN_CONV - 1) + t, tm), :] * cw_ref[t:t + 1, :]
    up = jnp.dot(u_ref[...], wu_ref[...], preferred_element_type=F32)
    o_ref[...] = (y * jax.nn.sigmoid(y) * up).astype(o_ref.dtype)


def _ffn_in(h, gain, sc, sh, w, cw, cb, *, layer, tm, tn):
    s, d = h.shape
    n_j = D_FF // tn
    row = pl.BlockSpec((1, d), lambda i, j: (0, 0))
    return pl.pallas_call(
        functools.partial(_ffn_in_kernel, tm=tm),
        out_shape=jax.ShapeDtypeStruct((s, D_FF), BF16),
        grid=(s // tm, n_j),
        in_specs=[pl.BlockSpec((tm, d), lambda i, j: (i, 0)), row, row, row,
                  pl.BlockSpec((None, d, tn), lambda i, j: (layer, 0, j)),
                  pl.BlockSpec((None, d, tn), lambda i, j: (layer, 0, n_j + j)),
                  pl.BlockSpec((FFN_CONV, tn), lambda i, j: (0, j)),
                  pl.BlockSpec((1, tn), lambda i, j: (0, j))],
        out_specs=pl.BlockSpec((tm, tn), lambda i, j: (i, j)),
        scratch_shapes=[pltpu.VMEM((tm, d), BF16),
                        pltpu.VMEM((tm + SUBLANES, tn), F32),
                        pltpu.VMEM((n_j, SUBLANES, tn), F32)],
        compiler_params=_params("arbitrary", "arbitrary"),
        name="ffn_in",
    )(h, gain, sc, sh, w, w, cw, cb)


def _mm_res_kernel(a_ref, w_ref, h_ref, g_ref, o_ref):
    y = jnp.dot(a_ref[...], w_ref[...], preferred_element_type=F32)
    o_ref[...] = h_ref[...] + g_ref[...] * y


def _matmul_residual(a, w, h, g, *, layer, tm, tn):
    s, k = a.shape
    n = w.shape[2]
    return pl.pallas_call(
        _mm_res_kernel,
        out_shape=jax.ShapeDtypeStruct((s, n), F32),
        grid=(s // tm, n // tn),
        in_specs=[pl.BlockSpec((tm, k), lambda i, j: (i, 0)),
                  pl.BlockSpec((None, k, tn), lambda i, j: (layer, 0, j)),
                  pl.BlockSpec((tm, tn), lambda i, j: (i, j)),
                  pl.BlockSpec((1, tn), lambda i, j: (0, j))],
        out_specs=pl.BlockSpec((tm, tn), lambda i, j: (i, j)),
        compiler_params=_params("parallel", "parallel"),
        name="matmul_residual",
    )(a, w, h, g)


def _merge_kernel(attn_ref, rnn_ref, wa_ref, wr_ref, ga_ref, gas_ref, gr_ref, grs_ref, tail_ref, o_ref):
    ya = jnp.dot(attn_ref[...], wa_ref[...], preferred_element_type=F32)
    yr = jnp.dot(rnn_ref[...], wr_ref[...], preferred_element_type=F32)
    g_attn = _unshift(ga_ref[...], gas_ref[...])
    is_last = pl.program_id(1) == pl.num_programs(1) - 1
    g_rnn = _unshift(gr_ref[...], jnp.where(is_last, tail_ref[...], grs_ref[...]))
    o_ref[...] = (jax.nn.sigmoid(g_attn) * ya + jax.nn.sigmoid(g_rnn) * yr).astype(o_ref.dtype)


def _merge(attn, rnn, wa, wr, proj, tail, *, layer, tm, tn):
    s, k = attn.shape
    n = wa.shape[2]
    ga_blk = COL_GA // tn
    gr_blk = COL_GR // tn
    per = tn // LANES
    last_spill = proj.shape[1] // LANES - 1
    return pl.pallas_call(
        _merge_kernel,
        out_shape=jax.ShapeDtypeStruct((s, n), BF16),
        grid=(s // tm, n // tn),
        in_specs=[pl.BlockSpec((tm, k), lambda i, j: (i, 0)),
                  pl.BlockSpec((tm, k), lambda i, j: (i, 0)),
                  pl.BlockSpec((None, k, tn), lambda i, j: (layer, 0, j)),
                  pl.BlockSpec((None, k, tn), lambda i, j: (layer, 0, j)),
                  pl.BlockSpec((tm, tn), lambda i, j: (i, ga_blk + j)),
                  pl.BlockSpec((tm, LANES), lambda i, j: (i, (ga_blk + j + 1) * per)),
                  pl.BlockSpec((tm, tn), lambda i, j: (i, gr_blk + j)),
                  pl.BlockSpec((tm, LANES), lambda i, j: (i, jnp.minimum((gr_blk + j + 1) * per, last_spill))),
                  pl.BlockSpec((tm, LANES), lambda i, j: (i, 0))],
        out_specs=pl.BlockSpec((tm, tn), lambda i, j: (i, j)),
        compiler_params=_params("parallel", "parallel"),
        name="merge",
    )(attn, rnn, wa, wr, proj, proj, proj, proj, tail)


def _head_norm(x, gain):
    return x * lax.rsqrt(jnp.mean(x * x, axis=-1, keepdims=True) + EPS) * gain


def _rope(x, c, sa, sb):
    return (x * c + pltpu.roll(x, LANES - ROT_DIM // 2, axis=1) * sa
            + pltpu.roll(x, ROT_DIM // 2, axis=1) * sb)


def _prep_kernel(p_ref, graw_ref, c_ref, sa_ref, sb_ref, qn_ref, kn_ref, blk_ref,
                 q_ref, kc_ref, ks_ref, vs_ref, kw_ref, vw_ref, g_ref):
    c, sa, sb = c_ref[...], sa_ref[...], sb_ref[...]
    qn = qn_ref[...]
    for hd in range(N_Q_HEADS):
        cols = slice(hd * HEAD_DIM, (hd + 1) * HEAD_DIM)
        q_ref[:, cols] = (_rope(_head_norm(p_ref[:, cols], qn), c, sa, sb) * Q_SCALE).astype(BF16)
    for g in range(N_KV_GROUPS):
        cols = slice(g * HEAD_DIM, (g + 1) * HEAD_DIM)

        def src(base):
            return p_ref[:, base + g * HEAD_DIM:base + (g + 1) * HEAD_DIM]

        kc_ref[:, cols] = _rope(src(COL_KC), c, sa, sb)
        ks_ref[:, 2 * g * HEAD_DIM:(2 * g + 1) * HEAD_DIM] = _rope(
            _head_norm(src(COL_KS), kn_ref[1:2, :]), c, sa, sb).astype(BF16)
        ks_ref[:, (2 * g + 1) * HEAD_DIM:(2 * g + 2) * HEAD_DIM] = blk_ref[...]
        kw_ref[:, cols] = _rope(_head_norm(src(COL_KW), kn_ref[2:3, :]), c, sa, sb).astype(BF16)
        vs_ref[:, cols] = src(COL_VS).astype(BF16)
        vw_ref[:, cols] = src(COL_VW).astype(BF16)
    sig = jax.nn.sigmoid(graw_ref[...])
    per = Q_PER_KV * N_NSA_BRANCHES
    for g in range(N_KV_GROUPS):
        g_ref[:, g * LANES:(g + 1) * LANES] = sig if g == 0 else pltpu.roll(sig, LANES - g * per, axis=1)


def _prep(proj, graw, c, sa, sb, qn, kn, blk_onehot, *, tr):
    s = proj.shape[0]
    ng = N_KV_GROUPS * LANES
    tab = pl.BlockSpec((tr, LANES), lambda i: (i, 0))
    kv_spec = pl.BlockSpec((tr, KV_WIDTH), lambda i: (i, 0))
    return pl.pallas_call(
        _prep_kernel,
        out_shape=[jax.ShapeDtypeStruct((s, ATTN_WIDTH), BF16),
                   jax.ShapeDtypeStruct((s, KV_WIDTH), F32),
                   jax.ShapeDtypeStruct((s, 2 * KV_WIDTH), BF16),
                   jax.ShapeDtypeStruct((s, KV_WIDTH), BF16),
                   jax.ShapeDtypeStruct((s, KV_WIDTH), BF16),
                   jax.ShapeDtypeStruct((s, KV_WIDTH), BF16),
                   jax.ShapeDtypeStruct((s, ng), F32)],
        grid=(s // tr,),
        in_specs=[pl.BlockSpec((tr, N_PROJ_ATTN), lambda i: (i, 0)),
                  tab, tab, tab, tab,
                  pl.BlockSpec((1, HEAD_DIM), lambda i: (0, 0)),
                  pl.BlockSpec((N_NSA_BRANCHES, HEAD_DIM), lambda i: (0, 0)),
                  tab],
        out_specs=[pl.BlockSpec((tr, ATTN_WIDTH), lambda i: (i, 0)),
                   kv_spec, pl.BlockSpec((tr, 2 * KV_WIDTH), lambda i: (i, 0)),
                   kv_spec, kv_spec, kv_spec,
                   pl.BlockSpec((tr, ng), lambda i: (i, 0))],
        compiler_params=_params("parallel"),
        name="qk_prep",
    )(proj, graw, c, sa, sb, qn, kn, blk_onehot)


def _compress_kernel(x_ref, pe_ref, w_ref, gain_ref, o_ref, shift_ref, *, do_norm, n_chunks):
    acc_a = jnp.zeros((n_chunks, HEAD_DIM), F32)
    acc_b = jnp.zeros((n_chunks, HEAD_DIM), F32)
    for l in range(CMP_STRIDE):
        xl = x_ref[pl.ds(l, n_chunks, stride=CMP_STRIDE), :]
        xa = (xl + pe_ref[l:l + 1, :]).astype(BF16)
        xb = (xl + pe_ref[CMP_STRIDE + l:CMP_STRIDE + l + 1, :]).astype(BF16)
        acc_a = acc_a + jnp.dot(xa, w_ref[l], preferred_element_type=F32)
        acc_b = acc_b + jnp.dot(xb, w_ref[CMP_STRIDE + l], preferred_element_type=F32)
    shift_ref[0:n_chunks, :] = acc_b
    shift_ref[n_chunks:n_chunks + SUBLANES, :] = jnp.zeros((SUBLANES, HEAD_DIM), F32)
    out = acc_a + shift_ref[1:n_chunks + 1, :]
    if do_norm:
        out = _head_norm(out, gain_ref[...])
    o_ref[...] = out.astype(o_ref.dtype)


def _compress(x, col_blk0, pe, w, gain, *, do_norm):
    s = x.shape[0]
    n_chunks = s // CMP_STRIDE
    return pl.pallas_call(
        functools.partial(_compress_kernel, do_norm=do_norm, n_chunks=n_chunks),
        out_shape=jax.ShapeDtypeStruct((N_KV_GROUPS, n_chunks, HEAD_DIM), BF16),
        grid=(N_KV_GROUPS,),
        in_specs=[pl.BlockSpec((s, HEAD_DIM), lambda g: (0, col_blk0 + g)),
                  pl.BlockSpec((CMP_LEN, HEAD_DIM), lambda g: (0, 0)),
                  pl.BlockSpec((CMP_LEN, HEAD_DIM, HEAD_DIM), lambda g: (0, 0, 0)),
                  pl.BlockSpec((1, HEAD_DIM), lambda g: (0, 0))],
        out_specs=pl.BlockSpec((None, n_chunks, HEAD_DIM), lambda g: (g, 0, 0)),
        scratch_shapes=[pltpu.VMEM((n_chunks + SUBLANES, HEAD_DIM), F32)],
        compiler_params=_params("parallel"),
        name="compress",
    )(x, pe, w, gain)


def _dot_nt(a, b):
    return lax.dot_general(a, b, (((1,), (1,)), ((), ())), preferred_element_type=F32)


def _biased_softmax2(s, bias):
    s = s + bias
    p = jnp.exp2(s - jnp.max(s, axis=-1, keepdims=True))
    return p, jnp.sum(p, axis=-1, keepdims=True)


def _stack_heads(q_ref):
    return jnp.concatenate([q_ref[:, z * HEAD_DIM:(z + 1) * HEAD_DIM] for z in range(Q_PER_KV)], axis=0)


def _cw_attn_kernel(*refs, n_chunks):
    q_ref, kc_ref, vc_ref, ovt_ref = refs[:4]
    kw_refs = refs[4:4 + N_WIN_BLK]
    vw_refs = refs[4 + N_WIN_BLK:4 + 2 * N_WIN_BLK]
    g_ref, ocw_ref, sel_ref, score_ref, cnt_ref = refs[4 + 2 * N_WIN_BLK:]
    qb = pl.program_id(1)
    t0 = qb * Q_TILE
    q4 = _stack_heads(q_ref)
    head_rows = [slice(z * Q_TILE, (z + 1) * Q_TILE) for z in range(Q_PER_KV)]

    tq = t0 + lax.broadcasted_iota(jnp.int32, (Q_TILE, n_chunks), 0)
    n_id = lax.broadcasted_iota(jnp.int32, (Q_TILE, n_chunks), 1)
    bias_c = jnp.where((n_id * CMP_STRIDE + (CMP_LEN - 1) <= tq) & (n_id < n_chunks - 1), 0.0, NEG_INF)
    row_ok = jnp.where(t0 + lax.broadcasted_iota(jnp.int32, (Q_TILE, 1), 0) >= CMP_LEN - 1, 1.0, 0.0)
    s_c = _dot_nt(q4, kc_ref[...])
    p_heads = []
    for r in head_rows:
        p, l = _biased_softmax2(s_c[r], bias_c)
        p_heads.append(p * (row_ok / jnp.maximum(l, 1e-30)))
    o_c = jnp.dot(jnp.concatenate(p_heads, axis=0).astype(BF16), vc_ref[...],
                  preferred_element_type=F32)

    p_sum = p_heads[0]
    for z in range(1, Q_PER_KV):
        p_sum = p_sum + p_heads[z]
    p_hi = p_sum.astype(BF16)
    p_lo = (p_sum - p_hi.astype(F32)).astype(BF16)
    ovt = ovt_ref[...]
    imp_t = _dot_nt(ovt, p_hi) + _dot_nt(ovt, p_lo)
    j_id = lax.broadcasted_iota(jnp.int32, (LANES, Q_TILE), 0)
    cur = (t0 + lax.broadcasted_iota(jnp.int32, (LANES, Q_TILE), 1)) // SEL_BLOCK
    valid = j_id <= cur
    forced = (j_id == 0) | (valid & (j_id > cur - N_LOCAL_SEL))
    score_ref[...] = jnp.where(forced, FORCE_SCORE, jnp.where(valid, imp_t, -1.0))
    cnt_ref[...] = jnp.zeros(cnt_ref.shape, F32)

    n_keys = N_WIN_BLK * Q_TILE
    tq_w = t0 + lax.broadcasted_iota(jnp.int32, (Q_TILE, n_keys), 0)
    pos = t0 - WINDOW + lax.broadcasted_iota(jnp.int32, (Q_TILE, n_keys), 1)
    bias_w = jnp.where((pos <= tq_w) & (pos > tq_w - WINDOW) & (pos >= 0), 0.0, NEG_INF)
    s_w = jnp.concatenate([_dot_nt(q4, kw_refs[i][...]) for i in range(N_WIN_BLK)], axis=1)
    pw_heads = []
    for r in head_rows:
        p, l = _biased_softmax2(s_w[r], bias_w)
        pw_heads.append((p * (1.0 / jnp.maximum(l, 1e-30))).astype(BF16))
    p_w = jnp.concatenate(pw_heads, axis=0)
    o_w = jnp.dot(p_w[:, 0:Q_TILE], vw_refs[0][...], preferred_element_type=F32)
    for i in range(1, N_WIN_BLK):
        o_w = o_w + jnp.dot(p_w[:, i * Q_TILE:(i + 1) * Q_TILE], vw_refs[i][...],
                            preferred_element_type=F32)

    gates = g_ref[...]
    for z, r in enumerate(head_rows):
        g_c = gates[:, z * N_NSA_BRANCHES:z * N_NSA_BRANCHES + 1]
        g_w = gates[:, z * N_NSA_BRANCHES + 2:z * N_NSA_BRANCHES + 3]
        ocw_ref[:, z * HEAD_DIM:(z + 1) * HEAD_DIM] = g_c * o_c[r] + g_w * o_w[r]

    n_grp = LANES // SUBLANES
    last_src_grp = ((t0 + Q_TILE - 1) // SEL_BLOCK) // SUBLANES
    sub = lax.broadcasted_iota(jnp.int32, (SUBLANES, Q_TILE), 0)
    for gj in range(n_grp):
        @pl.when(gj <= last_src_grp)
        def _():
            src = score_ref[gj * SUBLANES:(gj + 1) * SUBLANES, :]
            rows_b = [jnp.broadcast_to(src[r:r + 1, :], (SUBLANES, Q_TILE)) for r in range(SUBLANES)]
            for gi in range(n_grp):
                tgt = score_ref[gi * SUBLANES:(gi + 1) * SUBLANES, :]
                acc = cnt_ref[gi * SUBLANES:(gi + 1) * SUBLANES, :]
                for r in range(SUBLANES):
                    if gi < gj:
                        inc = jnp.where(rows_b[r] > tgt, 1.0, 0.0)
                    elif gi > gj:
                        inc = jnp.where(rows_b[r] >= tgt, 1.0, 0.0)
                    else:
                        inc = jnp.where(sub > r, jnp.where(rows_b[r] >= tgt, 1.0, 0.0),
                                        jnp.where(rows_b[r] > tgt, 1.0, 0.0))
                    acc = acc + inc
                cnt_ref[gi * SUBLANES:(gi + 1) * SUBLANES, :] = acc
    bias_t = jnp.where(cnt_ref[...] < float(SEL_TOP), 0.0, NEG_INF)
    sel_ref[...] = bias_t.T.astype(sel_ref.dtype)


def _cw_attention(q, k_cmp, v_cmp, ovt, kw, vw, gates):
    s = q.shape[0]
    n_chunks = k_cmp.shape[1]
    n_qb = s // Q_TILE
    grp_w = Q_PER_KV * HEAD_DIM

    def win_spec(i):
        return pl.BlockSpec((Q_TILE, HEAD_DIM),
                            lambda g, qb: (jnp.maximum(qb - (N_WIN_BLK - 1) + i, 0), g))

    cmp_spec = pl.BlockSpec((None, n_chunks, HEAD_DIM), lambda g, qb: (g, 0, 0))
    in_specs = ([pl.BlockSpec((Q_TILE, grp_w), lambda g, qb: (qb, g)), cmp_spec, cmp_spec,
                 pl.BlockSpec((LANES, n_chunks), lambda g, qb: (0, 0))]
                + [win_spec(i) for i in range(N_WIN_BLK)] * 2
                + [pl.BlockSpec((Q_TILE, LANES), lambda g, qb: (qb, g))])
    return pl.pallas_call(
        functools.partial(_cw_attn_kernel, n_chunks=n_chunks),
        out_shape=[jax.ShapeDtypeStruct((s, ATTN_WIDTH), F32),
                   jax.ShapeDtypeStruct((N_KV_GROUPS, s, LANES), BF16)],
        grid=(N_KV_GROUPS, n_qb),
        in_specs=in_specs,
        out_specs=[pl.BlockSpec((Q_TILE, grp_w), lambda g, qb: (qb, g)),
                   pl.BlockSpec((None, Q_TILE, LANES), lambda g, qb: (g, qb, 0))],
        scratch_shapes=[pltpu.VMEM((LANES, Q_TILE), F32), pltpu.VMEM((LANES, Q_TILE), F32)],
        compiler_params=_params("parallel", "parallel"),
        name="cmp_win_attention",
    )(q, k_cmp, v_cmp, ovt, *([kw] * N_WIN_BLK), *([vw] * N_WIN_BLK), gates)


def _sel_attn_kernel(qb_ref, kt_ref, q_ref, sb_ref, k_ref, vprev_ref, vlast_ref, ocw_ref, g_ref, o_ref,
                     qx_ref, m_ref, l_ref, acc_ref, p_ref, alpha_ref):
    step = pl.program_id(1)
    qb = qb_ref[step]
    kt = kt_ref[step]
    last_kt = (qb * SEL_Q_TILE) // K_TILE
    n_lane_blk = K_TILE // LANES
    slot = kt % 2
    head_rows = [slice(z * SEL_Q_TILE, (z + 1) * SEL_Q_TILE) for z in range(Q_PER_KV)]

    @pl.when(kt == 0)
    def _():
        for z, r in enumerate(head_rows):
            qx_ref[r, 0:HEAD_DIM] = q_ref[:, z * HEAD_DIM:(z + 1) * HEAD_DIM]
            qx_ref[r, HEAD_DIM:2 * HEAD_DIM] = sb_ref[...]
        m_ref[...] = jnp.full(m_ref.shape, NEG_INF, F32)
        l_ref[...] = jnp.zeros(l_ref.shape, F32)
        acc_ref[...] = jnp.zeros(acc_ref.shape, F32)
        p_ref[1] = jnp.zeros(p_ref.shape[1:], BF16)
        alpha_ref[1] = jnp.ones(alpha_ref.shape[1:], F32)

    def apply_pv(src_slot, v_ref):
        v = v_ref[...]
        for r in head_rows:
            acc_ref[r, :] = alpha_ref[src_slot, r, :] * acc_ref[r, :] + jnp.dot(
                p_ref[src_slot, r, :], v, preferred_element_type=F32)

    def update(causal):
        apply_pv(1 - slot, vprev_ref)
        k = k_ref[...]
        if causal:
            tq = qb * SEL_Q_TILE + lax.broadcasted_iota(jnp.int32, (SEL_Q_TILE, K_TILE), 0)
            kpos = kt * K_TILE + lax.broadcasted_iota(jnp.int32, (SEL_Q_TILE, K_TILE), 1)
            cbias = jnp.where(kpos <= tq, 0.0, NEG_INF)
        for r in head_rows:
            s = _dot_nt(qx_ref[r, :], k)
            if causal:
                s = s + cbias
            blk = [s[:, c * LANES:(c + 1) * LANES] for c in range(n_lane_blk)]
            mx = blk[0]
            for c in range(1, n_lane_blk):
                mx = jnp.maximum(mx, blk[c])
            m_old = m_ref[r, :]
            m_new = jnp.maximum(m_old, jnp.max(mx, axis=-1, keepdims=True))
            alpha = jnp.exp2(m_old - m_new)
            ps = [jnp.exp2(b - m_new) for b in blk]
            l_add = ps[0]
            for c in range(1, n_lane_blk):
                l_add = l_add + ps[c]
            l_ref[r, :] = alpha * l_ref[r, :] + l_add
            m_ref[r, :] = m_new
            alpha_ref[slot, r, :] = alpha
            p_ref[slot, r, :] = jnp.concatenate([x.astype(BF16) for x in ps], axis=1)

    @pl.when(kt < last_kt)
    def _():
        update(False)

    @pl.when(kt == last_kt)
    def _():
        update(True)
        apply_pv(slot, vlast_ref)
        gates = g_ref[...]
        for z, r in enumerate(head_rows):
            cols = slice(z * HEAD_DIM, (z + 1) * HEAD_DIM)
            g_s = gates[:, z * N_NSA_BRANCHES + 1:z * N_NSA_BRANCHES + 2]
            l_row = jnp.sum(l_ref[r, :], axis=-1, keepdims=True)
            o_s = acc_ref[r, :] / jnp.maximum(l_row, 1e-30)
            o_ref[:, cols] = (ocw_ref[:, cols] + g_s * o_s).astype(o_ref.dtype)


def _sel_attention(q, sel_bias, ks_ext, vs, ocw, gates):
    s = q.shape[0]
    n_qb = s // SEL_Q_TILE
    grp_w = Q_PER_KV * HEAD_DIM
    qb_of, kt_of = [], []
    for qb in range(n_qb):
        for kt in range((qb * SEL_Q_TILE) // K_TILE + 1):
            qb_of.append(qb)
            kt_of.append(kt)
    qb_arr = jnp.asarray(np.asarray(qb_of, np.int32))
    kt_arr = jnp.asarray(np.asarray(kt_of, np.int32))
    grid_spec = pltpu.PrefetchScalarGridSpec(
        num_scalar_prefetch=2,
        grid=(N_KV_GROUPS, len(qb_of)),
        in_specs=[pl.BlockSpec((SEL_Q_TILE, grp_w), lambda g, i, qbr, ktr: (qbr[i], g)),
                  pl.BlockSpec((None, SEL_Q_TILE, LANES), lambda g, i, qbr, ktr: (g, qbr[i], 0)),
                  pl.BlockSpec((K_TILE, 2 * HEAD_DIM), lambda g, i, qbr, ktr: (ktr[i], g)),
                  pl.BlockSpec((K_TILE, HEAD_DIM), lambda g, i, qbr, ktr: (jnp.maximum(ktr[i] - 1, 0), g)),
                  pl.BlockSpec((K_TILE, HEAD_DIM),
                               lambda g, i, qbr, ktr: ((qbr[i] * SEL_Q_TILE) // K_TILE, g)),
                  pl.BlockSpec((SEL_Q_TILE, grp_w), lambda g, i, qbr, ktr: (qbr[i], g)),
                  pl.BlockSpec((SEL_Q_TILE, LANES), lambda g, i, qbr, ktr: (qbr[i], g))],
        out_specs=pl.BlockSpec((SEL_Q_TILE, grp_w), lambda g, i, qbr, ktr: (qbr[i], g)),
        scratch_shapes=[pltpu.VMEM((SEL_ROWS, 2 * HEAD_DIM), BF16),
                        pltpu.VMEM((SEL_ROWS, LANES), F32),
                        pltpu.VMEM((SEL_ROWS, LANES), F32),
                        pltpu.VMEM((SEL_ROWS, HEAD_DIM), F32),
                        pltpu.VMEM((2, SEL_ROWS, K_TILE), BF16),
                        pltpu.VMEM((2, SEL_ROWS, LANES), F32)],
    )
    return pl.pallas_call(
        _sel_attn_kernel,
        out_shape=jax.ShapeDtypeStruct((s, ATTN_WIDTH), BF16),
        grid_spec=grid_spec,
        compiler_params=_params("parallel", "arbitrary"),
        name="sel_attention",
    )(qb_arr, kt_arr, q, sel_bias, ks_ext, vs, vs, ocw, gates)


def _rnn_kernel(rx_ref, rxs_ref, ry_ref, rys_ref, cw_ref, cb_ref, wa_ref, ba_ref, wx_ref, bx_ref, lam_ref,
                o_ref, ext_ref, a_ref, b_ref, h_ref, *, tt, tc):
    ti = pl.program_id(1)
    n_blk = tc // RNN_BLOCK_DIM
    halo = SUBLANES

    @pl.when(ti == 0)
    def _():
        ext_ref[0:halo, :] = jnp.zeros((halo, tc), F32)
        h_ref[...] = jnp.zeros(h_ref.shape, F32)

    @pl.when(ti > 0)
    def _():
        ext_ref[0:halo, :] = ext_ref[tt:tt + halo, :]

    ext_ref[halo:halo + tt, :] = _unshift(rx_ref[...], rxs_ref[...])
    xr = cb_ref[...] + ext_ref[pl.ds(halo - (RNN_CONV - 1), tt), :] * cw_ref[0:1, :]
    for j in range(1, RNN_CONV):
        xr = xr + ext_ref[pl.ds(halo - (RNN_CONV - 1) + j, tt), :] * cw_ref[j:j + 1, :]

    sp = jnp.maximum(-lam_ref[...], 0.0) + jnp.log(1.0 + jnp.exp(-jnp.abs(lam_ref[...])))
    neg_c_sp = -RG_C * sp
    xb = xr.astype(BF16)
    for blk in range(n_blk):
        cols = slice(blk * RNN_BLOCK_DIM, (blk + 1) * RNN_BLOCK_DIM)
        xs = xb[:, cols]
        r = jax.nn.sigmoid(jnp.dot(xs, wa_ref[blk], preferred_element_type=F32) + ba_ref[:, cols])
        i = jax.nn.sigmoid(jnp.dot(xs, wx_ref[blk], preferred_element_type=F32) + bx_ref[:, cols])
        a = jnp.exp(r * neg_c_sp[:, cols])
        one_m = 1.0 - a * a
        root = jnp.where(one_m > 0.0, one_m * lax.rsqrt(one_m), 0.0)
        a_ref[:, cols] = a
        b_ref[:, cols] = root * (i * xr[:, cols])

    row = lax.broadcasted_iota(jnp.int32, (SUBLANES, tc), 0)

    def scan_rows(i, carry):
        r0 = pl.multiple_of(i * SUBLANES, SUBLANES)
        a8 = a_ref[pl.ds(r0, SUBLANES), :]
        b8 = b_ref[pl.ds(r0, SUBLANES), :]
        for d in (1, 2, 4):
            keep = row >= d
            a_sh = pltpu.roll(a8, d, axis=0)
            b_sh = pltpu.roll(b8, d, axis=0)
            b8 = jnp.where(keep, a8 * b_sh + b8, b8)
            a8 = jnp.where(keep, a8 * a_sh, a8)
        h8 = a8 * carry + b8
        b_ref[pl.ds(r0, SUBLANES), :] = h8
        return jnp.broadcast_to(h8[SUBLANES - 1:SUBLANES, :], (SUBLANES, tc))

    h_ref[...] = lax.fori_loop(0, tt // SUBLANES, scan_rows, h_ref[...])
    ry = _unshift(ry_ref[...], rys_ref[...])
    o_ref[...] = (b_ref[...] * jax.nn.gelu(ry, approximate=True)).astype(o_ref.dtype)


def _rnn_branch(proj, cw, cb, wa, ba, wx, bx, lam, *, tt, tc):
    s = proj.shape[0]
    n_cb = RNN_WIDTH // tc
    blk_per = tc // RNN_BLOCK_DIM
    rx_blk = COL_RX // tc
    ry_blk = COL_RY // tc
    per = tc // LANES
    vec = pl.BlockSpec((1, tc), lambda c, t: (0, c))
    wspec = pl.BlockSpec((blk_per, RNN_BLOCK_DIM, RNN_BLOCK_DIM), lambda c, t: (c, 0, 0))
    return pl.pallas_call(
        functools.partial(_rnn_kernel, tt=tt, tc=tc),
        out_shape=jax.ShapeDtypeStruct((s, RNN_WIDTH), BF16),
        grid=(n_cb, s // tt),
        in_specs=[pl.BlockSpec((tt, tc), lambda c, t: (t, rx_blk + c)),
                  pl.BlockSpec((tt, LANES), lambda c, t: (t, (rx_blk + c + 1) * per)),
                  pl.BlockSpec((tt, tc), lambda c, t: (t, ry_blk + c)),
                  pl.BlockSpec((tt, LANES), lambda c, t: (t, (ry_blk + c + 1) * per)),
                  pl.BlockSpec((RNN_CONV, tc), lambda c, t: (0, c)),
                  vec, wspec, vec, wspec, vec, vec],
        out_specs=pl.BlockSpec((tt, tc), lambda c, t: (t, c)),
        scratch_shapes=[pltpu.VMEM((tt + SUBLANES, tc), F32),
                        pltpu.VMEM((tt, tc), F32),
                        pltpu.VMEM((tt, tc), F32),
                        pltpu.VMEM((SUBLANES, tc), F32)],
        compiler_params=_params("parallel", "arbitrary"),
        name="rg_lru",
    )(proj, proj, proj, proj, cw, cb, wa, ba, wx, bx, lam)


def _overlap_t(s):
    n_chunks = s // CMP_STRIDE
    n_cmp = n_chunks - 1
    n_sel = s // SEL_BLOCK
    cmp_start = np.arange(n_cmp) * CMP_STRIDE
    sel_start = np.arange(n_sel) * SEL_BLOCK
    ov = np.clip(np.minimum(cmp_start[:, None] + CMP_LEN, sel_start[None, :] + SEL_BLOCK)
                 - np.maximum(cmp_start[:, None], sel_start[None, :]), 0, None) / CMP_LEN
    out = np.zeros((LANES, n_chunks), np.float32)
    out[:n_sel, :n_cmp] = ov.T
    return out


def _block_onehot(s):
    return (np.arange(s)[:, None] // SEL_BLOCK == np.arange(LANES)[None, :]).astype(np.float32)


def kernel(x, c, positions, w_cond, b_cond, w_mod, b_mod, norm_mix, norm_ffn, w_in, q_norm, k_norm, cmp_pe_k, cmp_w_k, cmp_pe_v, cmp_w_v, rnn_conv_w, rnn_conv_b, rg_w_a, rg_b_a, rg_w_x, rg_b_x, rg_lambda, w_attn_up, w_rnn_up, w_out, w_ffn_in, ffn_conv_w, ffn_conv_b, w_ffn_down):
    b, s, d = x.shape
    depth = w_in.shape[0]
    assert b == 1 and d == D_MODEL and s % 2048 == 0 and s // SEL_BLOCK <= LANES

    c_emb = _vecmat(c.reshape(1, d, 1), w_cond[None], b_cond.reshape(1, 1, -1), silu=True, tn=w_cond.shape[1])
    c_col = jnp.broadcast_to(c_emb.reshape(1, -1, 1), (depth, c_emb.shape[-1], 1))
    mod = _vecmat(c_col, w_mod, b_mod[:, None, :], silu=False, tn=2048)

    inv_freq = ROPE_THETA ** (-jnp.arange(0, ROT_DIM, 2, dtype=jnp.float32) / ROT_DIM)
    freq_row = jnp.concatenate([inv_freq, inv_freq, jnp.zeros((LANES - ROT_DIM,), F32)])[None, :]
    rope_c, rope_sa, rope_sb = _rope_tables(positions.reshape(s, 1), freq_row, tr=512)

    ovt = jnp.asarray(_overlap_t(s), BF16)
    blk_onehot = jnp.asarray(_block_onehot(s), BF16)

    w_in_b = w_in.astype(BF16)
    n_tail = w_in.shape[2] - (GATE_SRC + N_PROJ_REST)
    assert n_tail == PROJ_SHIFT
    w_tail = jnp.pad(w_in[:, :, GATE_SRC + N_PROJ_REST:], ((0, 0), (0, 0), (0, LANES - n_tail))).astype(BF16)
    w_attn_up_b = w_attn_up.astype(BF16)
    w_rnn_up_b = w_rnn_up.astype(BF16)
    w_out_b = w_out.astype(BF16)
    w_ffn_in_b = w_ffn_in.astype(BF16)
    w_ffn_down_b = w_ffn_down.astype(BF16)

    h = x.reshape(s, d)
    for l in range(depth):
        sh1, sc1, g1, sh2, sc2, g2 = [mod[l, :, i * d:(i + 1) * d] for i in range(N_MOD)]
        proj_a, gr_tail, u = _norm_matmul(h, norm_mix[l][None], sc1, sh1, w_in_b, w_tail,
                                          layer=l, n=N_PROJ_ATTN, tm=512, tn=1024, out_dtype=F32)
        proj = _matmul(u, w_in_b, layer=l, col0=GATE_SRC, n=N_PROJ_REST, tm=1024, tn=512, out_dtype=F32)
        q, kc, ks_ext, vs, kw, vw, gates = _prep(proj_a, proj, rope_c, rope_sa, rope_sb,
                                                 q_norm[l][None], k_norm[l], blk_onehot, tr=256)
        k_cmp = _compress(kc, 0, cmp_pe_k[l], cmp_w_k[l].astype(BF16), k_norm[l][0:1], do_norm=True)
        v_cmp = _compress(proj_a, COL_VC // HEAD_DIM, cmp_pe_v[l], cmp_w_v[l].astype(BF16),
                          k_norm[l][0:1], do_norm=False)
        ocw, sel_bias = _cw_attention(q, k_cmp, v_cmp, ovt, kw, vw, gates)
        attn = _sel_attention(q, sel_bias, ks_ext, vs, ocw, gates)
        rnn = _rnn_branch(proj, rnn_conv_w[l], rnn_conv_b[l][None], rg_w_a[l].astype(BF16), rg_b_a[l][None],
                          rg_w_x[l].astype(BF16), rg_b_x[l][None], rg_lambda[l][None], tt=512, tc=1024)
        merged = _merge(attn, rnn, w_attn_up_b, w_rnn_up_b, proj, gr_tail, layer=l, tm=512, tn=1024)
        h = _matmul_residual(merged, w_out_b, h, g1, layer=l, tm=1024, tn=512)
        act = _ffn_in(h, norm_ffn[l][None], sc2, sh2, w_ffn_in_b, ffn_conv_w[l], ffn_conv_b[l][None],
                      layer=l, tm=512, tn=512)
        h = _matmul_residual(act, w_ffn_down_b, h, g2, layer=l, tm=1024, tn=256)
    return h.reshape(b, s, d)
```

```python
import functools

import numpy as np
import jax
import jax.numpy as jnp
from jax import lax
from jax.experimental import pallas as pl
from jax.experimental.pallas import tpu as pltpu

F32 = jnp.float32
BF16 = jnp.bfloat16

D_MODEL = 4096
N_Q_HEADS = 16
N_KV_GROUPS = 4
HEAD_DIM = 128
Q_PER_KV = N_Q_HEADS // N_KV_GROUPS
ATTN_WIDTH = N_Q_HEADS * HEAD_DIM
KV_WIDTH = N_KV_GROUPS * HEAD_DIM
N_NSA_BRANCHES = 3
ROT_DIM = HEAD_DIM // 4
ROPE_THETA = 500000.0
CMP_LEN = 32
CMP_STRIDE = 16
SEL_BLOCK = 64
SEL_TOP = 16
N_LOCAL_SEL = 2
WINDOW = 512
FORCE_SCORE = 2.0 * Q_PER_KV + 1.0
RNN_WIDTH = 2048
RNN_BLOCKS = 16
RNN_BLOCK_DIM = RNN_WIDTH // RNN_BLOCKS
RNN_CONV = 4
RG_C = 8.0
D_FF = 2 * D_MODEL
FFN_CONV = 3
N_MOD = 6
EPS = 1e-6
NEG_INF = -1e30
ATTN_SCALE = HEAD_DIM ** -0.5
LOG2_E = 1.4426950408889634
Q_SCALE = ATTN_SCALE * LOG2_E

LANES = 128
SUBLANES = 8
VMEM_LIMIT_BYTES = 56 * 1024 * 1024

COL_Q = 0
COL_KC = ATTN_WIDTH
COL_VC = COL_KC + KV_WIDTH
COL_KS = COL_VC + KV_WIDTH
COL_VS = COL_KS + KV_WIDTH
COL_KW = COL_VS + KV_WIDTH
COL_VW = COL_KW + KV_WIDTH
N_PROJ_ATTN = COL_VW + KV_WIDTH
GATE_SRC = N_PROJ_ATTN
PROJ_SHIFT = N_Q_HEADS * N_NSA_BRANCHES
N_PROJ_REST = 2 * RNN_WIDTH + 2 * D_MODEL
N_PROJ = N_PROJ_ATTN + N_PROJ_REST
COL_RX = GATE_SRC
COL_RY = COL_RX + RNN_WIDTH
COL_GA = COL_RY + RNN_WIDTH
COL_GR = COL_GA + D_MODEL

Q_TILE = 256
K_TILE = 1024
SEL_Q_TILE = 512
SEL_ROWS = Q_PER_KV * SEL_Q_TILE
N_WIN_BLK = WINDOW // Q_TILE + 1
NORM_CHUNK = 64


def _unshift(main, spill):
    return jnp.concatenate([main, spill], axis=1)[:, PROJ_SHIFT:PROJ_SHIFT + main.shape[1]]


def _params(*sem):
    return pltpu.CompilerParams(dimension_semantics=sem, vmem_limit_bytes=VMEM_LIMIT_BYTES)


def _vecmat_kernel(x_ref, w_ref, b_ref, o_ref, *, silu):
    y = jnp.sum(w_ref[...] * x_ref[...], axis=0, keepdims=True) + b_ref[...]
    if silu:
        y = y * jax.nn.sigmoid(y)
    o_ref[...] = y


def _vecmat(x_col, w, b, *, silu, tn):
    n_l, k, n = w.shape
    return pl.pallas_call(
        functools.partial(_vecmat_kernel, silu=silu),
        out_shape=jax.ShapeDtypeStruct((n_l, 1, n), F32),
        grid=(n_l, n // tn),
        in_specs=[pl.BlockSpec((None, k, 1), lambda l, j: (l, 0, 0)),
                  pl.BlockSpec((None, k, tn), lambda l, j: (l, 0, j)),
                  pl.BlockSpec((None, 1, tn), lambda l, j: (l, 0, j))],
        out_specs=pl.BlockSpec((None, 1, tn), lambda l, j: (l, 0, j)),
        compiler_params=_params("parallel", "parallel"),
        name="vecmat",
    )(x_col, w, b)


def _rope_table_kernel(pos_ref, freq_ref, c_ref, sa_ref, sb_ref):
    ang = pos_ref[...].astype(F32) * freq_ref[...]
    lane = lax.broadcasted_iota(jnp.int32, ang.shape, 1)
    cos = jnp.cos(ang)
    sin = jnp.sin(ang)
    c_ref[...] = jnp.where(lane < ROT_DIM, cos, 1.0)
    sa_ref[...] = jnp.where(lane < ROT_DIM // 2, -sin, 0.0)
    sb_ref[...] = jnp.where((lane >= ROT_DIM // 2) & (lane < ROT_DIM), sin, 0.0)


def _rope_tables(pos_col, freq_row, tr):
    s = pos_col.shape[0]
    spec = pl.BlockSpec((tr, LANES), lambda i: (i, 0))
    return pl.pallas_call(
        _rope_table_kernel,
        out_shape=[jax.ShapeDtypeStruct((s, LANES), F32)] * 3,
        grid=(s // tr,),
        in_specs=[pl.BlockSpec((tr, 1), lambda i: (i, 0)),
                  pl.BlockSpec((1, LANES), lambda i: (0, 0))],
        out_specs=[spec, spec, spec],
        compiler_params=_params("parallel"),
        name="rope_tables",
    )(pos_col, freq_row)


def _ada_norm(h_ref, gain_ref, sc_ref, sh_ref, u_ref):
    scale = gain_ref[...] * (1.0 + sc_ref[...])
    shift = sh_ref[...]
    for r0 in range(0, h_ref.shape[0], NORM_CHUNK):
        x = h_ref[r0:r0 + NORM_CHUNK, :]
        y = x * lax.rsqrt(jnp.mean(x * x, axis=-1, keepdims=True) + EPS)
        u_ref[r0:r0 + NORM_CHUNK, :] = (y * scale + shift).astype(BF16)


def _norm_mm_kernel(h_ref, gain_ref, sc_ref, sh_ref, w_ref, wg_ref, o_ref, og_ref, u_ref):
    @pl.when(pl.program_id(1) == 0)
    def _():
        _ada_norm(h_ref, gain_ref, sc_ref, sh_ref, u_ref)
        og_ref[...] = jnp.dot(u_ref[...], wg_ref[...], preferred_element_type=F32)

    o_ref[...] = jnp.dot(u_ref[...], w_ref[...], preferred_element_type=F32).astype(o_ref.dtype)


def _norm_matmul(h, gain, sc, sh, w, wg, *, layer, n, tm, tn, out_dtype):
    s, d = h.shape
    ng = wg.shape[2]
    row = pl.BlockSpec((1, d), lambda i, j: (0, 0))
    return pl.pallas_call(
        _norm_mm_kernel,
        out_shape=[jax.ShapeDtypeStruct((s, n), out_dtype), jax.ShapeDtypeStruct((s, ng), F32)],
        grid=(s // tm, n // tn),
        in_specs=[pl.BlockSpec((tm, d), lambda i, j: (i, 0)), row, row, row,
                  pl.BlockSpec((None, d, tn), lambda i, j: (layer, 0, j)),
                  pl.BlockSpec((None, d, ng), lambda i, j: (layer, 0, 0))],
        out_specs=[pl.BlockSpec((tm, tn), lambda i, j: (i, j)),
                   pl.BlockSpec((tm, ng), lambda i, j: (i, 0))],
        scratch_shapes=[pltpu.VMEM((tm, d), BF16)],
        compiler_params=_params("parallel", "arbitrary"),
        name="norm_matmul",
    )(h, gain, sc, sh, w, wg)


def _ffn_in_kernel(h_ref, gain_ref, sc_ref, sh_ref, wg_ref, wu_ref, cw_ref, cb_ref, o_ref,
                   u_ref, ext_ref, carry_ref, *, tm):
    i = pl.program_id(0)
    j = pl.program_id(1)
    halo = SUBLANES

    @pl.when(j == 0)
    def _():
        _ada_norm(h_ref, gain_ref, sc_ref, sh_ref, u_ref)

    @pl.when(i == 0)
    def _():
        ext_ref[0:halo, :] = jnp.zeros((halo, ext_ref.shape[1]), F32)

    @pl.when(i > 0)
    def _():
        ext_ref[0:halo, :] = carry_ref[j]

    ext_ref[halo:halo + tm, :] = jnp.dot(u_ref[...], wg_ref[...], preferred_element_type=F32)
    carry_ref[j] = ext_ref[tm:tm + halo, :]
    y = cb_ref[...] + ext_ref[pl.ds(halo - (FFN_CONV - 1), tm), :] * cw_ref[0:1, :]
    for t in range(1, FFN_CONV):
        y = y + ext_ref[pl.ds(halo - (FFN_CONV - 1) + t, tm), :] * cw_ref[t:t + 1, :]
    up = jnp.dot(u_ref[...], wu_ref[...], preferred_element_type=F32)
    o_ref[...] = (y * jax.nn.sigmoid(y) * up).astype(o_ref.dtype)


def _ffn_in(h, gain, sc, sh, w, cw, cb, *, layer, tm, tn):
    s, d = h.shape
    n_j = D_FF // tn
    row = pl.BlockSpec((1, d), lambda i, j: (0, 0))
    return pl.pallas_call(
        functools.partial(_ffn_in_kernel, tm=tm),
        out_shape=jax.ShapeDtypeStruct((s, D_FF), BF16),
        grid=(s // tm, n_j),
        in_specs=[pl.BlockSpec((tm, d), lambda i, j: (i, 0)), row, row, row,
                  pl.BlockSpec((None, d, tn), lambda i, j: (layer, 0, j)),
                  pl.BlockSpec((None, d, tn), lambda i, j: (layer, 0, n_j + j)),
                  pl.BlockSpec((FFN_CONV, tn), lambda i, j: (0, j)),
                  pl.BlockSpec((1, tn), lambda i, j: (0, j))],
        out_specs=pl.BlockSpec((tm, tn), lambda i, j: (i, j)),
        scratch_shapes=[pltpu.VMEM((tm, d), BF16),
                        pltpu.VMEM((tm + SUBLANES, tn), F32),
                        pltpu.VMEM((n_j, SUBLANES, tn), F32)],
        compiler_params=_params("arbitrary", "arbitrary"),
        name="ffn_in",
    )(h, gain, sc, sh, w, w, cw, cb)


def _mm_res_kernel(a_ref, w_ref, h_ref, g_ref, o_ref):
    y = jnp.dot(a_ref[...], w_ref[...], preferred_element_type=F32)
    o_ref[...] = h_ref[...] + g_ref[...] * y


def _matmul_residual(a, w, h, g, *, layer, tm, tn):
    s, k = a.shape
    n = w.shape[2]
    return pl.pallas_call(
        _mm_res_kernel,
        out_shape=jax.ShapeDtypeStruct((s, n), F32),
        grid=(s // tm, n // tn),
        in_specs=[pl.BlockSpec((tm, k), lambda i, j: (i, 0)),
                  pl.BlockSpec((None, k, tn), lambda i, j: (layer, 0, j)),
                  pl.BlockSpec((tm, tn), lambda i, j: (i, j)),
                  pl.BlockSpec((1, tn), lambda i, j: (0, j))],
        out_specs=pl.BlockSpec((tm, tn), lambda i, j: (i, j)),
        compiler_params=_params("parallel", "parallel"),
        name="matmul_residual",
    )(a, w, h, g)


def _merge_kernel(attn_ref, rnn_ref, wa_ref, wr_ref, ga_ref, gas_ref, gr_ref, grs_ref, tail_ref, o_ref):
    ya = jnp.dot(attn_ref[...], wa_ref[...], preferred_element_type=F32)
    yr = jnp.dot(rnn_ref[...], wr_ref[...], preferred_element_type=F32)
    g_attn = _unshift(ga_ref[...], gas_ref[...])
    is_last = pl.program_id(1) == pl.num_programs(1) - 1
    g_rnn = _unshift(gr_ref[...], jnp.where(is_last, tail_ref[...], grs_ref[...]))
    o_ref[...] = (jax.nn.sigmoid(g_attn) * ya + jax.nn.sigmoid(g_rnn) * yr).astype(o_ref.dtype)


def _merge(attn, rnn, wa, wr, proj, tail, *, layer, tm, tn):
    s, k = attn.shape
    n = wa.shape[2]
    ga_blk = COL_GA // tn
    gr_blk = COL_GR // tn
    per = tn // LANES
    last_spill = proj.shape[1] // LANES - 1
    return pl.pallas_call(
        _merge_kernel,
        out_shape=jax.ShapeDtypeStruct((s, n), BF16),
        grid=(s // tm, n // tn),
        in_specs=[pl.BlockSpec((tm, k), lambda i, j: (i, 0)),
                  pl.BlockSpec((tm, k), lambda i, j: (i, 0)),
                  pl.BlockSpec((None, k, tn), lambda i, j: (layer, 0, j)),
                  pl.BlockSpec((None, k, tn), lambda i, j: (layer, 0, j)),
                  pl.BlockSpec((tm, tn), lambda i, j: (i, ga_blk + j)),
                  pl.BlockSpec((tm, LANES), lambda i, j: (i, (ga_blk + j + 1) * per)),
                  pl.BlockSpec((tm, tn), lambda i, j: (i, gr_blk + j)),
                  pl.BlockSpec((tm, LANES), lambda i, j: (i, jnp.minimum((gr_blk + j + 1) * per, last_spill))),
                  pl.BlockSpec((tm, LANES), lambda i, j: (i, 0))],
        out_specs=pl.BlockSpec((tm, tn), lambda i, j: (i, j)),
        compiler_params=_params("parallel", "parallel"),
        name="merge",
    )(attn, rnn, wa, wr, proj, proj, proj, proj, tail)


def _head_norm(x, gain):
    return x * lax.rsqrt(jnp.mean(x * x, axis=-1, keepdims=True) + EPS) * gain


def _rope(x, c, sa, sb):
    return (x * c + pltpu.roll(x, LANES - ROT_DIM // 2, axis=1) * sa
            + pltpu.roll(x, ROT_DIM // 2, axis=1) * sb)


def _prep_kernel(p_ref, graw_ref, c_ref, sa_ref, sb_ref, qn_ref, kn_ref, blk_ref,
                 q_ref, kc_ref, ks_ref, vs_ref, kw_ref, vw_ref, g_ref):
    c, sa, sb = c_ref[...], sa_ref[...], sb_ref[...]
    qn = qn_ref[...]
    for hd in range(N_Q_HEADS):
        cols = slice(hd * HEAD_DIM, (hd + 1) * HEAD_DIM)
        q_ref[:, cols] = (_rope(_head_norm(p_ref[:, cols], qn), c, sa, sb) * Q_SCALE).astype(BF16)
    for g in range(N_KV_GROUPS):
        cols = slice(g * HEAD_DIM, (g + 1) * HEAD_DIM)

        def src(base):
            return p_ref[:, base + g * HEAD_DIM:base + (g + 1) * HEAD_DIM]

        kc_ref[:, cols] = _rope(src(COL_KC), c, sa, sb)
        ks_ref[:, 2 * g * HEAD_DIM:(2 * g + 1) * HEAD_DIM] = _rope(
            _head_norm(src(COL_KS), kn_ref[1:2, :]), c, sa, sb).astype(BF16)
        ks_ref[:, (2 * g + 1) * HEAD_DIM:(2 * g + 2) * HEAD_DIM] = blk_ref[...]
        kw_ref[:, cols] = _rope(_head_norm(src(COL_KW), kn_ref[2:3, :]), c, sa, sb).astype(BF16)
        vs_ref[:, cols] = src(COL_VS).astype(BF16)
        vw_ref[:, cols] = src(COL_VW).astype(BF16)
    sig = jax.nn.sigmoid(graw_ref[...])
    per = Q_PER_KV * N_NSA_BRANCHES
    for g in range(N_KV_GROUPS):
        g_ref[:, g * LANES:(g + 1) * LANES] = sig if g == 0 else pltpu.roll(sig, LANES - g * per, axis=1)


def _prep(proj, graw, c, sa, sb, qn, kn, blk_onehot, *, tr):
    s = proj.shape[0]
    ng = N_KV_GROUPS * LANES
    tab = pl.BlockSpec((tr, LANES), lambda i: (i, 0))
    kv_spec = pl.BlockSpec((tr, KV_WIDTH), lambda i: (i, 0))
    return pl.pallas_call(
        _prep_kernel,
        out_shape=[jax.ShapeDtypeStruct((s, ATTN_WIDTH), BF16),
                   jax.ShapeDtypeStruct((s, KV_WIDTH), F32),
                   jax.ShapeDtypeStruct((s, 2 * KV_WIDTH), BF16),
                   jax.ShapeDtypeStruct((s, KV_WIDTH), BF16),
                   jax.ShapeDtypeStruct((s, KV_WIDTH), BF16),
                   jax.ShapeDtypeStruct((s, KV_WIDTH), BF16),
                   jax.ShapeDtypeStruct((s, ng), F32)],
        grid=(s // tr,),
        in_specs=[pl.BlockSpec((tr, N_PROJ_ATTN), lambda i: (i, 0)),
                  pl.BlockSpec((tr, LANES), lambda i: (i, GATE_SRC // LANES)),
                  tab, tab, tab,
                  pl.BlockSpec((1, HEAD_DIM), lambda i: (0, 0)),
                  pl.BlockSpec((N_NSA_BRANCHES, HEAD_DIM), lambda i: (0, 0)),
                  tab],
        out_specs=[pl.BlockSpec((tr, ATTN_WIDTH), lambda i: (i, 0)),
                   kv_spec, pl.BlockSpec((tr, 2 * KV_WIDTH), lambda i: (i, 0)),
                   kv_spec, kv_spec, kv_spec,
                   pl.BlockSpec((tr, ng), lambda i: (i, 0))],
        compiler_params=_params("parallel"),
        name="qk_prep",
    )(proj, graw, c, sa, sb, qn, kn, blk_onehot)


def _compress_kernel(x_ref, pe_ref, w_ref, gain_ref, o_ref, shift_ref, *, do_norm, n_chunks):
    acc_a = jnp.zeros((n_chunks, HEAD_DIM), F32)
    acc_b = jnp.zeros((n_chunks, HEAD_DIM), F32)
    for l in range(CMP_STRIDE):
        xl = x_ref[pl.ds(l, n_chunks, stride=CMP_STRIDE), :]
        xa = (xl + pe_ref[l:l + 1, :]).astype(BF16)
        xb = (xl + pe_ref[CMP_STRIDE + l:CMP_STRIDE + l + 1, :]).astype(BF16)
        acc_a = acc_a + jnp.dot(xa, w_ref[l], preferred_element_type=F32)
        acc_b = acc_b + jnp.dot(xb, w_ref[CMP_STRIDE + l], preferred_element_type=F32)
    shift_ref[0:n_chunks, :] = acc_b
    shift_ref[n_chunks:n_chunks + SUBLANES, :] = jnp.zeros((SUBLANES, HEAD_DIM), F32)
    out = acc_a + shift_ref[1:n_chunks + 1, :]
    if do_norm:
        out = _head_norm(out, gain_ref[...])
    o_ref[...] = out.astype(o_ref.dtype)


def _compress(x, col_blk0, pe, w, gain, *, do_norm):
    s = x.shape[0]
    n_chunks = s // CMP_STRIDE
    return pl.pallas_call(
        functools.partial(_compress_kernel, do_norm=do_norm, n_chunks=n_chunks),
        out_shape=jax.ShapeDtypeStruct((N_KV_GROUPS, n_chunks, HEAD_DIM), BF16),
        grid=(N_KV_GROUPS,),
        in_specs=[pl.BlockSpec((s, HEAD_DIM), lambda g: (0, col_blk0 + g)),
                  pl.BlockSpec((CMP_LEN, HEAD_DIM), lambda g: (0, 0)),
                  pl.BlockSpec((CMP_LEN, HEAD_DIM, HEAD_DIM), lambda g: (0, 0, 0)),
                  pl.BlockSpec((1, HEAD_DIM), lambda g: (0, 0))],
        out_specs=pl.BlockSpec((None, n_chunks, HEAD_DIM), lambda g: (g, 0, 0)),
        scratch_shapes=[pltpu.VMEM((n_chunks + SUBLANES, HEAD_DIM), F32)],
        compiler_params=_params("parallel"),
        name="compress",
    )(x, pe, w, gain)


def _dot_nt(a, b):
    return lax.dot_general(a, b, (((1,), (1,)), ((), ())), preferred_element_type=F32)


def _biased_softmax2(s, bias):
    s = s + bias
    p = jnp.exp2(s - jnp.max(s, axis=-1, keepdims=True))
    return p, jnp.sum(p, axis=-1, keepdims=True)


def _stack_heads(q_ref):
    return jnp.concatenate([q_ref[:, z * HEAD_DIM:(z + 1) * HEAD_DIM] for z in range(Q_PER_KV)], axis=0)


def _cw_attn_kernel(*refs, n_chunks):
    q_ref, kc_ref, vc_ref, ovt_ref = refs[:4]
    kw_refs = refs[4:4 + N_WIN_BLK]
    vw_refs = refs[4 + N_WIN_BLK:4 + 2 * N_WIN_BLK]
    g_ref, ocw_ref, sel_ref, score_ref, cnt_ref = refs[4 + 2 * N_WIN_BLK:]
    qb = pl.program_id(1)
    t0 = qb * Q_TILE
    q4 = _stack_heads(q_ref)
    head_rows = [slice(z * Q_TILE, (z + 1) * Q_TILE) for z in range(Q_PER_KV)]

    tq = t0 + lax.broadcasted_iota(jnp.int32, (Q_TILE, n_chunks), 0)
    n_id = lax.broadcasted_iota(jnp.int32, (Q_TILE, n_chunks), 1)
    bias_c = jnp.where((n_id * CMP_STRIDE + (CMP_LEN - 1) <= tq) & (n_id < n_chunks - 1), 0.0, NEG_INF)
    row_ok = jnp.where(t0 + lax.broadcasted_iota(jnp.int32, (Q_TILE, 1), 0) >= CMP_LEN - 1, 1.0, 0.0)
    s_c = _dot_nt(q4, kc_ref[...])
    p_heads = []
    for r in head_rows:
        p, l = _biased_softmax2(s_c[r], bias_c)
        p_heads.append(p * (row_ok / jnp.maximum(l, 1e-30)))
    o_c = jnp.dot(jnp.concatenate(p_heads, axis=0).astype(BF16), vc_ref[...],
                  preferred_element_type=F32)

    p_sum = p_heads[0]
    for z in range(1, Q_PER_KV):
        p_sum = p_sum + p_heads[z]
    p_hi = p_sum.astype(BF16)
    p_lo = (p_sum - p_hi.astype(F32)).astype(BF16)
    ovt = ovt_ref[...]
    imp_t = _dot_nt(ovt, p_hi) + _dot_nt(ovt, p_lo)
    j_id = lax.broadcasted_iota(jnp.int32, (LANES, Q_TILE), 0)
    cur = (t0 + lax.broadcasted_iota(jnp.int32, (LANES, Q_TILE), 1)) // SEL_BLOCK
    valid = j_id <= cur
    forced = (j_id == 0) | (valid & (j_id > cur - N_LOCAL_SEL))
    score_ref[...] = jnp.where(forced, FORCE_SCORE, jnp.where(valid, imp_t, -1.0))
    cnt_ref[...] = jnp.zeros(cnt_ref.shape, F32)

    n_keys = N_WIN_BLK * Q_TILE
    tq_w = t0 + lax.broadcasted_iota(jnp.int32, (Q_TILE, n_keys), 0)
    pos = t0 - WINDOW + lax.broadcasted_iota(jnp.int32, (Q_TILE, n_keys), 1)
    bias_w = jnp.where((pos <= tq_w) & (pos > tq_w - WINDOW) & (pos >= 0), 0.0, NEG_INF)
    s_w = jnp.concatenate([_dot_nt(q4, kw_refs[i][...]) for i in range(N_WIN_BLK)], axis=1)
    pw_heads = []
    for r in head_rows:
        p, l = _biased_softmax2(s_w[r], bias_w)
        pw_heads.append((p * (1.0 / jnp.maximum(l, 1e-30))).astype(BF16))
    p_w = jnp.concatenate(pw_heads, axis=0)
    o_w = jnp.dot(p_w[:, 0:Q_TILE], vw_refs[0][...], preferred_element_type=F32)
    for i in range(1, N_WIN_BLK):
        o_w = o_w + jnp.dot(p_w[:, i * Q_TILE:(i + 1) * Q_TILE], vw_refs[i][...],
                            preferred_element_type=F32)

    gates = g_ref[...]
    for z, r in enumerate(head_rows):
        g_c = gates[:, z * N_NSA_BRANCHES:z * N_NSA_BRANCHES + 1]
        g_w = gates[:, z * N_NSA_BRANCHES + 2:z * N_NSA_BRANCHES + 3]
        ocw_ref[:, z * HEAD_DIM:(z + 1) * HEAD_DIM] = g_c * o_c[r] + g_w * o_w[r]

    n_grp = LANES // SUBLANES
    last_src_grp = ((t0 + Q_TILE - 1) // SEL_BLOCK) // SUBLANES
    sub = lax.broadcasted_iota(jnp.int32, (SUBLANES, Q_TILE), 0)
    for gj in range(n_grp):
        @pl.when(gj <= last_src_grp)
        def _():
            src = score_ref[gj * SUBLANES:(gj + 1) * SUBLANES, :]
            rows_b = [jnp.broadcast_to(src[r:r + 1, :], (SUBLANES, Q_TILE)) for r in range(SUBLANES)]
            for gi in range(n_grp):
                tgt = score_ref[gi * SUBLANES:(gi + 1) * SUBLANES, :]
                acc = cnt_ref[gi * SUBLANES:(gi + 1) * SUBLANES, :]
                for r in range(SUBLANES):
                    if gi < gj:
                        inc = jnp.where(rows_b[r] > tgt, 1.0, 0.0)
                    elif gi > gj:
                        inc = jnp.where(rows_b[r] >= tgt, 1.0, 0.0)
                    else:
                        inc = jnp.where(sub > r, jnp.where(rows_b[r] >= tgt, 1.0, 0.0),
                                        jnp.where(rows_b[r] > tgt, 1.0, 0.0))
                    acc = acc + inc
                cnt_ref[gi * SUBLANES:(gi + 1) * SUBLANES, :] = acc
    bias_t = jnp.where(cnt_ref[...] < float(SEL_TOP), 0.0, NEG_INF)
    sel_ref[...] = bias_t.T.astype(sel_ref.dtype)


def _cw_attention(q, k_cmp, v_cmp, ovt, kw, vw, gates):
    s = q.shape[0]
    n_chunks = k_cmp.shape[1]
    n_qb = s // Q_TILE
    grp_w = Q_PER_KV * HEAD_DIM

    def win_spec(i):
        return pl.BlockSpec((Q_TILE, HEAD_DIM),
                            lambda g, qb: (jnp.maximum(qb - (N_WIN_BLK - 1) + i, 0), g))

    cmp_spec = pl.BlockSpec((None, n_chunks, HEAD_DIM), lambda g, qb: (g, 0, 0))
    in_specs = ([pl.BlockSpec((Q_TILE, grp_w), lambda g, qb: (qb, g)), cmp_spec, cmp_spec,
                 pl.BlockSpec((LANES, n_chunks), lambda g, qb: (0, 0))]
                + [win_spec(i) for i in range(N_WIN_BLK)] * 2
                + [pl.BlockSpec((Q_TILE, LANES), lambda g, qb: (qb, g))])
    return pl.pallas_call(
        functools.partial(_cw_attn_kernel, n_chunks=n_chunks),
        out_shape=[jax.ShapeDtypeStruct((s, ATTN_WIDTH), F32),
                   jax.ShapeDtypeStruct((N_KV_GROUPS, s, LANES), BF16)],
        grid=(N_KV_GROUPS, n_qb),
        in_specs=in_specs,
        out_specs=[pl.BlockSpec((Q_TILE, grp_w), lambda g, qb: (qb, g)),
                   pl.BlockSpec((None, Q_TILE, LANES), lambda g, qb: (g, qb, 0))],
        scratch_shapes=[pltpu.VMEM((LANES, Q_TILE), F32), pltpu.VMEM((LANES, Q_TILE), F32)],
        compiler_params=_params("parallel", "parallel"),
        name="cmp_win_attention",
    )(q, k_cmp, v_cmp, ovt, *([kw] * N_WIN_BLK), *([vw] * N_WIN_BLK), gates)


def _sel_attn_kernel(qb_ref, kt_ref, q_ref, sb_ref, k_ref, vprev_ref, vlast_ref, ocw_ref, g_ref, o_ref,
                     qx_ref, m_ref, l_ref, acc_ref, p_ref, alpha_ref):
    step = pl.program_id(1)
    qb = qb_ref[step]
    kt = kt_ref[step]
    last_kt = (qb * SEL_Q_TILE) // K_TILE
    n_lane_blk = K_TILE // LANES
    slot = kt % 2
    head_rows = [slice(z * SEL_Q_TILE, (z + 1) * SEL_Q_TILE) for z in range(Q_PER_KV)]

    @pl.when(kt == 0)
    def _():
        for z, r in enumerate(head_rows):
            qx_ref[r, 0:HEAD_DIM] = q_ref[:, z * HEAD_DIM:(z + 1) * HEAD_DIM]
            qx_ref[r, HEAD_DIM:2 * HEAD_DIM] = sb_ref[...]
        m_ref[...] = jnp.full(m_ref.shape, NEG_INF, F32)
        l_ref[...] = jnp.zeros(l_ref.shape, F32)
        acc_ref[...] = jnp.zeros(acc_ref.shape, F32)
        p_ref[1] = jnp.zeros(p_ref.shape[1:], BF16)
        alpha_ref[1] = jnp.ones(alpha_ref.shape[1:], F32)

    def apply_pv(src_slot, v_ref):
        v = v_ref[...]
        for r in head_rows:
            acc_ref[r, :] = alpha_ref[src_slot, r, :] * acc_ref[r, :] + jnp.dot(
                p_ref[src_slot, r, :], v, preferred_element_type=F32)

    def update(causal):
        apply_pv(1 - slot, vprev_ref)
        k = k_ref[...]
        if causal:
            tq = qb * SEL_Q_TILE + lax.broadcasted_iota(jnp.int32, (SEL_Q_TILE, K_TILE), 0)
            kpos = kt * K_TILE + lax.broadcasted_iota(jnp.int32, (SEL_Q_TILE, K_TILE), 1)
            cbias = jnp.where(kpos <= tq, 0.0, NEG_INF)
        for r in head_rows:
            s = _dot_nt(qx_ref[r, :], k)
            if causal:
                s = s + cbias
            blk = [s[:, c * LANES:(c + 1) * LANES] for c in range(n_lane_blk)]
            mx = blk[0]
            for c in range(1, n_lane_blk):
                mx = jnp.maximum(mx, blk[c])
            m_old = m_ref[r, :]
            m_new = jnp.maximum(m_old, jnp.max(mx, axis=-1, keepdims=True))
            alpha = jnp.exp2(m_old - m_new)
            ps = [jnp.exp2(b - m_new) for b in blk]
            l_add = ps[0]
            for c in range(1, n_lane_blk):
                l_add = l_add + ps[c]
            l_ref[r, :] = alpha * l_ref[r, :] + l_add
            m_ref[r, :] = m_new
            alpha_ref[slot, r, :] = alpha
            p_ref[slot, r, :] = jnp.concatenate([x.astype(BF16) for x in ps], axis=1)

    @pl.when(kt < last_kt)
    def _():
        update(False)

    @pl.when(kt == last_kt)
    def _():
        update(True)
        apply_pv(slot, vlast_ref)
        gates = g_ref[...]
        for z, r in enumerate(head_rows):
            cols = slice(z * HEAD_DIM, (z + 1) * HEAD_DIM)
            g_s = gates[:, z * N_NSA_BRANCHES + 1:z * N_NSA_BRANCHES + 2]
            l_row = jnp.sum(l_ref[r, :], axis=-1, keepdims=True)
            o_s = acc_ref[r, :] / jnp.maximum(l_row, 1e-30)
            o_ref[:, cols] = (ocw_ref[:, cols] + g_s * o_s).astype(o_ref.dtype)


def _sel_attention(q, sel_bias, ks_ext, vs, ocw, gates):
    s = q.shape[0]
    n_qb = s // SEL_Q_TILE
    grp_w = Q_PER_KV * HEAD_DIM
    qb_of, kt_of = [], []
    for qb in range(n_qb):
        for kt in range((qb * SEL_Q_TILE) // K_TILE + 1):
            qb_of.append(qb)
            kt_of.append(kt)
    qb_arr = jnp.asarray(np.asarray(qb_of, np.int32))
    kt_arr = jnp.asarray(np.asarray(kt_of, np.int32))
    grid_spec = pltpu.PrefetchScalarGridSpec(
        num_scalar_prefetch=2,
        grid=(N_KV_GROUPS, len(qb_of)),
        in_specs=[pl.BlockSpec((SEL_Q_TILE, grp_w), lambda g, i, qbr, ktr: (qbr[i], g)),
                  pl.BlockSpec((None, SEL_Q_TILE, LANES), lambda g, i, qbr, ktr: (g, qbr[i], 0)),
                  pl.BlockSpec((K_TILE, 2 * HEAD_DIM), lambda g, i, qbr, ktr: (ktr[i], g)),
                  pl.BlockSpec((K_TILE, HEAD_DIM), lambda g, i, qbr, ktr: (jnp.maximum(ktr[i] - 1, 0), g)),
                  pl.BlockSpec((K_TILE, HEAD_DIM),
                               lambda g, i, qbr, ktr: ((qbr[i] * SEL_Q_TILE) // K_TILE, g)),
                  pl.BlockSpec((SEL_Q_TILE, grp_w), lambda g, i, qbr, ktr: (qbr[i], g)),
                  pl.BlockSpec((SEL_Q_TILE, LANES), lambda g, i, qbr, ktr: (qbr[i], g))],
        out_specs=pl.BlockSpec((SEL_Q_TILE, grp_w), lambda g, i, qbr, ktr: (qbr[i], g)),
        scratch_shapes=[pltpu.VMEM((SEL_ROWS, 2 * HEAD_DIM), BF16),
                        pltpu.VMEM((SEL_ROWS, LANES), F32),
                        pltpu.VMEM((SEL_ROWS, LANES), F32),
                        pltpu.VMEM((SEL_ROWS, HEAD_DIM), F32),
                        pltpu.VMEM((2, SEL_ROWS, K_TILE), BF16),
                        pltpu.VMEM((2, SEL_ROWS, LANES), F32)],
    )
    return pl.pallas_call(
        _sel_attn_kernel,
        out_shape=jax.ShapeDtypeStruct((s, ATTN_WIDTH), BF16),
        grid_spec=grid_spec,
        compiler_params=_params("parallel", "arbitrary"),
        name="sel_attention",
    )(qb_arr, kt_arr, q, sel_bias, ks_ext, vs, vs, ocw, gates)


def _rnn_kernel(rx_ref, rxs_ref, ry_ref, rys_ref, cw_ref, cb_ref, wa_ref, ba_ref, wx_ref, bx_ref, lam_ref,
                o_ref, ext_ref, a_ref, b_ref, h_ref, *, tt, tc):
    ti = pl.program_id(1)
    n_blk = tc // RNN_BLOCK_DIM
    halo = SUBLANES

    @pl.when(ti == 0)
    def _():
        ext_ref[0:halo, :] = jnp.zeros((halo, tc), F32)
        h_ref[...] = jnp.zeros(h_ref.shape, F32)

    @pl.when(ti > 0)
    def _():
        ext_ref[0:halo, :] = ext_ref[tt:tt + halo, :]

    ext_ref[halo:halo + tt, :] = _unshift(rx_ref[...], rxs_ref[...])
    xr = cb_ref[...] + ext_ref[pl.ds(halo - (RNN_CONV - 1), tt), :] * cw_ref[0:1, :]
    for j in range(1, RNN_CONV):
        xr = xr + ext_ref[pl.ds(halo - (RNN_CONV - 1) + j, tt), :] * cw_ref[j:j + 1, :]

    sp = jnp.maximum(-lam_ref[...], 0.0) + jnp.log(1.0 + jnp.exp(-jnp.abs(lam_ref[...])))
    neg_c_sp = -RG_C * sp
    xb = xr.astype(BF16)
    for blk in range(n_blk):
        cols = slice(blk * RNN_BLOCK_DIM, (blk + 1) * RNN_BLOCK_DIM)
        xs = xb[:, cols]
        r = jax.nn.sigmoid(jnp.dot(xs, wa_ref[blk], preferred_element_type=F32) + ba_ref[:, cols])
        i = jax.nn.sigmoid(jnp.dot(xs, wx_ref[blk], preferred_element_type=F32) + bx_ref[:, cols])
        a = jnp.exp(r * neg_c_sp[:, cols])
        one_m = 1.0 - a * a
        root = jnp.where(one_m > 0.0, one_m * lax.rsqrt(one_m), 0.0)
        a_ref[:, cols] = a
        b_ref[:, cols] = root * (i * xr[:, cols])

    row = lax.broadcasted_iota(jnp.int32, (SUBLANES, tc), 0)

    def scan_rows(i, carry):
        r0 = pl.multiple_of(i * SUBLANES, SUBLANES)
        a8 = a_ref[pl.ds(r0, SUBLANES), :]
        b8 = b_ref[pl.ds(r0, SUBLANES), :]
        for d in (1, 2, 4):
            keep = row >= d
            a_sh = pltpu.roll(a8, d, axis=0)
            b_sh = pltpu.roll(b8, d, axis=0)
            b8 = jnp.where(keep, a8 * b_sh + b8, b8)
            a8 = jnp.where(keep, a8 * a_sh, a8)
        h8 = a8 * carry + b8
        b_ref[pl.ds(r0, SUBLANES), :] = h8
        return jnp.broadcast_to(h8[SUBLANES - 1:SUBLANES, :], (SUBLANES, tc))

    h_ref[...] = lax.fori_loop(0, tt // SUBLANES, scan_rows, h_ref[...])
    ry = _unshift(ry_ref[...], rys_ref[...])
    o_ref[...] = (b_ref[...] * jax.nn.gelu(ry, approximate=True)).astype(o_ref.dtype)


def _rnn_branch(proj, cw, cb, wa, ba, wx, bx, lam, *, tt, tc):
    s = proj.shape[0]
    n_cb = RNN_WIDTH // tc
    blk_per = tc // RNN_BLOCK_DIM
    rx_blk = COL_RX // tc
    ry_blk = COL_RY // tc
    per = tc // LANES
    vec = pl.BlockSpec((1, tc), lambda c, t: (0, c))
    wspec = pl.BlockSpec((blk_per, RNN_BLOCK_DIM, RNN_BLOCK_DIM), lambda c, t: (c, 0, 0))
    return pl.pallas_call(
        functools.partial(_rnn_kernel, tt=tt, tc=tc),
        out_shape=jax.ShapeDtypeStruct((s, RNN_WIDTH), BF16),
        grid=(n_cb, s // tt),
        in_specs=[pl.BlockSpec((tt, tc), lambda c, t: (t, rx_blk + c)),
                  pl.BlockSpec((tt, LANES), lambda c, t: (t, (rx_blk + c + 1) * per)),
                  pl.BlockSpec((tt, tc), lambda c, t: (t, ry_blk + c)),
                  pl.BlockSpec((tt, LANES), lambda c, t: (t, (ry_blk + c + 1) * per)),
                  pl.BlockSpec((RNN_CONV, tc), lambda c, t: (0, c)),
                  vec, wspec, vec, wspec, vec, vec],
        out_specs=pl.BlockSpec((tt, tc), lambda c, t: (t, c)),
        scratch_shapes=[pltpu.VMEM((tt + SUBLANES, tc), F32),
                        pltpu.VMEM((tt, tc), F32),
                        pltpu.VMEM((tt, tc), F32),
                        pltpu.VMEM((SUBLANES, tc), F32)],
        compiler_params=_params("parallel", "arbitrary"),
        name="rg_lru",
    )(proj, proj, proj, proj, cw, cb, wa, ba, wx, bx, lam)


def _overlap_t(s):
    n_chunks = s // CMP_STRIDE
    n_cmp = n_chunks - 1
    n_sel = s // SEL_BLOCK
    cmp_start = np.arange(n_cmp) * CMP_STRIDE
    sel_start = np.arange(n_sel) * SEL_BLOCK
    ov = np.clip(np.minimum(cmp_start[:, None] + CMP_LEN, sel_start[None, :] + SEL_BLOCK)
                 - np.maximum(cmp_start[:, None], sel_start[None, :]), 0, None) / CMP_LEN
    out = np.zeros((LANES, n_chunks), np.float32)
    out[:n_sel, :n_cmp] = ov.T
    return out


def _block_onehot(s):
    return (np.arange(s)[:, None] // SEL_BLOCK == np.arange(LANES)[None, :]).astype(np.float32)


def kernel(x, c, positions, w_cond, b_cond, w_mod, b_mod, norm_mix, norm_ffn, w_in, q_norm, k_norm, cmp_pe_k, cmp_w_k, cmp_pe_v, cmp_w_v, rnn_conv_w, rnn_conv_b, rg_w_a, rg_b_a, rg_w_x, rg_b_x, rg_lambda, w_attn_up, w_rnn_up, w_out, w_ffn_in, ffn_conv_w, ffn_conv_b, w_ffn_down):
    b, s, d = x.shape
    depth = w_in.shape[0]
    assert b == 1 and d == D_MODEL and s % 2048 == 0 and s // SEL_BLOCK <= LANES

    c_emb = _vecmat(c.reshape(1, d, 1), w_cond[None], b_cond.reshape(1, 1, -1), silu=True, tn=w_cond.shape[1])
    c_col = jnp.broadcast_to(c_emb.reshape(1, -1, 1), (depth, c_emb.shape[-1], 1))
    mod = _vecmat(c_col, w_mod, b_mod[:, None, :], silu=False, tn=2048)

    inv_freq = ROPE_THETA ** (-jnp.arange(0, ROT_DIM, 2, dtype=jnp.float32) / ROT_DIM)
    freq_row = jnp.concatenate([inv_freq, inv_freq, jnp.zeros((LANES - ROT_DIM,), F32)])[None, :]
    rope_c, rope_sa, rope_sb = _rope_tables(positions.reshape(s, 1), freq_row, tr=512)

    ovt = jnp.asarray(_overlap_t(s), BF16)
    blk_onehot = jnp.asarray(_block_onehot(s), BF16)

    w_in_b = w_in.astype(BF16)
    n_tail = w_in.shape[2] - (GATE_SRC + N_PROJ_REST)
    assert n_tail == PROJ_SHIFT
    w_tail = jnp.pad(w_in[:, :, GATE_SRC + N_PROJ_REST:], ((0, 0), (0, 0), (0, LANES - n_tail))).astype(BF16)
    w_attn_up_b = w_attn_up.astype(BF16)
    w_rnn_up_b = w_rnn_up.astype(BF16)
    w_out_b = w_out.astype(BF16)
    w_ffn_in_b = w_ffn_in.astype(BF16)
    w_ffn_down_b = w_ffn_down.astype(BF16)

    h = x.reshape(s, d)
    for l in range(depth):
        sh1, sc1, g1, sh2, sc2, g2 = [mod[l, :, i * d:(i + 1) * d] for i in range(N_MOD)]
        proj, gr_tail = _norm_matmul(h, norm_mix[l][None], sc1, sh1, w_in_b, w_tail,
                                     layer=l, n=N_PROJ, tm=512, tn=1024, out_dtype=F32)
        q, kc, ks_ext, vs, kw, vw, gates = _prep(proj, proj, rope_c, rope_sa, rope_sb,
                                                 q_norm[l][None], k_norm[l], blk_onehot, tr=256)
        k_cmp = _compress(kc, 0, cmp_pe_k[l], cmp_w_k[l].astype(BF16), k_norm[l][0:1], do_norm=True)
        v_cmp = _compress(proj, COL_VC // HEAD_DIM, cmp_pe_v[l], cmp_w_v[l].astype(BF16),
                          k_norm[l][0:1], do_norm=False)
        ocw, sel_bias = _cw_attention(q, k_cmp, v_cmp, ovt, kw, vw, gates)
        attn = _sel_attention(q, sel_bias, ks_ext, vs, ocw, gates)
        rnn = _rnn_branch(proj, rnn_conv_w[l], rnn_conv_b[l][None], rg_w_a[l].astype(BF16), rg_b_a[l][None],
                          rg_w_x[l].astype(BF16), rg_b_x[l][None], rg_lambda[l][None], tt=512, tc=1024)
        merged = _merge(attn, rnn, w_attn_up_b, w_rnn_up_b, proj, gr_tail, layer=l, tm=512, tn=1024)
        h = _matmul_residual(merged, w_out_b, h, g1, layer=l, tm=1024, tn=512)
        act = _ffn_in(h, norm_ffn[l][None], sc2, sh2, w_ffn_in_b, ffn_conv_w[l], ffn_conv_b[l][None],
                      layer=l, tm=512, tn=512)
        h = _matmul_residual(act, w_ffn_down_b, h, g2, layer=l, tm=1024, tn=256)
    return h.reshape(b, s, d)
```

```python
import functools

import numpy as np
import jax
import jax.numpy as jnp
from jax import lax
from jax.experimental import pallas as pl
from jax.experimental.pallas import tpu as pltpu

F32 = jnp.float32
BF16 = jnp.bfloat16

D_MODEL = 4096
N_Q_HEADS = 16
N_KV_GROUPS = 4
HEAD_DIM = 128
Q_PER_KV = N_Q_HEADS // N_KV_GROUPS
ATTN_WIDTH = N_Q_HEADS * HEAD_DIM
KV_WIDTH = N_KV_GROUPS * HEAD_DIM
N_NSA_BRANCHES = 3
ROT_DIM = HEAD_DIM // 4
ROPE_THETA = 500000.0
CMP_LEN = 32
CMP_STRIDE = 16
SEL_BLOCK = 64
SEL_TOP = 16
N_LOCAL_SEL = 2
WINDOW = 512
FORCE_SCORE = 2.0 * Q_PER_KV + 1.0
RNN_WIDTH = 2048
RNN_BLOCKS = 16
RNN_BLOCK_DIM = RNN_WIDTH // RNN_BLOCKS
RNN_CONV = 4
RG_C = 8.0
D_FF = 2 * D_MODEL
FFN_CONV = 3
N_MOD = 6
EPS = 1e-6
NEG_INF = -1e30
ATTN_SCALE = HEAD_DIM ** -0.5
LOG2_E = 1.4426950408889634
Q_SCALE = ATTN_SCALE * LOG2_E

LANES = 128
SUBLANES = 8
VMEM_LIMIT_BYTES = 56 * 1024 * 1024

COL_Q = 0
COL_KC = ATTN_WIDTH
COL_VC = COL_KC + KV_WIDTH
COL_KS = COL_VC + KV_WIDTH
COL_VS = COL_KS + KV_WIDTH
COL_KW = COL_VS + KV_WIDTH
COL_VW = COL_KW + KV_WIDTH
N_PROJ_ATTN = COL_VW + KV_WIDTH
GATE_SRC = N_PROJ_ATTN
PROJ_SHIFT = N_Q_HEADS * N_NSA_BRANCHES
N_PROJ_REST = 2 * RNN_WIDTH + 2 * D_MODEL
N_PROJ = N_PROJ_ATTN + N_PROJ_REST
COL_RX = GATE_SRC
COL_RY = COL_RX + RNN_WIDTH
COL_GA = COL_RY + RNN_WIDTH
COL_GR = COL_GA + D_MODEL

Q_TILE = 256
K_TILE = 1024
SEL_Q_TILE = 512
SEL_ROWS = Q_PER_KV * SEL_Q_TILE
N_WIN_BLK = WINDOW // Q_TILE + 1
NORM_CHUNK = 64


def _unshift(main, spill):
    return jnp.concatenate([main, spill], axis=1)[:, PROJ_SHIFT:PROJ_SHIFT + main.shape[1]]


def _params(*sem):
    return pltpu.CompilerParams(dimension_semantics=sem, vmem_limit_bytes=VMEM_LIMIT_BYTES)


def _vecmat_kernel(x_ref, w_ref, b_ref, o_ref, *, silu):
    y = jnp.sum(w_ref[...] * x_ref[...], axis=0, keepdims=True) + b_ref[...]
    if silu:
        y = y * jax.nn.sigmoid(y)
    o_ref[...] = y


def _vecmat(x_col, w, b, *, silu, tn):
    n_l, k, n = w.shape
    return pl.pallas_call(
        functools.partial(_vecmat_kernel, silu=silu),
        out_shape=jax.ShapeDtypeStruct((n_l, 1, n), F32),
        grid=(n_l, n // tn),
        in_specs=[pl.BlockSpec((None, k, 1), lambda l, j: (l, 0, 0)),
                  pl.BlockSpec((None, k, tn), lambda l, j: (l, 0, j)),
                  pl.BlockSpec((None, 1, tn), lambda l, j: (l, 0, j))],
        out_specs=pl.BlockSpec((None, 1, tn), lambda l, j: (l, 0, j)),
        compiler_params=_params("parallel", "parallel"),
        name="vecmat",
    )(x_col, w, b)


def _rope_table_kernel(pos_ref, freq_ref, c_ref, sa_ref, sb_ref):
    ang = pos_ref[...].astype(F32) * freq_ref[...]
    lane = lax.broadcasted_iota(jnp.int32, ang.shape, 1)
    cos = jnp.cos(ang)
    sin = jnp.sin(ang)
    c_ref[...] = jnp.where(lane < ROT_DIM, cos, 1.0)
    sa_ref[...] = jnp.where(lane < ROT_DIM // 2, -sin, 0.0)
    sb_ref[...] = jnp.where((lane >= ROT_DIM // 2) & (lane < ROT_DIM), sin, 0.0)


def _rope_tables(pos_col, freq_row, tr):
    s = pos_col.shape[0]
    spec = pl.BlockSpec((tr, LANES), lambda i: (i, 0))
    return pl.pallas_call(
        _rope_table_kernel,
        out_shape=[jax.ShapeDtypeStruct((s, LANES), F32)] * 3,
        grid=(s // tr,),
        in_specs=[pl.BlockSpec((tr, 1), lambda i: (i, 0)),
                  pl.BlockSpec((1, LANES), lambda i: (0, 0))],
        out_specs=[spec, spec, spec],
        compiler_params=_params("parallel"),
        name="rope_tables",
    )(pos_col, freq_row)


def _ada_norm(h_ref, gain_ref, sc_ref, sh_ref, u_ref):
    scale = gain_ref[...] * (1.0 + sc_ref[...])
    shift = sh_ref[...]
    for r0 in range(0, h_ref.shape[0], NORM_CHUNK):
        x = h_ref[r0:r0 + NORM_CHUNK, :]
        y = x * lax.rsqrt(jnp.mean(x * x, axis=-1, keepdims=True) + EPS)
        u_ref[r0:r0 + NORM_CHUNK, :] = (y * scale + shift).astype(BF16)


def _norm_mm_kernel(h_ref, gain_ref, sc_ref, sh_ref, w_ref, wg_ref, o_ref, og_ref, u_ref):
    @pl.when(pl.program_id(1) == 0)
    def _():
        _ada_norm(h_ref, gain_ref, sc_ref, sh_ref, u_ref)
        og_ref[...] = jnp.dot(u_ref[...], wg_ref[...], preferred_element_type=F32)

    o_ref[...] = jnp.dot(u_ref[...], w_ref[...], preferred_element_type=F32).astype(o_ref.dtype)


def _norm_matmul(h, gain, sc, sh, w, wg, *, layer, n, tm, tn, out_dtype):
    s, d = h.shape
    ng = wg.shape[2]
    row = pl.BlockSpec((1, d), lambda i, j: (0, 0))
    return pl.pallas_call(
        _norm_mm_kernel,
        out_shape=[jax.ShapeDtypeStruct((s, n), out_dtype), jax.ShapeDtypeStruct((s, ng), F32)],
        grid=(s // tm, n // tn),
        in_specs=[pl.BlockSpec((tm, d), lambda i, j: (i, 0)), row, row, row,
                  pl.BlockSpec((None, d, tn), lambda i, j: (layer, 0, j)),
                  pl.BlockSpec((None, d, ng), lambda i, j: (layer, 0, 0))],
        out_specs=[pl.BlockSpec((tm, tn), lambda i, j: (i, j)),
                   pl.BlockSpec((tm, ng), lambda i, j: (i, 0))],
        scratch_shapes=[pltpu.VMEM((tm, d), BF16)],
        compiler_params=_params("parallel", "arbitrary"),
        name="norm_matmul",
    )(h, gain, sc, sh, w, wg)


def _ffn_in_kernel(h_ref, gain_ref, sc_ref, sh_ref, wg_ref, wu_ref, cw_ref, cb_ref, o_ref,
                   u_ref, ext_ref, carry_ref, *, tm):
    i = pl.program_id(0)
    j = pl.program_id(1)
    halo = SUBLANES

    @pl.when(j == 0)
    def _():
        _ada_norm(h_ref, gain_ref, sc_ref, sh_ref, u_ref)

    @pl.when(i == 0)
    def _():
        ext_ref[0:halo, :] = jnp.zeros((halo, ext_ref.shape[1]), F32)

    @pl.when(i > 0)
    def _():
        ext_ref[0:halo, :] = carry_ref[j]

    ext_ref[halo:halo + tm, :] = jnp.dot(u_ref[...], wg_ref[...], preferred_element_type=F32)
    carry_ref[j] = ext_ref[tm:tm + halo, :]
    y = cb_ref[...] + ext_ref[pl.ds(halo - (FFN_CONV - 1), tm), :] * cw_ref[0:1, :]
    for t in range(1, FFN_CONV):
        y = y + ext_ref[pl.ds(halo - (FFN_CONV - 1) + t, tm), :] * cw_ref[t:t + 1, :]
    up = jnp.dot(u_ref[...], wu_ref[...], preferred_element_type=F32)
    o_ref[...] = (y * jax.nn.sigmoid(y) * up).astype(o_ref.dtype)


def _ffn_in(h, gain, sc, sh, w, cw, cb, *, layer, tm, tn):
    s, d = h.shape
    n_j = D_FF // tn
    row = pl.BlockSpec((1, d), lambda i, j: (0, 0))
    return pl.pallas_call(
        functools.partial(_ffn_in_kernel, tm=tm),
        out_shape=jax.ShapeDtypeStruct((s, D_FF), BF16),
        grid=(s // tm, n_j),
        in_specs=[pl.BlockSpec((tm, d), lambda i, j: (i, 0)), row, row, row,
                  pl.BlockSpec((None, d, tn), lambda i, j: (layer, 0, j)),
                  pl.BlockSpec((None, d, tn), lambda i, j: (layer, 0, n_j + j)),
                  pl.BlockSpec((FFN_CONV, tn), lambda i, j: (0, j)),
                  pl.BlockSpec((1, tn), lambda i, j: (0, j))],
        out_specs=pl.BlockSpec((tm, tn), lambda i, j: (i, j)),
        scratch_shapes=[pltpu.VMEM((tm, d), BF16),
                        pltpu.VMEM((tm + SUBLANES, tn), F32),
                        pltpu.VMEM((n_j, SUBLANES, tn), F32)],
        compiler_params=_params("arbitrary", "arbitrary"),
        name="ffn_in",
    )(h, gain, sc, sh, w, w, cw, cb)


def _mm_res_kernel(a_ref, w_ref, h_ref, g_ref, o_ref):
    y = jnp.dot(a_ref[...], w_ref[...], preferred_element_type=F32)
    o_ref[...] = h_ref[...] + g_ref[...] * y


def _matmul_residual(a, w, h, g, *, layer, tm, tn):
    s, k = a.shape
    n = w.shape[2]
    return pl.pallas_call(
        _mm_res_kernel,
        out_shape=jax.ShapeDtypeStruct((s, n), F32),
        grid=(s // tm, n // tn),
        in_specs=[pl.BlockSpec((tm, k), lambda i, j: (i, 0)),
                  pl.BlockSpec((None, k, tn), lambda i, j: (layer, 0, j)),
                  pl.BlockSpec((tm, tn), lambda i, j: (i, j)),
                  pl.BlockSpec((1, tn), lambda i, j: (0, j))],
        out_specs=pl.BlockSpec((tm, tn), lambda i, j: (i, j)),
        compiler_params=_params("parallel", "parallel"),
        name="matmul_residual",
    )(a, w, h, g)


def _merge_kernel(attn_ref, rnn_ref, wa_ref, wr_ref, ga_ref, gas_ref, gr_ref, grs_ref, tail_ref, o_ref):
    ya = jnp.dot(attn_ref[...], wa_ref[...], preferred_element_type=F32)
    yr = jnp.dot(rnn_ref[...], wr_ref[...], preferred_element_type=F32)
    g_attn = _unshift(ga_ref[...], gas_ref[...])
    is_last = pl.program_id(1) == pl.num_programs(1) - 1
    g_rnn = _unshift(gr_ref[...], jnp.where(is_last, tail_ref[...], grs_ref[...]))
    o_ref[...] = (jax.nn.sigmoid(g_attn) * ya + jax.nn.sigmoid(g_rnn) * yr).astype(o_ref.dtype)


def _merge(attn, rnn, wa, wr, proj, tail, *, layer, tm, tn):
    s, k = attn.shape
    n = wa.shape[2]
    ga_blk = COL_GA // tn
    gr_blk = COL_GR // tn
    per = tn // LANES
    last_spill = proj.shape[1] // LANES - 1
    return pl.pallas_call(
        _merge_kernel,
        out_shape=jax.ShapeDtypeStruct((s, n), BF16),
        grid=(s // tm, n // tn),
        in_specs=[pl.BlockSpec((tm, k), lambda i, j: (i, 0)),
                  pl.BlockSpec((tm, k), lambda i, j: (i, 0)),
                  pl.BlockSpec((None, k, tn), lambda i, j: (layer, 0, j)),
                  pl.BlockSpec((None, k, tn), lambda i, j: (layer, 0, j)),
                  pl.BlockSpec((tm, tn), lambda i, j: (i, ga_blk + j)),
                  pl.BlockSpec((tm, LANES), lambda i, j: (i, (ga_blk + j + 1) * per)),
                  pl.BlockSpec((tm, tn), lambda i, j: (i, gr_blk + j)),
                  pl.BlockSpec((tm, LANES), lambda i, j: (i, jnp.minimum((gr_blk + j + 1) * per, last_spill))),
                  pl.BlockSpec((tm, LANES), lambda i, j: (i, 0))],
        out_specs=pl.BlockSpec((tm, tn), lambda i, j: (i, j)),
        compiler_params=_params("parallel", "parallel"),
        name="merge",
    )(attn, rnn, wa, wr, proj, proj, proj, proj, tail)


def _head_norm(x, gain):
    return x * lax.rsqrt(jnp.mean(x * x, axis=-1, keepdims=True) + EPS) * gain


def _rope(x, c, sa, sb):
    return (x * c + pltpu.roll(x, LANES - ROT_DIM // 2, axis=1) * sa
            + pltpu.roll(x, ROT_DIM // 2, axis=1) * sb)


def _prep_kernel(p_ref, graw_ref, c_ref, sa_ref, sb_ref, qn_ref, kn_ref, blk_ref,
                 q_ref, kc_ref, ks_ref, vs_ref, kw_ref, vw_ref, g_ref):
    c, sa, sb = c_ref[...], sa_ref[...], sb_ref[...]
    qn = qn_ref[...]
    for hd in range(N_Q_HEADS):
        cols = slice(hd * HEAD_DIM, (hd + 1) * HEAD_DIM)
        q_ref[:, cols] = (_rope(_head_norm(p_ref[:, cols], qn), c, sa, sb) * Q_SCALE).astype(BF16)
    for g in range(N_KV_GROUPS):
        cols = slice(g * HEAD_DIM, (g + 1) * HEAD_DIM)

        def src(base):
            return p_ref[:, base + g * HEAD_DIM:base + (g + 1) * HEAD_DIM]

        kc_ref[:, cols] = _rope(src(COL_KC), c, sa, sb)
        ks_ref[:, 2 * g * HEAD_DIM:(2 * g + 1) * HEAD_DIM] = _rope(
            _head_norm(src(COL_KS), kn_ref[1:2, :]), c, sa, sb).astype(BF16)
        ks_ref[:, (2 * g + 1) * HEAD_DIM:(2 * g + 2) * HEAD_DIM] = blk_ref[...]
        kw_ref[:, cols] = _rope(_head_norm(src(COL_KW), kn_ref[2:3, :]), c, sa, sb).astype(BF16)
        vs_ref[:, cols] = src(COL_VS).astype(BF16)
        vw_ref[:, cols] = src(COL_VW).astype(BF16)
    sig = jax.nn.sigmoid(graw_ref[...])
    per = Q_PER_KV * N_NSA_BRANCHES
    for g in range(N_KV_GROUPS):
        g_ref[:, g * LANES:(g + 1) * LANES] = sig if g == 0 else pltpu.roll(sig, LANES - g * per, axis=1)


def _prep(proj, graw, c, sa, sb, qn, kn, blk_onehot, *, tr):
    s = proj.shape[0]
    ng = N_KV_GROUPS * LANES
    tab = pl.BlockSpec((tr, LANES), lambda i: (i, 0))
    kv_spec = pl.BlockSpec((tr, KV_WIDTH), lambda i: (i, 0))
    return pl.pallas_call(
        _prep_kernel,
        out_shape=[jax.ShapeDtypeStruct((s, ATTN_WIDTH), BF16),
                   jax.ShapeDtypeStruct((s, KV_WIDTH), F32),
                   jax.ShapeDtypeStruct((s, 2 * KV_WIDTH), BF16),
                   jax.ShapeDtypeStruct((s, KV_WIDTH), BF16),
                   jax.ShapeDtypeStruct((s, KV_WIDTH), BF16),
                   jax.ShapeDtypeStruct((s, KV_WIDTH), BF16),
                   jax.ShapeDtypeStruct((s, ng), F32)],
        grid=(s // tr,),
        in_specs=[pl.BlockSpec((tr, N_PROJ_ATTN), lambda i: (i, 0)),
                  pl.BlockSpec((tr, LANES), lambda i: (i, GATE_SRC // LANES)),
                  tab, tab, tab,
                  pl.BlockSpec((1, HEAD_DIM), lambda i: (0, 0)),
                  pl.BlockSpec((N_NSA_BRANCHES, HEAD_DIM), lambda i: (0, 0)),
                  tab],
        out_specs=[pl.BlockSpec((tr, ATTN_WIDTH), lambda i: (i, 0)),
                   kv_spec, pl.BlockSpec((tr, 2 * KV_WIDTH), lambda i: (i, 0)),
                   kv_spec, kv_spec, kv_spec,
                   pl.BlockSpec((tr, ng), lambda i: (i, 0))],
        compiler_params=_params("parallel"),
        name="qk_prep",
    )(proj, graw, c, sa, sb, qn, kn, blk_onehot)


def _compress_kernel(x_ref, pe_ref, w_ref, gain_ref, o_ref, shift_ref, *, do_norm, n_chunks):
    acc_a = jnp.zeros((n_chunks, HEAD_DIM), F32)
    acc_b = jnp.zeros((n_chunks, HEAD_DIM), F32)
    for l in range(CMP_STRIDE):
        xl = x_ref[pl.ds(l, n_chunks, stride=CMP_STRIDE), :]
        xa = (xl + pe_ref[l:l + 1, :]).astype(BF16)
        xb = (xl + pe_ref[CMP_STRIDE + l:CMP_STRIDE + l + 1, :]).astype(BF16)
        acc_a = acc_a + jnp.dot(xa, w_ref[l], preferred_element_type=F32)
        acc_b = acc_b + jnp.dot(xb, w_ref[CMP_STRIDE + l], preferred_element_type=F32)
    shift_ref[0:n_chunks, :] = acc_b
    shift_ref[n_chunks:n_chunks + SUBLANES, :] = jnp.zeros((SUBLANES, HEAD_DIM), F32)
    out = acc_a + shift_ref[1:n_chunks + 1, :]
    if do_norm:
        out = _head_norm(out, gain_ref[...])
    o_ref[...] = out.astype(o_ref.dtype)


def _compress(x, col_blk0, pe, w, gain, *, do_norm):
    s = x.shape[0]
    n_chunks = s // CMP_STRIDE
    return pl.pallas_call(
        functools.partial(_compress_kernel, do_norm=do_norm, n_chunks=n_chunks),
        out_shape=jax.ShapeDtypeStruct((N_KV_GROUPS, n_chunks, HEAD_DIM), BF16),
        grid=(N_KV_GROUPS,),
        in_specs=[pl.BlockSpec((s, HEAD_DIM), lambda g: (0, col_blk0 + g)),
                  pl.BlockSpec((CMP_LEN, HEAD_DIM), lambda g: (0, 0)),
                  pl.BlockSpec((CMP_LEN, HEAD_DIM, HEAD_DIM), lambda g: (0, 0, 0)),
                  pl.BlockSpec((1, HEAD_DIM), lambda g: (0, 0))],
        out_specs=pl.BlockSpec((None, n_chunks, HEAD_DIM), lambda g: (g, 0, 0)),
        scratch_shapes=[pltpu.VMEM((n_chunks + SUBLANES, HEAD_DIM), F32)],
        compiler_params=_params("parallel"),
        name="compress",
    )(x, pe, w, gain)


def _dot_nt(a, b):
    return lax.dot_general(a, b, (((1,), (1,)), ((), ())), preferred_element_type=F32)


def _biased_softmax2(s, bias):
    s = s + bias
    p = jnp.exp2(s - jnp.max(s, axis=-1, keepdims=True))
    return p, jnp.sum(p, axis=-1, keepdims=True)


def _stack_heads(q_ref):
    return jnp.concatenate([q_ref[:, z * HEAD_DIM:(z + 1) * HEAD_DIM] for z in range(Q_PER_KV)], axis=0)


def _cw_attn_kernel(*refs, n_chunks):
    q_ref, kc_ref, vc_ref, ovt_ref = refs[:4]
    kw_refs = refs[4:4 + N_WIN_BLK]
    vw_refs = refs[4 + N_WIN_BLK:4 + 2 * N_WIN_BLK]
    g_ref, ocw_ref, sel_ref, score_ref, cnt_ref = refs[4 + 2 * N_WIN_BLK:]
    qb = pl.program_id(1)
    t0 = qb * Q_TILE
    q4 = _stack_heads(q_ref)
    head_rows = [slice(z * Q_TILE, (z + 1) * Q_TILE) for z in range(Q_PER_KV)]

    tq = t0 + lax.broadcasted_iota(jnp.int32, (Q_TILE, n_chunks), 0)
    n_id = lax.broadcasted_iota(jnp.int32, (Q_TILE, n_chunks), 1)
    bias_c = jnp.where((n_id * CMP_STRIDE + (CMP_LEN - 1) <= tq) & (n_id < n_chunks - 1), 0.0, NEG_INF)
    row_ok = jnp.where(t0 + lax.broadcasted_iota(jnp.int32, (Q_TILE, 1), 0) >= CMP_LEN - 1, 1.0, 0.0)
    s_c = _dot_nt(q4, kc_ref[...])
    p_heads = []
    for r in head_rows:
        p, l = _biased_softmax2(s_c[r], bias_c)
        p_heads.append(p * (row_ok / jnp.maximum(l, 1e-30)))
    o_c = jnp.dot(jnp.concatenate(p_heads, axis=0).astype(BF16), vc_ref[...],
                  preferred_element_type=F32)

    p_sum = p_heads[0]
    for z in range(1, Q_PER_KV):
        p_sum = p_sum + p_heads[z]
    p_hi = p_sum.astype(BF16)
    p_lo = (p_sum - p_hi.astype(F32)).astype(BF16)
    ovt = ovt_ref[...]
    imp_t = _dot_nt(ovt, p_hi) + _dot_nt(ovt, p_lo)
    j_id = lax.broadcasted_iota(jnp.int32, (LANES, Q_TILE), 0)
    cur = (t0 + lax.broadcasted_iota(jnp.int32, (LANES, Q_TILE), 1)) // SEL_BLOCK
    valid = j_id <= cur
    forced = (j_id == 0) | (valid & (j_id > cur - N_LOCAL_SEL))
    score_ref[...] = jnp.where(forced, FORCE_SCORE, jnp.where(valid, imp_t, -1.0))
    cnt_ref[...] = jnp.zeros(cnt_ref.shape, F32)

    n_keys = N_WIN_BLK * Q_TILE
    tq_w = t0 + lax.broadcasted_iota(jnp.int32, (Q_TILE, n_keys), 0)
    pos = t0 - WINDOW + lax.broadcasted_iota(jnp.int32, (Q_TILE, n_keys), 1)
    bias_w = jnp.where((pos <= tq_w) & (pos > tq_w - WINDOW) & (pos >= 0), 0.0, NEG_INF)
    s_w = jnp.concatenate([_dot_nt(q4, kw_refs[i][...]) for i in range(N_WIN_BLK)], axis=1)
    pw_heads = []
    for r in head_rows:
        p, l = _biased_softmax2(s_w[r], bias_w)
        pw_heads.append((p * (1.0 / jnp.maximum(l, 1e-30))).astype(BF16))
    p_w = jnp.concatenate(pw_heads, axis=0)
    o_w = jnp.dot(p_w[:, 0:Q_TILE], vw_refs[0][...], preferred_element_type=F32)
    for i in range(1, N_WIN_BLK):
        o_w = o_w + jnp.dot(p_w[:, i * Q_TILE:(i + 1) * Q_TILE], vw_refs[i][...],
                            preferred_element_type=F32)

    gates = g_ref[...]
    for z, r in enumerate(head_rows):
        g_c = gates[:, z * N_NSA_BRANCHES:z * N_NSA_BRANCHES + 1]
        g_w = gates[:, z * N_NSA_BRANCHES + 2:z * N_NSA_BRANCHES + 3]
        ocw_ref[:, z * HEAD_DIM:(z + 1) * HEAD_DIM] = g_c * o_c[r] + g_w * o_w[r]

    n_grp = LANES // SUBLANES
    last_src_grp = ((t0 + Q_TILE - 1) // SEL_BLOCK) // SUBLANES
    sub = lax.broadcasted_iota(jnp.int32, (SUBLANES, Q_TILE), 0)
    for gj in range(n_grp):
        @pl.when(gj <= last_src_grp)
        def _():
            src = score_ref[gj * SUBLANES:(gj + 1) * SUBLANES, :]
            rows_b = [jnp.broadcast_to(src[r:r + 1, :], (SUBLANES, Q_TILE)) for r in range(SUBLANES)]
            for gi in range(gj + 1):
                tgt = score_ref[gi * SUBLANES:(gi + 1) * SUBLANES, :]
                acc = cnt_ref[gi * SUBLANES:(gi + 1) * SUBLANES, :]
                for r in range(SUBLANES):
                    if gi < gj:
                        inc = jnp.where(rows_b[r] > tgt, 1.0, 0.0)
                    else:
                        inc = jnp.where(sub > r, jnp.where(rows_b[r] >= tgt, 1.0, 0.0),
                                        jnp.where(rows_b[r] > tgt, 1.0, 0.0))
                    acc = acc + inc
                cnt_ref[gi * SUBLANES:(gi + 1) * SUBLANES, :] = acc

            def later_target(gi, carry):
                r0 = pl.multiple_of(gi * SUBLANES, SUBLANES)
                tgt = score_ref[pl.ds(r0, SUBLANES), :]
                acc = cnt_ref[pl.ds(r0, SUBLANES), :]
                for r in range(SUBLANES):
                    acc = acc + jnp.where(rows_b[r] >= tgt, 1.0, 0.0)
                cnt_ref[pl.ds(r0, SUBLANES), :] = acc
                return carry

            lax.fori_loop(gj + 1, last_src_grp + 1, later_target, 0)
    bias_t = jnp.where(cnt_ref[...] < float(SEL_TOP), 0.0, NEG_INF)
    sel_ref[...] = bias_t.T.astype(sel_ref.dtype)


def _cw_attention(q, k_cmp, v_cmp, ovt, kw, vw, gates):
    s = q.shape[0]
    n_chunks = k_cmp.shape[1]
    n_qb = s // Q_TILE
    grp_w = Q_PER_KV * HEAD_DIM

    def win_spec(i):
        return pl.BlockSpec((Q_TILE, HEAD_DIM),
                            lambda g, qb: (jnp.maximum(qb - (N_WIN_BLK - 1) + i, 0), g))

    cmp_spec = pl.BlockSpec((None, n_chunks, HEAD_DIM), lambda g, qb: (g, 0, 0))
    in_specs = ([pl.BlockSpec((Q_TILE, grp_w), lambda g, qb: (qb, g)), cmp_spec, cmp_spec,
                 pl.BlockSpec((LANES, n_chunks), lambda g, qb: (0, 0))]
                + [win_spec(i) for i in range(N_WIN_BLK)] * 2
                + [pl.BlockSpec((Q_TILE, LANES), lambda g, qb: (qb, g))])
    return pl.pallas_call(
        functools.partial(_cw_attn_kernel, n_chunks=n_chunks),
        out_shape=[jax.ShapeDtypeStruct((s, ATTN_WIDTH), F32),
                   jax.ShapeDtypeStruct((N_KV_GROUPS, s, LANES), BF16)],
        grid=(N_KV_GROUPS, n_qb),
        in_specs=in_specs,
        out_specs=[pl.BlockSpec((Q_TILE, grp_w), lambda g, qb: (qb, g)),
                   pl.BlockSpec((None, Q_TILE, LANES), lambda g, qb: (g, qb, 0))],
        scratch_shapes=[pltpu.VMEM((LANES, Q_TILE), F32), pltpu.VMEM((LANES, Q_TILE), F32)],
        compiler_params=_params("parallel", "parallel"),
        name="cmp_win_attention",
    )(q, k_cmp, v_cmp, ovt, *([kw] * N_WIN_BLK), *([vw] * N_WIN_BLK), gates)


def _sel_attn_kernel(qb_ref, kt_ref, q_ref, sb_ref, k_ref, vprev_ref, vlast_ref, ocw_ref, g_ref, o_ref,
                     qx_ref, m_ref, l_ref, acc_ref, p_ref, alpha_ref):
    step = pl.program_id(1)
    qb = qb_ref[step]
    kt = kt_ref[step]
    last_kt = (qb * SEL_Q_TILE) // K_TILE
    n_lane_blk = K_TILE // LANES
    slot = kt % 2
    head_rows = [slice(z * SEL_Q_TILE, (z + 1) * SEL_Q_TILE) for z in range(Q_PER_KV)]

    @pl.when(kt == 0)
    def _():
        for z, r in enumerate(head_rows):
            qx_ref[r, 0:HEAD_DIM] = q_ref[:, z * HEAD_DIM:(z + 1) * HEAD_DIM]
            qx_ref[r, HEAD_DIM:2 * HEAD_DIM] = sb_ref[...]
        m_ref[...] = jnp.full(m_ref.shape, NEG_INF, F32)
        l_ref[...] = jnp.zeros(l_ref.shape, F32)
        acc_ref[...] = jnp.zeros(acc_ref.shape, F32)
        p_ref[1] = jnp.zeros(p_ref.shape[1:], BF16)
        alpha_ref[1] = jnp.ones(alpha_ref.shape[1:], F32)

    def apply_pv(src_slot, v_ref):
        v = v_ref[...]
        for r in head_rows:
            acc_ref[r, :] = alpha_ref[src_slot, r, :] * acc_ref[r, :] + jnp.dot(
                p_ref[src_slot, r, :], v, preferred_element_type=F32)

    def update(causal):
        apply_pv(1 - slot, vprev_ref)
        k = k_ref[...]
        if causal:
            tq = qb * SEL_Q_TILE + lax.broadcasted_iota(jnp.int32, (SEL_Q_TILE, K_TILE), 0)
            kpos = kt * K_TILE + lax.broadcasted_iota(jnp.int32, (SEL_Q_TILE, K_TILE), 1)
            cbias = jnp.where(kpos <= tq, 0.0, NEG_INF)
        for r in head_rows:
            s = _dot_nt(qx_ref[r, :], k)
            if causal:
                s = s + cbias
            blk = [s[:, c * LANES:(c + 1) * LANES] for c in range(n_lane_blk)]
            mx = blk[0]
            for c in range(1, n_lane_blk):
                mx = jnp.maximum(mx, blk[c])
            m_old = m_ref[r, :]
            m_new = jnp.maximum(m_old, jnp.max(mx, axis=-1, keepdims=True))
            alpha = jnp.exp2(m_old - m_new)
            ps = [jnp.exp2(b - m_new) for b in blk]
            l_add = ps[0]
            for c in range(1, n_lane_blk):
                l_add = l_add + ps[c]
            l_ref[r, :] = alpha * l_ref[r, :] + l_add
            m_ref[r, :] = m_new
            alpha_ref[slot, r, :] = alpha
            p_ref[slot, r, :] = jnp.concatenate([x.astype(BF16) for x in ps], axis=1)

    @pl.when(kt < last_kt)
    def _():
        update(False)

    @pl.when(kt == last_kt)
    def _():
        update(True)
        apply_pv(slot, vlast_ref)
        gates = g_ref[...]
        for z, r in enumerate(head_rows):
            cols = slice(z * HEAD_DIM, (z + 1) * HEAD_DIM)
            g_s = gates[:, z * N_NSA_BRANCHES + 1:z * N_NSA_BRANCHES + 2]
            l_row = jnp.sum(l_ref[r, :], axis=-1, keepdims=True)
            o_s = acc_ref[r, :] / jnp.maximum(l_row, 1e-30)
            o_ref[:, cols] = (ocw_ref[:, cols] + g_s * o_s).astype(o_ref.dtype)


def _sel_attention(q, sel_bias, ks_ext, vs, ocw, gates):
    s = q.shape[0]
    n_qb = s // SEL_Q_TILE
    grp_w = Q_PER_KV * HEAD_DIM
    qb_of, kt_of = [], []
    for qb in range(n_qb):
        for kt in range((qb * SEL_Q_TILE) // K_TILE + 1):
            qb_of.append(qb)
            kt_of.append(kt)
    qb_arr = jnp.asarray(np.asarray(qb_of, np.int32))
    kt_arr = jnp.asarray(np.asarray(kt_of, np.int32))
    grid_spec = pltpu.PrefetchScalarGridSpec(
        num_scalar_prefetch=2,
        grid=(N_KV_GROUPS, len(qb_of)),
        in_specs=[pl.BlockSpec((SEL_Q_TILE, grp_w), lambda g, i, qbr, ktr: (qbr[i], g)),
                  pl.BlockSpec((None, SEL_Q_TILE, LANES), lambda g, i, qbr, ktr: (g, qbr[i], 0)),
                  pl.BlockSpec((K_TILE, 2 * HEAD_DIM), lambda g, i, qbr, ktr: (ktr[i], g)),
                  pl.BlockSpec((K_TILE, HEAD_DIM), lambda g, i, qbr, ktr: (jnp.maximum(ktr[i] - 1, 0), g)),
                  pl.BlockSpec((K_TILE, HEAD_DIM),
                               lambda g, i, qbr, ktr: ((qbr[i] * SEL_Q_TILE) // K_TILE, g)),
                  pl.BlockSpec((SEL_Q_TILE, grp_w), lambda g, i, qbr, ktr: (qbr[i], g)),
                  pl.BlockSpec((SEL_Q_TILE, LANES), lambda g, i, qbr, ktr: (qbr[i], g))],
        out_specs=pl.BlockSpec((SEL_Q_TILE, grp_w), lambda g, i, qbr, ktr: (qbr[i], g)),
        scratch_shapes=[pltpu.VMEM((SEL_ROWS, 2 * HEAD_DIM), BF16),
                        pltpu.VMEM((SEL_ROWS, LANES), F32),
                        pltpu.VMEM((SEL_ROWS, LANES), F32),
                        pltpu.VMEM((SEL_ROWS, HEAD_DIM), F32),
                        pltpu.VMEM((2, SEL_ROWS, K_TILE), BF16),
                        pltpu.VMEM((2, SEL_ROWS, LANES), F32)],
    )
    return pl.pallas_call(
        _sel_attn_kernel,
        out_shape=jax.ShapeDtypeStruct((s, ATTN_WIDTH), BF16),
        grid_spec=grid_spec,
        compiler_params=_params("parallel", "arbitrary"),
        name="sel_attention",
    )(qb_arr, kt_arr, q, sel_bias, ks_ext, vs, vs, ocw, gates)


def _rnn_kernel(rx_ref, rxs_ref, ry_ref, rys_ref, cw_ref, cb_ref, wa_ref, ba_ref, wx_ref, bx_ref, lam_ref,
                o_ref, ext_ref, a_ref, b_ref, h_ref, *, tt, tc):
    ti = pl.program_id(1)
    n_blk = tc // RNN_BLOCK_DIM
    halo = SUBLANES

    @pl.when(ti == 0)
    def _():
        ext_ref[0:halo, :] = jnp.zeros((halo, tc), F32)
        h_ref[...] = jnp.zeros(h_ref.shape, F32)

    @pl.when(ti > 0)
    def _():
        ext_ref[0:halo, :] = ext_ref[tt:tt + halo, :]

    ext_ref[halo:halo + tt, :] = _unshift(rx_ref[...], rxs_ref[...])
    xr = cb_ref[...] + ext_ref[pl.ds(halo - (RNN_CONV - 1), tt), :] * cw_ref[0:1, :]
    for j in range(1, RNN_CONV):
        xr = xr + ext_ref[pl.ds(halo - (RNN_CONV - 1) + j, tt), :] * cw_ref[j:j + 1, :]

    sp = jnp.maximum(-lam_ref[...], 0.0) + jnp.log(1.0 + jnp.exp(-jnp.abs(lam_ref[...])))
    neg_c_sp = -RG_C * sp
    xb = xr.astype(BF16)
    for blk in range(n_blk):
        cols = slice(blk * RNN_BLOCK_DIM, (blk + 1) * RNN_BLOCK_DIM)
        xs = xb[:, cols]
        r = jax.nn.sigmoid(jnp.dot(xs, wa_ref[blk], preferred_element_type=F32) + ba_ref[:, cols])
        i = jax.nn.sigmoid(jnp.dot(xs, wx_ref[blk], preferred_element_type=F32) + bx_ref[:, cols])
        a = jnp.exp(r * neg_c_sp[:, cols])
        one_m = 1.0 - a * a
        root = jnp.where(one_m > 0.0, one_m * lax.rsqrt(one_m), 0.0)
        a_ref[:, cols] = a
        b_ref[:, cols] = root * (i * xr[:, cols])

    row = lax.broadcasted_iota(jnp.int32, (SUBLANES, tc), 0)

    def scan_rows(i, carry):
        r0 = pl.multiple_of(i * SUBLANES, SUBLANES)
        a8 = a_ref[pl.ds(r0, SUBLANES), :]
        b8 = b_ref[pl.ds(r0, SUBLANES), :]
        for d in (1, 2, 4):
            keep = row >= d
            a_sh = pltpu.roll(a8, d, axis=0)
            b_sh = pltpu.roll(b8, d, axis=0)
            b8 = jnp.where(keep, a8 * b_sh + b8, b8)
            a8 = jnp.where(keep, a8 * a_sh, a8)
        h8 = a8 * carry + b8
        b_ref[pl.ds(r0, SUBLANES), :] = h8
        return jnp.broadcast_to(h8[SUBLANES - 1:SUBLANES, :], (SUBLANES, tc))

    h_ref[...] = lax.fori_loop(0, tt // SUBLANES, scan_rows, h_ref[...])
    ry = _unshift(ry_ref[...], rys_ref[...])
    o_ref[...] = (b_ref[...] * jax.nn.gelu(ry, approximate=True)).astype(o_ref.dtype)


def _rnn_branch(proj, cw, cb, wa, ba, wx, bx, lam, *, tt, tc):
    s = proj.shape[0]
    n_cb = RNN_WIDTH // tc
    blk_per = tc // RNN_BLOCK_DIM
    rx_blk = COL_RX // tc
    ry_blk = COL_RY // tc
    per = tc // LANES
    vec = pl.BlockSpec((1, tc), lambda c, t: (0, c))
    wspec = pl.BlockSpec((blk_per, RNN_BLOCK_DIM, RNN_BLOCK_DIM), lambda c, t: (c, 0, 0))
    return pl.pallas_call(
        functools.partial(_rnn_kernel, tt=tt, tc=tc),
        out_shape=jax.ShapeDtypeStruct((s, RNN_WIDTH), BF16),
        grid=(n_cb, s // tt),
        in_specs=[pl.BlockSpec((tt, tc), lambda c, t: (t, rx_blk + c)),
                  pl.BlockSpec((tt, LANES), lambda c, t: (t, (rx_blk + c + 1) * per)),
                  pl.BlockSpec((tt, tc), lambda c, t: (t, ry_blk + c)),
                  pl.BlockSpec((tt, LANES), lambda c, t: (t, (ry_blk + c + 1) * per)),
                  pl.BlockSpec((RNN_CONV, tc), lambda c, t: (0, c)),
                  vec, wspec, vec, wspec, vec, vec],
        out_specs=pl.BlockSpec((tt, tc), lambda c, t: (t, c)),
        scratch_shapes=[pltpu.VMEM((tt + SUBLANES, tc), F32),
                        pltpu.VMEM((tt, tc), F32),
                        pltpu.VMEM((tt, tc), F32),
                        pltpu.VMEM((SUBLANES, tc), F32)],
        compiler_params=_params("parallel", "arbitrary"),
        name="rg_lru",
    )(proj, proj, proj, proj, cw, cb, wa, ba, wx, bx, lam)


def _overlap_t(s):
    n_chunks = s // CMP_STRIDE
    n_cmp = n_chunks - 1
    n_sel = s // SEL_BLOCK
    cmp_start = np.arange(n_cmp) * CMP_STRIDE
    sel_start = np.arange(n_sel) * SEL_BLOCK
    ov = np.clip(np.minimum(cmp_start[:, None] + CMP_LEN, sel_start[None, :] + SEL_BLOCK)
                 - np.maximum(cmp_start[:, None], sel_start[None, :]), 0, None) / CMP_LEN
    out = np.zeros((LANES, n_chunks), np.float32)
    out[:n_sel, :n_cmp] = ov.T
    return out


def _block_onehot(s):
    return (np.arange(s)[:, None] // SEL_BLOCK == np.arange(LANES)[None, :]).astype(np.float32)


def kernel(x, c, positions, w_cond, b_cond, w_mod, b_mod, norm_mix, norm_ffn, w_in, q_norm, k_norm, cmp_pe_k, cmp_w_k, cmp_pe_v, cmp_w_v, rnn_conv_w, rnn_conv_b, rg_w_a, rg_b_a, rg_w_x, rg_b_x, rg_lambda, w_attn_up, w_rnn_up, w_out, w_ffn_in, ffn_conv_w, ffn_conv_b, w_ffn_down):
    b, s, d = x.shape
    depth = w_in.shape[0]
    assert b == 1 and d == D_MODEL and s % 2048 == 0 and s // SEL_BLOCK <= LANES

    c_emb = _vecmat(c.reshape(1, d, 1), w_cond[None], b_cond.reshape(1, 1, -1), silu=True, tn=w_cond.shape[1])
    c_col = jnp.broadcast_to(c_emb.reshape(1, -1, 1), (depth, c_emb.shape[-1], 1))
    mod = _vecmat(c_col, w_mod, b_mod[:, None, :], silu=False, tn=2048)

    inv_freq = ROPE_THETA ** (-jnp.arange(0, ROT_DIM, 2, dtype=jnp.float32) / ROT_DIM)
    freq_row = jnp.concatenate([inv_freq, inv_freq, jnp.zeros((LANES - ROT_DIM,), F32)])[None, :]
    rope_c, rope_sa, rope_sb = _rope_tables(positions.reshape(s, 1), freq_row, tr=512)

    ovt = jnp.asarray(_overlap_t(s), BF16)
    blk_onehot = jnp.asarray(_block_onehot(s), BF16)

    w_in_b = w_in.astype(BF16)
    n_tail = w_in.shape[2] - (GATE_SRC + N_PROJ_REST)
    assert n_tail == PROJ_SHIFT
    w_tail = jnp.pad(w_in[:, :, GATE_SRC + N_PROJ_REST:], ((0, 0), (0, 0), (0, LANES - n_tail))).astype(BF16)
    w_attn_up_b = w_attn_up.astype(BF16)
    w_rnn_up_b = w_rnn_up.astype(BF16)
    w_out_b = w_out.astype(BF16)
    w_ffn_in_b = w_ffn_in.astype(BF16)
    w_ffn_down_b = w_ffn_down.astype(BF16)

    h = x.reshape(s, d)
    for l in range(depth):
        sh1, sc1, g1, sh2, sc2, g2 = [mod[l, :, i * d:(i + 1) * d] for i in range(N_MOD)]
        proj, gr_tail = _norm_matmul(h, norm_mix[l][None], sc1, sh1, w_in_b, w_tail,
                                     layer=l, n=N_PROJ, tm=512, tn=1024, out_dtype=F32)
        q, kc, ks_ext, vs, kw, vw, gates = _prep(proj, proj, rope_c, rope_sa, rope_sb,
                                                 q_norm[l][None], k_norm[l], blk_onehot, tr=256)
        k_cmp = _compress(kc, 0, cmp_pe_k[l], cmp_w_k[l].astype(BF16), k_norm[l][0:1], do_norm=True)
        v_cmp = _compress(proj, COL_VC // HEAD_DIM, cmp_pe_v[l], cmp_w_v[l].astype(BF16),
                          k_norm[l][0:1], do_norm=False)
        ocw, sel_bias = _cw_attention(q, k_cmp, v_cmp, ovt, kw, vw, gates)
        attn = _sel_attention(q, sel_bias, ks_ext, vs, ocw, gates)
        rnn = _rnn_branch(proj, rnn_conv_w[l], rnn_conv_b[l][None], rg_w_a[l].astype(BF16), rg_b_a[l][None],
                          rg_w_x[l].astype(BF16), rg_b_x[l][None], rg_lambda[l][None], tt=512, tc=1024)
        merged = _merge(attn, rnn, w_attn_up_b, w_rnn_up_b, proj, gr_tail, layer=l, tm=512, tn=1024)
        h = _matmul_residual(merged, w_out_b, h, g1, layer=l, tm=1024, tn=512)
        act = _ffn_in(h, norm_ffn[l][None], sc2, sh2, w_ffn_in_b, ffn_conv_w[l], ffn_conv_b[l][None],
                      layer=l, tm=512, tn=512)
        h = _matmul_residual(act, w_ffn_down_b, h, g2, layer=l, tm=1024, tn=256)
    return h.reshape(b, s, d)
```
